```python
import math
import jax
import jax.numpy as jnp
from jax import lax
import numpy as np

D_MODEL = 1024
BATCH = 16
SEQ = 2048
DEPTH = 4
DEC_BATCH = 8
DEC_SEQ = 4096
PAST_LEN = 128

HEAD_DIM = 64
ROPE_THETA = 10000.0
Q_BLOCK = 128
LN_EPS = 1e-5
RMS_EPS = 1e-6
NEG_INF = -1e30

A_HEADS = 4
A_HALF = HEAD_DIM // 2
A_W = A_HEADS * HEAD_DIM

B_HEADS = 6
B_Q_RANK = 256
B_KV_RANK = 128
B_NOPE = 64
B_ROPE = 32
B_V = 64
B_W = B_HEADS * B_V

C_GROUPS = ((128, 1), (512, 4), (2048, 16))
C_HPG = 2
C_HEADS = C_HPG * len(C_GROUPS)
C_W = C_HEADS * HEAD_DIM

COL_A = 3 * A_W
COL_B = B_Q_RANK + B_KV_RANK + B_ROPE
COL_C = 3 * C_W
IN_COLS = COL_A + COL_B + COL_C
MIX_W = A_W + B_W + C_W

N_GROUPS = 4
EXPERTS_PER_GROUP = 8
N_EXPERTS = N_GROUPS * EXPERTS_PER_GROUP
TOP_K_INNER = 2
D_EXPERT = 512
MOE_BLOCK = 128

DN_ALPHA = (2 * DEPTH) ** 0.25
DN_BETA = (8 * DEPTH) ** -0.25

kernel_name = 'hybrid_diff_mla_dilated_hmoe_encoder'


def rope_tables(seq, dim):
    inv_freq = 1.0 / (ROPE_THETA ** (jnp.arange(0, dim, 2, dtype=jnp.float32) / dim))
    ang = jnp.arange(seq, dtype=jnp.float32)[:, None] * inv_freq[None, :]
    return jnp.cos(ang), jnp.sin(ang)


def apply_rope(x, cos, sin):
    shape = (1, x.shape[1]) + (1,) * (x.ndim - 3) + (cos.shape[-1],)
    c = cos.reshape(shape).astype(x.dtype)
    s = sin.reshape(shape).astype(x.dtype)
    x1, x2 = jnp.split(x, 2, axis=-1)
    return jnp.concatenate([x1 * c - x2 * s, x1 * s + x2 * c], axis=-1)


def layer_norm(x, g, b):
    xf = x.astype(jnp.float32)
    mu = jnp.mean(xf, axis=-1, keepdims=True)
    var = jnp.mean(jnp.square(xf - mu), axis=-1, keepdims=True)
    return ((xf - mu) * lax.rsqrt(var + LN_EPS) * g.astype(jnp.float32) + b.astype(jnp.float32)).astype(x.dtype)


def rms_norm(x, g):
    xf = x.astype(jnp.float32)
    ms = jnp.mean(jnp.square(xf), axis=-1, keepdims=True)
    return (xf * lax.rsqrt(ms + RMS_EPS) * g.astype(jnp.float32)).astype(x.dtype)


def to_query_blocks(t):
    b, s = t.shape[:2]
    return jnp.moveaxis(t.reshape((b, s // Q_BLOCK, Q_BLOCK) + t.shape[2:]), 1, 0)


def from_query_blocks(o):
    o = jnp.moveaxis(o, 0, 1)
    return o.reshape((o.shape[0], o.shape[1] * o.shape[2]) + o.shape[3:])


def differential_attention(q, k, v, lam):
    q = q * (A_HALF ** -0.5)

    def block(qb):
        s = jnp.einsum('bqhcd,bkhcd->bchqk', qb, k, preferred_element_type=jnp.float32)
        p = jax.nn.softmax(s, axis=-1)
        w = (p[:, 0] - lam * p[:, 1]).astype(v.dtype)
        return jnp.einsum('bhqk,bkhd->bqhd', w, v)

    return from_query_blocks(lax.map(block, to_query_blocks(q)))


def latent_attention(q_nope, q_rope, k_nope, k_rope, v):
    scale = (B_NOPE + B_ROPE) ** -0.5

    def block(args):
        qn, qr = args
        s = (jnp.einsum('bqhd,bkhd->bhqk', qn, k_nope, preferred_element_type=jnp.float32)
             + jnp.einsum('bqhr,bkr->bhqk', qr, k_rope, preferred_element_type=jnp.float32)) * scale
        p = jax.nn.softmax(s, axis=-1).astype(v.dtype)
        return jnp.einsum('bhqk,bkhd->bqhd', p, v)

    return from_query_blocks(lax.map(block, (to_query_blocks(q_nope), to_query_blocks(q_rope))))


def dilated_window_attention(q, k, v, r, side):
    b, s_len, h, d = q.shape
    L = s_len // r
    nb = -(-L // side)
    lp = nb * side

    def to_sub(t):
        return jnp.swapaxes(t.reshape(b, L, r, h, d), 1, 2).reshape(b * r, L, h, d)

    def from_sub(t):
        rest = t.shape[2:]
        return jnp.swapaxes(t.reshape((b, r, L) + rest), 1, 2).reshape((b, s_len) + rest)

    qs = to_sub(q * (d ** -0.5))
    qs = jnp.pad(qs, ((0, 0), (0, lp - L), (0, 0), (0, 0))).reshape(b * r, nb, side, h, d)
    pad_kv = ((0, 0), (side, lp - L + side), (0, 0), (0, 0))

    def windows(t):
        tb = jnp.pad(to_sub(t), pad_kv).reshape(b * r, nb + 2, side, h, d)
        return jnp.concatenate([tb[:, :-2], tb[:, 1:-1], tb[:, 2:]], axis=2)

    kw, vw = windows(k), windows(v)
    s = jnp.einsum('bnqhd,bnkhd->bnhqk', qs, kw, preferred_element_type=jnp.float32)
    qi = jnp.arange(side)[:, None]
    kt = jnp.arange(3 * side)[None, :]
    rel = kt - side - qi
    kidx = jnp.arange(nb)[:, None, None] * side + kt[None] - side
    valid = (jnp.abs(rel) <= side)[None] & (kidx >= 0) & (kidx < L)
    s = jnp.where(valid[None, :, None], s, NEG_INF)
    lse = jax.nn.logsumexp(s, axis=-1)
    p = jnp.exp(s - lse[..., None]).astype(v.dtype)
    o = jnp.einsum('bnhqk,bnkhd->bnqhd', p, vw).reshape(b * r, lp, h, d)[:, :L]
    lse = jnp.swapaxes(lse, 2, 3).reshape(b * r, lp, h)[:, :L]
    return from_sub(o), from_sub(lse)


def dilated_mixture_attention(q, k, v):
    outs, lses = [], []
    for g, (window, dil) in enumerate(C_GROUPS):
        hs = slice(g * C_HPG, (g + 1) * C_HPG)
        o, l = dilated_window_attention(q[:, :, hs], k[:, :, hs], v[:, :, hs], dil, window // (2 * dil))
        outs.append(o)
        lses.append(l)
    alpha = jax.nn.softmax(jnp.stack(lses), axis=0)
    return jnp.concatenate([o * a[..., None].astype(o.dtype) for o, a in zip(outs, alpha)], axis=2)


def expert_dispatch(xf, eid, ew, w1, w3, w2):
    n, d = xf.shape
    a = n * TOP_K_INNER
    e_flat = eid.reshape(a).astype(jnp.int32)
    order = jnp.argsort(e_flat)
    e_sorted = e_flat[order]
    counts = jnp.bincount(e_flat, length=N_EXPERTS).astype(jnp.int32)
    padded = (counts + MOE_BLOCK - 1) // MOE_BLOCK * MOE_BLOCK
    pad_end = jnp.cumsum(padded)
    pad_start = pad_end - padded
    start = jnp.cumsum(counts) - counts
    dest = pad_start[e_sorted] + jnp.arange(a, dtype=jnp.int32) - start[e_sorted]
    n_blocks = -(-(a + N_EXPERTS * (MOE_BLOCK - 1)) // MOE_BLOCK)
    p_len = n_blocks * MOE_BLOCK
    slot_tok = jnp.full((p_len,), n, jnp.int32).at[dest].set((order // TOP_K_INNER).astype(jnp.int32))
    slot_w = jnp.zeros((p_len,), xf.dtype).at[dest].set(ew.reshape(a)[order].astype(xf.dtype))
    blk_start = jnp.arange(n_blocks, dtype=jnp.int32) * MOE_BLOCK
    blk_exp = jnp.minimum(jnp.searchsorted(pad_end, blk_start, side='right'), N_EXPERTS - 1)
    x_pad = jnp.concatenate([xf, jnp.zeros((1, d), xf.dtype)], axis=0)

    def run(args):
        tok, e = args
        xb = x_pad[tok]
        hid = jax.nn.silu(xb @ w1[e]) * (xb @ w3[e])
        return hid @ w2[e]

    yb = lax.map(run, (slot_tok.reshape(n_blocks, MOE_BLOCK), blk_exp)).reshape(p_len, d)
    y = jnp.zeros((n + 1, d), xf.dtype).at[slot_tok].add(yb * slot_w[:, None])
    return y[:n]


def hierarchical_moe(x, w_coarse, w_fine, w1, w3, w2):
    b, s, d = x.shape
    xf = x.reshape(b * s, d)
    cl = jnp.einsum('nd,dg->ng', xf, w_coarse).astype(jnp.float32)
    cp = jax.nn.softmax(cl, axis=-1)
    grp = jnp.argmax(cl, axis=-1)
    pg = jnp.take_along_axis(cp, grp[:, None], axis=1)[:, 0]
    fl = jnp.einsum('nd,gde->nge', xf, w_fine).astype(jnp.float32)
    fl = jnp.take_along_axis(fl, grp[:, None, None], axis=1)[:, 0]
    tv, ti = lax.top_k(fl, TOP_K_INNER)
    tw = jax.nn.softmax(tv, axis=-1) * pg[:, None]
    eid = grp[:, None] * EXPERTS_PER_GROUP + ti
    return expert_dispatch(xf, eid, tw, w1, w3, w2).reshape(b, s, d)


def encoder_layer(x, layer_idx, rope_a, rope_b, rope_c, w_in, lam_vecs, subln_g, q_norm_g, w_uq,
                  kv_norm_g, w_ukv, w_out, ln1_g, ln1_b, w_coarse, w_fine, w1, w3, w2, ln2_g, ln2_b):
    b, s, _ = x.shape
    h = x @ w_in
    h_a, h_b, h_c = jnp.split(h, [COL_A, COL_A + COL_B], axis=-1)

    qa, ka, va = jnp.split(h_a, 3, axis=-1)
    qa = apply_rope(qa.reshape(b, s, A_HEADS, 2, A_HALF), *rope_a)
    ka = apply_rope(ka.reshape(b, s, A_HEADS, 2, A_HALF), *rope_a)
    va = va.reshape(b, s, A_HEADS, HEAD_DIM)
    lam_init = 0.8 - 0.6 * math.exp(-0.3 * layer_idx)
    lv = lam_vecs.astype(jnp.float32)
    lam = jnp.exp(jnp.sum(lv[0] * lv[1])) - jnp.exp(jnp.sum(lv[2] * lv[3])) + lam_init
    oa = rms_norm(differential_attention(qa, ka, va, lam), subln_g) * (1.0 - lam_init)

    c_q, c_kv, k_rope = jnp.split(h_b, [B_Q_RANK, B_Q_RANK + B_KV_RANK], axis=-1)
    qb = (rms_norm(c_q, q_norm_g) @ w_uq).reshape(b, s, B_HEADS, B_NOPE + B_ROPE)
    q_nope, q_rope = jnp.split(qb, [B_NOPE], axis=-1)
    kvb = (rms_norm(c_kv, kv_norm_g) @ w_ukv).reshape(b, s, B_HEADS, B_NOPE + B_V)
    k_nope, vb = jnp.split(kvb, [B_NOPE], axis=-1)
    ob = latent_attention(q_nope, apply_rope(q_rope, *rope_b), k_nope, apply_rope(k_rope, *rope_b), vb)

    qc, kc, vc = [t.reshape(b, s, C_HEADS, HEAD_DIM) for t in jnp.split(h_c, 3, axis=-1)]
    oc = dilated_mixture_attention(apply_rope(qc, *rope_c), apply_rope(kc, *rope_c), vc)

    mix = jnp.concatenate([oa.reshape(b, s, A_W), ob.reshape(b, s, B_W), oc.reshape(b, s, C_W)], axis=-1) @ w_out
    x = layer_norm(DN_ALPHA * x + mix, ln1_g, ln1_b)
    x = layer_norm(DN_ALPHA * x + hierarchical_moe(x, w_coarse, w_fine, w1, w3, w2), ln2_g, ln2_b)
    return x


def setup_inputs(seed: int = 0) -> dict:
    key = jax.random.key(seed)
    ks = jax.random.split(key, 20)
    f32 = jnp.float32

    def nrm(k, shape, scale):
        return jax.random.normal(k, shape, f32) * scale

    in_scale = np.concatenate([
        np.ones(2 * A_W), np.full(A_W, DN_BETA),
        np.ones(COL_B),
        np.ones(2 * C_W), np.full(C_W, DN_BETA)]).astype(np.float32)
    ukv_scale = np.tile(np.concatenate([np.ones(B_NOPE), np.full(B_V, DN_BETA)]), B_HEADS).astype(np.float32)
    return {
        'x_prompt': nrm(ks[0], (BATCH, SEQ, D_MODEL), 1.0),
        'x_sample': nrm(ks[1], (DEC_BATCH, DEC_SEQ, D_MODEL), 1.0),
        'w_in': nrm(ks[2], (DEPTH, D_MODEL, IN_COLS), D_MODEL ** -0.5) * in_scale,
        'diff_lambda': nrm(ks[3], (DEPTH, 4, A_HALF), 0.1),
        'diff_subln': 1.0 + nrm(ks[4], (DEPTH, HEAD_DIM), 0.1),
        'mla_q_norm': 1.0 + nrm(ks[5], (DEPTH, B_Q_RANK), 0.1),
        'mla_w_uq': nrm(ks[6], (DEPTH, B_Q_RANK, B_HEADS * (B_NOPE + B_ROPE)), B_Q_RANK ** -0.5),
        'mla_kv_norm': 1.0 + nrm(ks[7], (DEPTH, B_KV_RANK), 0.1),
        'mla_w_ukv': nrm(ks[8], (DEPTH, B_KV_RANK, B_HEADS * (B_NOPE + B_V)), B_KV_RANK ** -0.5) * ukv_scale,
        'w_out': nrm(ks[9], (DEPTH, MIX_W, D_MODEL), MIX_W ** -0.5 * DN_BETA),
        'ln1_g': 1.0 + nrm(ks[10], (DEPTH, D_MODEL), 0.1),
        'ln1_b': nrm(ks[11], (DEPTH, D_MODEL), 0.02),
        'moe_w_coarse': nrm(ks[12], (DEPTH, D_MODEL, N_GROUPS), D_MODEL ** -0.5),
        'moe_w_fine': nrm(ks[13], (DEPTH, N_GROUPS, D_MODEL, EXPERTS_PER_GROUP), D_MODEL ** -0.5),
        'moe_w1': nrm(ks[14], (DEPTH, N_EXPERTS, D_MODEL, D_EXPERT), D_MODEL ** -0.5),
        'moe_w3': nrm(ks[15], (DEPTH, N_EXPERTS, D_MODEL, D_EXPERT), D_MODEL ** -0.5 * DN_BETA),
        'moe_w2': nrm(ks[16], (DEPTH, N_EXPERTS, D_EXPERT, D_MODEL), D_EXPERT ** -0.5 * DN_BETA),
        'ln2_g': 1.0 + nrm(ks[17], (DEPTH, D_MODEL), 0.1),
        'ln2_b': nrm(ks[18], (DEPTH, D_MODEL), 0.02),
    }


def reference(x_prompt, x_sample, w_in, diff_lambda, diff_subln, mla_q_norm, mla_w_uq, mla_kv_norm,
              mla_w_ukv, w_out, ln1_g, ln1_b, moe_w_coarse, moe_w_fine, moe_w1, moe_w3, moe_w2, ln2_g, ln2_b):
    def trunk(x):
        s = x.shape[1]
        rope_a = rope_tables(s, A_HALF)
        rope_b = rope_tables(s, B_ROPE)
        rope_c = rope_tables(s, HEAD_DIM)
        for l in range(DEPTH):
            x = encoder_layer(x, l, rope_a, rope_b, rope_c, w_in[l], diff_lambda[l], diff_subln[l],
                              mla_q_norm[l], mla_w_uq[l], mla_kv_norm[l], mla_w_ukv[l], w_out[l],
                              ln1_g[l], ln1_b[l], moe_w_coarse[l], moe_w_fine[l], moe_w1[l], moe_w3[l],
                              moe_w2[l], ln2_g[l], ln2_b[l])
        return x

    y_prompt = trunk(x_prompt)
    y_sample = trunk(x_sample)
    return (y_prompt, y_sample)
```

```python
import functools
import math

import jax
import jax.numpy as jnp
import numpy as np
from jax import lax
from jax.experimental import pallas as pl
from jax.experimental.pallas import tpu as pltpu

D_MODEL = 1024
DEPTH = 4
HEAD_DIM = 64
ROPE_THETA = 10000.0
LN_EPS = 1e-5
RMS_EPS = 1e-6
NEG_INF = -1e30

A_HEADS = 4
A_HALF = HEAD_DIM // 2
A_W = A_HEADS * HEAD_DIM

B_HEADS = 6
B_Q_RANK = 256
B_KV_RANK = 128
B_NOPE = 64
B_ROPE = 32
B_V = 64
B_W = B_HEADS * B_V
B_SLOT = 128

C_GROUPS = ((128, 1), (512, 4), (2048, 16))
C_HPG = 2
C_GW = C_HPG * HEAD_DIM
C_SIDE = 64
C_BLOCK = 128

COL_A = 3 * A_W
COL_B = B_Q_RANK + B_KV_RANK + B_ROPE

N_GROUPS = 4
EXPERTS_PER_GROUP = 8
N_EXPERTS = N_GROUPS * EXPERTS_PER_GROUP
TOP_K_INNER = 2
D_EXPERT = 512

DN_ALPHA = (2 * DEPTH) ** 0.25

LOG2E = 1.4426950408889634
SCALE_A = (A_HALF ** -0.5) * LOG2E
SCALE_B = ((B_NOPE + B_ROPE) ** -0.5) * LOG2E
SCALE_C = (HEAD_DIM ** -0.5) * LOG2E

LANES = 128
ROW_BLOCK = 256
Q_BLOCK = 256
K_CHUNK = 512
MOE_ROWS = 256
VMEM_LIMIT = 48 * 1024 * 1024

BF16 = jnp.bfloat16
F32 = jnp.float32

_NT = (((1,), (1,)), ((), ()))


def _dot(a, b):
    return jnp.dot(a, b, preferred_element_type=F32)


def _dot_nt(a, b):
    return lax.dot_general(a, b, _NT, preferred_element_type=F32)


def _params(*sem):
    return pltpu.CompilerParams(dimension_semantics=sem, vmem_limit_bytes=VMEM_LIMIT)


def _rope(h, cos, sin_signed, half):
    width = h.shape[1]
    lane = lax.broadcasted_iota(jnp.int32, h.shape, 1)
    first = (lane % (2 * half)) < half
    partner = jnp.where(first, pltpu.roll(h, width - half, 1), pltpu.roll(h, half, 1))
    return h * cos + partner * sin_signed


def _rms(x, g):
    return x * lax.rsqrt(jnp.mean(x * x, axis=1, keepdims=True) + RMS_EPS) * g


def _in_proj_kernel(x_ref, tab_ref, wa_ref, wb_ref, wc_ref, wuq_ref, wuk_ref, wuv_ref, qn_ref, kvn_ref,
                    qa_ref, ka_ref, va_ref, qb_ref, kb_ref, vb_ref, c0_ref, c1_ref, c2_ref):
    xb = x_ref[...].astype(BF16)
    cos_a, sin_a = tab_ref[:, 0:256], tab_ref[:, 256:512]
    cos_b, sin_b = tab_ref[:, 512:640], tab_ref[:, 640:768]
    cos_c, sin_c = tab_ref[:, 768:896], tab_ref[:, 896:1024]

    ha = _dot(xb, wa_ref[...])
    qa_ref[...] = (_rope(ha[:, 0:A_W], cos_a, sin_a, A_HALF // 2) * SCALE_A).astype(BF16)
    ka_ref[...] = _rope(ha[:, A_W:2 * A_W], cos_a, sin_a, A_HALF // 2).astype(BF16)
    va_ref[...] = ha[:, 2 * A_W:3 * A_W].astype(BF16)

    hb = _dot(xb, wb_ref[...])
    cq = _rms(hb[:, 0:B_Q_RANK], qn_ref[...]).astype(BF16)
    ckv = _rms(hb[:, B_Q_RANK:B_Q_RANK + B_KV_RANK], kvn_ref[...]).astype(BF16)
    qb = _dot(cq, wuq_ref[...])
    kb = _dot(ckv, wuk_ref[...])
    vb_ref[...] = _dot(ckv, wuv_ref[...]).astype(BF16)
    k_rope = _rope(hb[:, 384:512], cos_b, sin_b, B_ROPE // 2)
    for h in range(B_HEADS):
        sl = slice(B_SLOT * h, B_SLOT * (h + 1))
        qb_ref[:, sl] = (_rope(qb[:, sl], cos_b, sin_b, B_ROPE // 2) * SCALE_B).astype(BF16)
        kb_ref[:, sl] = (kb[:, sl] + k_rope).astype(BF16)

    hc = _dot(xb, wc_ref[...])
    for g, c_ref in enumerate((c0_ref, c1_ref, c2_ref)):
        base = 3 * C_GW * g
        c_ref[:, 0:C_GW] = (_rope(hc[:, base:base + C_GW], cos_c, sin_c, HEAD_DIM // 2) * SCALE_C).astype(BF16)
        c_ref[:, C_GW:2 * C_GW] = _rope(hc[:, base + C_GW:base + 2 * C_GW], cos_c, sin_c,
                                        HEAD_DIM // 2).astype(BF16)
        c_ref[:, 2 * C_GW:3 * C_GW] = hc[:, base + 2 * C_GW:base + 3 * C_GW].astype(BF16)


def _in_proj(x, tab, lw, seq):
    n = x.shape[0]
    tm = ROW_BLOCK
    nrep = seq // tm
    row = lambda w: pl.BlockSpec((tm, w), lambda i: (i, 0))
    full = lambda a: pl.BlockSpec(a.shape, lambda i: (0,) * a.ndim)
    weights = (lw['w_a'], lw['w_b'], lw['w_c'], lw['w_uq'], lw['w_uk'], lw['w_uv'], lw['q_norm'], lw['kv_norm'])
    out_w = (A_W, A_W, A_W, B_HEADS * B_SLOT, B_HEADS * B_SLOT, B_W, 3 * C_GW, 3 * C_GW, 3 * C_GW)
    return pl.pallas_call(
        _in_proj_kernel,
        grid=(n // tm,),
        in_specs=[row(D_MODEL), pl.BlockSpec((tm, 1024), lambda i: (i % nrep, 0))] + [full(w) for w in weights],
        out_specs=[row(w) for w in out_w],
        out_shape=[jax.ShapeDtypeStruct((n, w), BF16) for w in out_w],
        compiler_params=_params("parallel"),
        name="in_proj",
    )(x, tab, *weights)


def _attn_a_kernel(sc_ref, q_ref, k_ref, v_ref, g_ref, o_ref, s_scr):
    lam = sc_ref[0]
    post = sc_ref[1]
    tq = q_ref.shape[0]
    nkc = k_ref.shape[0] // K_CHUNK
    q = q_ref[...]
    lane = lax.broadcasted_iota(jnp.int32, (tq, A_W), 1)
    out = jnp.zeros((tq, A_W), F32)
    for h in range(A_HEADS):
        maxes = []
        for c in range(2):
            lo = A_HALF * (2 * h + c)
            qm = jnp.where((lane >= lo) & (lane < lo + A_HALF), q, jnp.zeros_like(q))

            def scores(kc, m, qm=qm, c=c):
                r0 = pl.multiple_of(kc * K_CHUNK, K_CHUNK)
                s = _dot_nt(qm, k_ref[pl.ds(r0, K_CHUNK), :])
                s_scr[c, kc] = s
                return jnp.maximum(m, jnp.max(s, axis=1, keepdims=True))

            maxes.append(lax.fori_loop(0, nkc, scores, jnp.full((tq, 1), -jnp.inf, F32)))

        def probs(kc, carry):
            l1, l2 = carry
            p1 = jnp.exp2(s_scr[0, kc] - maxes[0])
            p2 = jnp.exp2(s_scr[1, kc] - maxes[1])
            s_scr[0, kc] = p1
            s_scr[1, kc] = p2
            return l1 + jnp.sum(p1, axis=1, keepdims=True), l2 + jnp.sum(p2, axis=1, keepdims=True)

        zero = jnp.zeros((tq, 1), F32)
        l1, l2 = lax.fori_loop(0, nkc, probs, (zero, zero))
        inv1 = 1.0 / l1
        gamma = lam * l1 / l2

        def weighted(kc, acc):
            r0 = pl.multiple_of(kc * K_CHUNK, K_CHUNK)
            w = (s_scr[0, kc] - gamma * s_scr[1, kc]).astype(BF16)
            return acc + _dot(w, v_ref[pl.ds(r0, K_CHUNK), :])

        acc = lax.fori_loop(0, nkc, weighted, jnp.zeros((tq, A_W), F32))
        head = (lane >= HEAD_DIM * h) & (lane < HEAD_DIM * (h + 1))
        oh = jnp.where(head, acc * inv1, 0.0)
        ms = jnp.sum(oh * oh, axis=1, keepdims=True) * (1.0 / HEAD_DIM)
        out = out + oh * lax.rsqrt(ms + RMS_EPS)
    o_ref[...] = (out * g_ref[...] * post).astype(BF16)


def _attn_a(sc, q, k, v, g, batch, seq):
    n = q.shape[0]
    tq = Q_BLOCK
    nq = seq // tq
    return pl.pallas_call(
        _attn_a_kernel,
        grid=(batch, nq),
        in_specs=[pl.BlockSpec(memory_space=pltpu.SMEM),
                  pl.BlockSpec((tq, A_W), lambda b, i: (b * nq + i, 0)),
                  pl.BlockSpec((seq, A_W), lambda b, i: (b, 0)),
                  pl.BlockSpec((seq, A_W), lambda b, i: (b, 0)),
                  pl.BlockSpec((1, A_W), lambda b, i: (0, 0))],
        out_specs=pl.BlockSpec((tq, A_W), lambda b, i: (b * nq + i, 0)),
        out_shape=jax.ShapeDtypeStruct((n, A_W), BF16),
        scratch_shapes=[pltpu.VMEM((2, seq // K_CHUNK, tq, K_CHUNK), F32)],
        compiler_params=_params("parallel", "parallel"),
        name="attn_a",
    )(sc, q, k, v, g)


def _attn_b_kernel(q_ref, k_ref, v_ref, o_ref, s_scr):
    tq = q_ref.shape[0]
    nkc = k_ref.shape[0] // K_CHUNK
    lane = lax.broadcasted_iota(jnp.int32, (tq, LANES), 1)
    pair = None
    for h in range(B_HEADS):
        sl = slice(B_SLOT * h, B_SLOT * (h + 1))
        vs = slice(LANES * (h // 2), LANES * (h // 2 + 1))
        qh = q_ref[:, sl]

        def scores(kc, m, qh=qh, sl=sl):
            r0 = pl.multiple_of(kc * K_CHUNK, K_CHUNK)
            s = _dot_nt(qh, k_ref[pl.ds(r0, K_CHUNK), sl])
            s_scr[kc] = s
            return jnp.maximum(m, jnp.max(s, axis=1, keepdims=True))

        m = lax.fori_loop(0, nkc, scores, jnp.full((tq, 1), -jnp.inf, F32))

        def weighted(kc, carry, m=m, vs=vs):
            l, acc = carry
            r0 = pl.multiple_of(kc * K_CHUNK, K_CHUNK)
            p = jnp.exp2(s_scr[kc] - m)
            acc = acc + _dot(p.astype(BF16), v_ref[pl.ds(r0, K_CHUNK), vs])
            return l + jnp.sum(p, axis=1, keepdims=True), acc

        l, acc = lax.fori_loop(0, nkc, weighted, (jnp.zeros((tq, 1), F32), jnp.zeros((tq, LANES), F32)))
        oh = acc * (1.0 / l)
        if h % 2 == 0:
            pair = oh
        else:
            o_ref[:, vs] = jnp.where(lane < B_V, pair, oh).astype(BF16)


def _attn_b(q, k, v, batch, seq):
    n = q.shape[0]
    tq = Q_BLOCK
    nq = seq // tq
    wq = B_HEADS * B_SLOT
    return pl.pallas_call(
        _attn_b_kernel,
        grid=(batch, nq),
        in_specs=[pl.BlockSpec((tq, wq), lambda b, i: (b * nq + i, 0)),
                  pl.BlockSpec((seq, wq), lambda b, i: (b, 0)),
                  pl.BlockSpec((seq, B_W), lambda b, i: (b, 0))],
        out_specs=pl.BlockSpec((tq, B_W), lambda b, i: (b * nq + i, 0)),
        out_shape=jax.ShapeDtypeStruct((n, B_W), BF16),
        scratch_shapes=[pltpu.VMEM((seq // K_CHUNK, tq, K_CHUNK), F32)],
        compiler_params=_params("parallel", "parallel"),
        name="attn_b",
    )(q, k, v)


def _attn_c_kernel(q_ref, kp_ref, ko_ref, kn_ref, vp_ref, vo_ref, vn_ref, o_ref, lse_ref, *, sub_len):
    i = pl.program_id(2)
    tb = C_BLOCK
    q = q_ref[...]
    kcat = jnp.concatenate([kp_ref[...], ko_ref[...], kn_ref[...]], axis=0)
    vcat = jnp.concatenate([vp_ref[...], vo_ref[...], vn_ref[...]], axis=0)
    qpos = i * tb + lax.broadcasted_iota(jnp.int32, (tb, 3 * tb), 0)
    kpos = (i - 1) * tb + lax.broadcasted_iota(jnp.int32, (tb, 3 * tb), 1)
    valid = (jnp.abs(kpos - qpos) <= C_SIDE) & (kpos >= 0) & (kpos < sub_len)
    lane = lax.broadcasted_iota(jnp.int32, (tb, C_GW), 1)
    o = jnp.zeros((tb, C_GW), F32)
    lse = jnp.zeros((tb, C_GW), F32)
    for hh in range(C_HPG):
        head = (lane >= HEAD_DIM * hh) & (lane < HEAD_DIM * (hh + 1))
        qm = jnp.where(head, q, jnp.zeros_like(q))
        s = jnp.where(valid, _dot_nt(qm, kcat), NEG_INF)
        m = jnp.max(s, axis=1, keepdims=True)
        p = jnp.exp2(s - m)
        l = jnp.sum(p, axis=1, keepdims=True)
        oh = _dot(p.astype(BF16), vcat) * (1.0 / l)
        o = jnp.where(head, oh, o)
        lse = jnp.where(head, m + jnp.log(l) * LOG2E, lse)
    o_ref[...] = o
    lse_ref[...] = lse


def _attn_c(c, batch, seq, dil):
    n = c.shape[0]
    sub_len = seq // dil
    nblk = sub_len // C_BLOCK
    c3 = c.reshape(batch, sub_len, dil * 3 * C_GW)
    blk = (None, C_BLOCK, C_GW)
    prev = lambda i: jnp.maximum(i - 1, 0)
    nxt = lambda i: jnp.minimum(i + 1, nblk - 1)
    specs = [pl.BlockSpec(blk, lambda b, j, i: (b, i, 3 * j))]
    for col in (1, 2):
        specs += [pl.BlockSpec(blk, lambda b, j, i, col=col: (b, prev(i), 3 * j + col)),
                  pl.BlockSpec(blk, lambda b, j, i, col=col: (b, i, 3 * j + col)),
                  pl.BlockSpec(blk, lambda b, j, i, col=col: (b, nxt(i), 3 * j + col))]
    out_spec = pl.BlockSpec(blk, lambda b, j, i: (b, i, j))
    o, lse = pl.pallas_call(
        functools.partial(_attn_c_kernel, sub_len=sub_len),
        grid=(batch, dil, nblk),
        in_specs=specs,
        out_specs=[out_spec, out_spec],
        out_shape=[jax.ShapeDtypeStruct((batch, sub_len, dil * C_GW), F32)] * 2,
        compiler_params=_params("parallel", "parallel", "parallel"),
        name=f"attn_c_d{dil}",
    )(c3, c3, c3, c3, c3, c3, c3)
    return o.reshape(n, C_GW), lse.reshape(n, C_GW)


def _layer_norm(z, g, b):
    mu = jnp.mean(z, axis=1, keepdims=True)
    zc = z - mu
    var = jnp.mean(zc * zc, axis=1, keepdims=True)
    return zc * lax.rsqrt(var + LN_EPS) * g + b


def _first_index(hit, lane):
    return jnp.min(jnp.where(hit, lane, LANES), axis=1, keepdims=True)


def _out_proj_kernel(x_ref, oa_ref, ob_ref, oc0_ref, oc1_ref, oc2_ref, l0_ref, l1_ref, l2_ref,
                     wout_ref, g_ref, b_ref, wr_ref, x1_ref, ri_ref, rw_ref):
    la, lb, lc = l0_ref[...], l1_ref[...], l2_ref[...]
    mx = jnp.maximum(jnp.maximum(la, lb), lc)
    ea, eb, ec = jnp.exp2(la - mx), jnp.exp2(lb - mx), jnp.exp2(lc - mx)
    inv = 1.0 / (ea + eb + ec)
    mix = jnp.concatenate(
        [oa_ref[...], ob_ref[...],
         (oc0_ref[...] * (ea * inv)).astype(BF16),
         (oc1_ref[...] * (eb * inv)).astype(BF16),
         (oc2_ref[...] * (ec * inv)).astype(BF16)], axis=1)
    x1 = _layer_norm(DN_ALPHA * x_ref[...] + _dot(mix, wout_ref[...]), g_ref[...], b_ref[...])
    x1_ref[...] = x1

    logits = jnp.dot(x1, wr_ref[...], preferred_element_type=F32, precision=lax.Precision.HIGHEST)
    lane = lax.broadcasted_iota(jnp.int32, logits.shape, 1)
    ninf = -jnp.inf
    cl = jnp.where(lane < N_GROUPS, logits, ninf)
    cmax = jnp.max(cl, axis=1, keepdims=True)
    grp = _first_index(cl == cmax, lane)
    pg = 1.0 / jnp.sum(jnp.exp(cl - cmax), axis=1, keepdims=True)
    lo = N_GROUPS + EXPERTS_PER_GROUP * grp
    fl = jnp.where((lane >= lo) & (lane < lo + EXPERTS_PER_GROUP), logits, ninf)
    v1 = jnp.max(fl, axis=1, keepdims=True)
    i1 = _first_index(fl == v1, lane)
    fl2 = jnp.where(lane == i1, ninf, fl)
    v2 = jnp.max(fl2, axis=1, keepdims=True)
    i2 = _first_index(fl2 == v2, lane)
    e21 = jnp.exp(v2 - v1)
    t1 = pg / (1.0 + e21)
    t2 = t1 * e21
    ri_ref[...] = jnp.where(lane == 0, i1 - N_GROUPS, jnp.where(lane == 1, i2 - N_GROUPS, 0))
    rw_ref[...] = jnp.where(lane == 0, t1, jnp.where(lane == 1, t2, 0.0))


def _out_proj(x, oa, ob, ocs, lses, lw):
    n = x.shape[0]
    tm = ROW_BLOCK
    row = lambda w: pl.BlockSpec((tm, w), lambda i: (i, 0))
    full = lambda a: pl.BlockSpec(a.shape, lambda i: (0,) * a.ndim)
    weights = (lw['w_out'], lw['ln1_g'], lw['ln1_b'], lw['w_router'])
    return pl.pallas_call(
        _out_proj_kernel,
        grid=(n // tm,),
        in_specs=[row(D_MODEL), row(A_W), row(B_W)] + [row(C_GW)] * 6 + [full(w) for w in weights],
        out_specs=[row(D_MODEL), row(LANES), row(LANES)],
        out_shape=[jax.ShapeDtypeStruct((n, D_MODEL), F32),
                   jax.ShapeDtypeStruct((n, LANES), jnp.int32),
                   jax.ShapeDtypeStruct((n, LANES), F32)],
        compiler_params=_params("parallel"),
        name="out_proj",
    )(x, oa, ob, *ocs, *lses, *weights)


def _gather_rows(idx_vmem, idx_smem, src_hbm, dst, sem_i, sem_g):
    rows = dst.shape[0]
    cp = pltpu.make_async_copy(idx_vmem, idx_smem, sem_i)
    cp.start()
    cp.wait()

    def issue(r, carry):
        pltpu.make_async_copy(src_hbm.at[pl.ds(idx_smem[r], 1)], dst.at[pl.ds(r, 1)], sem_g).start()
        return carry

    lax.fori_loop(0, rows, issue, 0, unroll=8)

    def drain(r, carry):
        pltpu.make_async_copy(src_hbm.at[pl.ds(0, 1)], dst.at[pl.ds(r, 1)], sem_g).wait()
        return carry

    lax.fori_loop(0, rows, drain, 0, unroll=8)


def _expert_kernel(be_ref, tok_ref, x_hbm, w1_ref, w3_ref, w2_ref, y_ref, idx_smem, xg, sem_i, sem_g):
    del be_ref
    _gather_rows(tok_ref.at[0, 0], idx_smem, x_hbm, xg, sem_i, sem_g)
    xb = xg[...].astype(BF16)
    h1 = _dot(xb, w1_ref[...])
    h3 = _dot(xb, w3_ref[...])
    hid = (h1 * (1.0 / (1.0 + jnp.exp(-h1))) * h3).astype(BF16)
    y_ref[...] = _dot(hid, w2_ref[...])


def _experts(blk_exp, slot_tok, x1, lw):
    n_blocks = blk_exp.shape[0]
    tb = MOE_ROWS
    grid_spec = pltpu.PrefetchScalarGridSpec(
        num_scalar_prefetch=1,
        grid=(n_blocks,),
        in_specs=[pl.BlockSpec((1, 1, tb), lambda i, be: (i, 0, 0)),
                  pl.BlockSpec(memory_space=pl.ANY),
                  pl.BlockSpec((None, D_MODEL, D_EXPERT), lambda i, be: (be[i], 0, 0)),
                  pl.BlockSpec((None, D_MODEL, D_EXPERT), lambda i, be: (be[i], 0, 0)),
                  pl.BlockSpec((None, D_EXPERT, D_MODEL), lambda i, be: (be[i], 0, 0))],
        out_specs=pl.BlockSpec((tb, D_MODEL), lambda i, be: (i, 0)),
        scratch_shapes=[pltpu.SMEM((tb,), jnp.int32), pltpu.VMEM((tb, D_MODEL), F32),
                        pltpu.SemaphoreType.DMA, pltpu.SemaphoreType.DMA])
    return pl.pallas_call(
        _expert_kernel,
        grid_spec=grid_spec,
        out_shape=jax.ShapeDtypeStruct((n_blocks * tb, D_MODEL), F32),
        compiler_params=_params("arbitrary"),
        name="experts",
    )(blk_exp, slot_tok.reshape(n_blocks, 1, tb), x1, lw['w1'], lw['w3'], lw['w2'])


def _combine_kernel(pos_ref, x1_ref, rw_ref, yb_hbm, g_ref, b_ref, out_ref, idx_smem, yg, sem_i, sem_g):
    tm = x1_ref.shape[0]
    _gather_rows(pos_ref.at[0, 0], idx_smem, yb_hbm, yg, sem_i, sem_g)
    rw = rw_ref[...]
    y = rw[:, 0:1] * yg[0:tm, :] + rw[:, 1:2] * yg[tm:2 * tm, :]
    out_ref[...] = _layer_norm(DN_ALPHA * x1_ref[...] + y, g_ref[...], b_ref[...])


def _combine(pos, x1, rw, yb, lw):
    n = x1.shape[0]
    tm = ROW_BLOCK
    nb = n // tm
    pos_blk = pos.reshape(nb, tm, TOP_K_INNER).transpose(0, 2, 1).reshape(nb, 1, TOP_K_INNER * tm)
    full = lambda a: pl.BlockSpec(a.shape, lambda i: (0,) * a.ndim)
    return pl.pallas_call(
        _combine_kernel,
        grid=(nb,),
        in_specs=[pl.BlockSpec((1, 1, TOP_K_INNER * tm), lambda i: (i, 0, 0)),
                  pl.BlockSpec((tm, D_MODEL), lambda i: (i, 0)),
                  pl.BlockSpec((tm, LANES), lambda i: (i, 0)),
                  pl.BlockSpec(memory_space=pl.ANY),
                  full(lw['ln2_g']), full(lw['ln2_b'])],
        out_specs=pl.BlockSpec((tm, D_MODEL), lambda i: (i, 0)),
        out_shape=jax.ShapeDtypeStruct((n, D_MODEL), F32),
        scratch_shapes=[pltpu.SMEM((TOP_K_INNER * tm,), jnp.int32),
                        pltpu.VMEM((TOP_K_INNER * tm, D_MODEL), F32),
                        pltpu.SemaphoreType.DMA, pltpu.SemaphoreType.DMA],
        compiler_params=_params("arbitrary"),
        name="combine",
    )(pos_blk, x1, rw, yb, lw['ln2_g'], lw['ln2_b'])


def _dispatch_plan(eid):
    n = eid.shape[0]
    a = n * TOP_K_INNER
    tb = MOE_ROWS
    e_flat = eid.reshape(a)
    order = jnp.argsort(e_flat)
    e_sorted = e_flat[order]
    counts = jnp.bincount(e_flat, length=N_EXPERTS).astype(jnp.int32)
    padded = (counts + tb - 1) // tb * tb
    pad_end = jnp.cumsum(padded)
    pad_start = pad_end - padded
    start = jnp.cumsum(counts) - counts
    dest = pad_start[e_sorted] + jnp.arange(a, dtype=jnp.int32) - start[e_sorted]
    n_blocks = -(-(a + N_EXPERTS * (tb - 1)) // tb)
    slot_tok = jnp.zeros((n_blocks * tb,), jnp.int32).at[dest].set((order // TOP_K_INNER).astype(jnp.int32))
    blk_start = jnp.arange(n_blocks, dtype=jnp.int32) * tb
    blk_exp = jnp.minimum(jnp.searchsorted(pad_end, blk_start, side='right'), N_EXPERTS - 1).astype(jnp.int32)
    pos = jnp.zeros((a,), jnp.int32).at[order].set(dest).reshape(n, TOP_K_INNER)
    return blk_exp, slot_tok, pos


def _rope_tables(seq, dim):
    inv_freq = 1.0 / (ROPE_THETA ** (jnp.arange(0, dim, 2, dtype=F32) / dim))
    ang = jnp.arange(seq, dtype=F32)[:, None] * inv_freq[None, :]
    return jnp.cos(ang), jnp.sin(ang)


def _rope_table_block(seq):
    c16, s16 = _rope_tables(seq, A_HALF)
    c32, s32 = _rope_tables(seq, HEAD_DIM)
    ones = lambda w: jnp.ones((seq, w), F32)
    zeros = lambda w: jnp.zeros((seq, w), F32)
    cos_a = jnp.tile(jnp.concatenate([c16, c16], 1), (1, 2 * A_HEADS))
    sin_a = jnp.tile(jnp.concatenate([-s16, s16], 1), (1, 2 * A_HEADS))
    cos_b = jnp.concatenate([ones(B_NOPE), c16, c16, ones(B_SLOT - B_NOPE - B_ROPE)], 1)
    sin_b = jnp.concatenate([zeros(B_NOPE), -s16, s16, zeros(B_SLOT - B_NOPE - B_ROPE)], 1)
    cos_c = jnp.tile(jnp.concatenate([c32, c32], 1), (1, C_HPG))
    sin_c = jnp.tile(jnp.concatenate([-s32, s32], 1), (1, C_HPG))
    return jnp.concatenate([cos_a, sin_a, cos_b, sin_b, cos_c, sin_c], 1)


def _layer_weights(l, w_in, diff_lambda, diff_subln, mla_q_norm, mla_w_uq, mla_kv_norm, mla_w_ukv, w_out,
                   ln1_g, ln1_b, moe_w_coarse, moe_w_fine, w1, w3, w2, ln2_g, ln2_b):
    wi = w_in[l]
    zc = lambda rows, w: jnp.zeros((rows, w), F32)
    b0 = COL_A
    w_b = jnp.concatenate([wi[:, b0:b0 + B_Q_RANK + B_KV_RANK], zc(D_MODEL, B_NOPE),
                           wi[:, b0 + B_Q_RANK + B_KV_RANK:b0 + COL_B], zc(D_MODEL, B_SLOT - B_NOPE - B_ROPE)], 1)
    c0 = COL_A + COL_B
    cw = C_HPG * HEAD_DIM * len(C_GROUPS)
    w_c = jnp.concatenate([wi[:, c0 + part * cw + g * C_GW:c0 + part * cw + (g + 1) * C_GW]
                           for g in range(len(C_GROUPS)) for part in range(3)], 1)
    qd = B_NOPE + B_ROPE
    w_uq = jnp.concatenate([jnp.concatenate([mla_w_uq[l][:, h * qd:(h + 1) * qd], zc(B_Q_RANK, B_SLOT - qd)], 1)
                            for h in range(B_HEADS)], 1)
    kvd = B_NOPE + B_V
    w_uk = jnp.concatenate([jnp.concatenate([mla_w_ukv[l][:, h * kvd:h * kvd + B_NOPE],
                                             zc(B_KV_RANK, B_SLOT - B_NOPE)], 1) for h in range(B_HEADS)], 1)
    w_uv = jnp.concatenate([mla_w_ukv[l][:, h * kvd + B_NOPE:(h + 1) * kvd] for h in range(B_HEADS)], 1)
    w_router = jnp.concatenate(
        [moe_w_coarse[l]] + [moe_w_fine[l][g] for g in range(N_GROUPS)]
        + [zc(D_MODEL, LANES - N_GROUPS - N_EXPERTS)], 1)
    lam_init = 0.8 - 0.6 * math.exp(-0.3 * l)
    lv = diff_lambda[l].astype(F32)
    lam = jnp.exp(jnp.sum(lv[0] * lv[1])) - jnp.exp(jnp.sum(lv[2] * lv[3])) + lam_init
    return dict(
        w_a=wi[:, 0:COL_A].astype(BF16), w_b=w_b.astype(BF16), w_c=w_c.astype(BF16),
        w_uq=w_uq.astype(BF16), w_uk=w_uk.astype(BF16), w_uv=w_uv.astype(BF16),
        q_norm=mla_q_norm[l].reshape(1, B_Q_RANK), kv_norm=mla_kv_norm[l].reshape(1, B_KV_RANK),
        diff_sc=jnp.stack([lam, jnp.asarray(1.0 - lam_init, F32)]).astype(F32),
        subln=jnp.tile(diff_subln[l], A_HEADS).reshape(1, A_W),
        w_out=w_out[l].astype(BF16), ln1_g=ln1_g[l].reshape(1, D_MODEL), ln1_b=ln1_b[l].reshape(1, D_MODEL),
        w_router=w_router, w1=w1[l], w3=w3[l], w2=w2[l],
        ln2_g=ln2_g[l].reshape(1, D_MODEL), ln2_b=ln2_b[l].reshape(1, D_MODEL))


def _encoder_layer(x, tab, lw, batch, seq):
    qa, ka, va, qb, kb, vb, c0, c1, c2 = _in_proj(x, tab, lw, seq)
    oa = _attn_a(lw['diff_sc'], qa, ka, va, lw['subln'], batch, seq)
    ob = _attn_b(qb, kb, vb, batch, seq)
    ocs, lses = zip(*[_attn_c(c, batch, seq, dil) for c, (_, dil) in zip((c0, c1, c2), C_GROUPS)])
    x1, ri, rw = _out_proj(x, oa, ob, ocs, lses, lw)
    blk_exp, slot_tok, pos = _dispatch_plan(ri[:, 0:TOP_K_INNER])
    yb = _experts(blk_exp, slot_tok, x1, lw)
    return _combine(pos, x1, rw, yb, lw)


def kernel(x_prompt, x_sample, w_in, diff_lambda, diff_subln, mla_q_norm, mla_w_uq, mla_kv_norm, mla_w_ukv,
           w_out, ln1_g, ln1_b, moe_w_coarse, moe_w_fine, moe_w1, moe_w3, moe_w2, ln2_g, ln2_b):
    w1, w3, w2 = moe_w1.astype(BF16), moe_w3.astype(BF16), moe_w2.astype(BF16)
    layers = [_layer_weights(l, w_in, diff_lambda, diff_subln, mla_q_norm, mla_w_uq, mla_kv_norm, mla_w_ukv,
                             w_out, ln1_g, ln1_b, moe_w_coarse, moe_w_fine, w1, w3, w2, ln2_g, ln2_b)
              for l in range(DEPTH)]

    def trunk(x):
        batch, seq, _ = x.shape
        tab = _rope_table_block(seq)
        h = x.reshape(batch * seq, D_MODEL)
        for lw in layers:
            h = _encoder_layer(h, tab, lw, batch, seq)
        return h.reshape(batch, seq, D_MODEL)

    return (trunk(x_prompt), trunk(x_sample))
```

```python
import functools
import math

import jax
import jax.numpy as jnp
from jax import lax
from jax.experimental import pallas as pl
from jax.experimental.pallas import tpu as pltpu

D_MODEL = 1024
DEPTH = 4
HEAD_DIM = 64
ROPE_THETA = 10000.0
LN_EPS = 1e-5
RMS_EPS = 1e-6
NEG_INF = -1e30

A_HEADS = 4
A_HALF = HEAD_DIM // 2
A_W = A_HEADS * HEAD_DIM

B_HEADS = 6
B_Q_RANK = 256
B_KV_RANK = 128
B_NOPE = 64
B_ROPE = 32
B_V = 64
B_W = B_HEADS * B_V
B_SLOT = 128

C_GROUPS = ((128, 1), (512, 4), (2048, 16))
C_HPG = 2
C_GW = C_HPG * HEAD_DIM
C_SIDE = 64
C_BLOCK = 128

COL_A = 3 * A_W
COL_B = B_Q_RANK + B_KV_RANK + B_ROPE

N_GROUPS = 4
EXPERTS_PER_GROUP = 8
N_EXPERTS = N_GROUPS * EXPERTS_PER_GROUP
TOP_K_INNER = 2
D_EXPERT = 512

DN_ALPHA = (2 * DEPTH) ** 0.25

LOG2E = 1.4426950408889634
SCALE_A = (A_HALF ** -0.5) * LOG2E
SCALE_B = ((B_NOPE + B_ROPE) ** -0.5) * LOG2E
SCALE_C = (HEAD_DIM ** -0.5) * LOG2E

LANES = 128
SUBLANES = 8
ROW_BLOCK = 256
Q_BLOCK = 256
K_CHUNK = 512
MOE_ROWS = 256
GATHER_ROWS = 256
VMEM_LIMIT = 48 * 1024 * 1024

BF16 = jnp.bfloat16
F32 = jnp.float32

_NT = (((1,), (1,)), ((), ()))


def _dot(a, b):
    return jnp.dot(a, b, preferred_element_type=F32)


def _dot_nt(a, b):
    return lax.dot_general(a, b, _NT, preferred_element_type=F32)


def _params(*sem):
    return pltpu.CompilerParams(dimension_semantics=sem, vmem_limit_bytes=VMEM_LIMIT)


def _rope(h, cos, sin_signed, half):
    width = h.shape[1]
    lane = lax.broadcasted_iota(jnp.int32, h.shape, 1)
    first = (lane % (2 * half)) < half
    partner = jnp.where(first, pltpu.roll(h, width - half, 1), pltpu.roll(h, half, 1))
    return h * cos + partner * sin_signed


def _rms(x, g):
    return x * lax.rsqrt(jnp.mean(x * x, axis=1, keepdims=True) + RMS_EPS) * g


def _in_proj_kernel(x_ref, tab_ref, wa_ref, wb_ref, wc_ref, wuq_ref, wuk_ref, wuv_ref, qn_ref, kvn_ref,
                    qa_ref, ka_ref, va_ref, qb_ref, kb_ref, vb_ref, c0_ref, c1_ref, c2_ref):
    xb = x_ref[...].astype(BF16)
    cos_a, sin_a = tab_ref[:, 0:256], tab_ref[:, 256:512]
    cos_b, sin_b = tab_ref[:, 512:640], tab_ref[:, 640:768]
    cos_c, sin_c = tab_ref[:, 768:896], tab_ref[:, 896:1024]

    ha = _dot(xb, wa_ref[...])
    qa_ref[...] = (_rope(ha[:, 0:A_W], cos_a, sin_a, A_HALF // 2) * SCALE_A).astype(BF16)
    ka_ref[...] = _rope(ha[:, A_W:2 * A_W], cos_a, sin_a, A_HALF // 2).astype(BF16)
    va_ref[...] = ha[:, 2 * A_W:3 * A_W].astype(BF16)

    hb = _dot(xb, wb_ref[...])
    cq = _rms(hb[:, 0:B_Q_RANK], qn_ref[...]).astype(BF16)
    ckv = _rms(hb[:, B_Q_RANK:B_Q_RANK + B_KV_RANK], kvn_ref[...]).astype(BF16)
    qb = _dot(cq, wuq_ref[...])
    kb = _dot(ckv, wuk_ref[...])
    vb_ref[...] = _dot(ckv, wuv_ref[...]).astype(BF16)
    k_rope = _rope(hb[:, 384:512], cos_b, sin_b, B_ROPE // 2)
    for h in range(B_HEADS):
        sl = slice(B_SLOT * h, B_SLOT * (h + 1))
        qb_ref[:, sl] = (_rope(qb[:, sl], cos_b, sin_b, B_ROPE // 2) * SCALE_B).astype(BF16)
        kb_ref[:, sl] = (kb[:, sl] + k_rope).astype(BF16)

    hc = _dot(xb, wc_ref[...])
    for g, c_ref in enumerate((c0_ref, c1_ref, c2_ref)):
        base = 3 * C_GW * g
        c_ref[:, 0:C_GW] = (_rope(hc[:, base:base + C_GW], cos_c, sin_c, HEAD_DIM // 2) * SCALE_C).astype(BF16)
        c_ref[:, C_GW:2 * C_GW] = _rope(hc[:, base + C_GW:base + 2 * C_GW], cos_c, sin_c,
                                        HEAD_DIM // 2).astype(BF16)
        c_ref[:, 2 * C_GW:3 * C_GW] = hc[:, base + 2 * C_GW:base + 3 * C_GW].astype(BF16)


def _in_proj(x, tab, lw, seq):
    n = x.shape[0]
    tm = ROW_BLOCK
    nrep = seq // tm
    row = lambda w: pl.BlockSpec((tm, w), lambda i: (i, 0))
    full = lambda a: pl.BlockSpec(a.shape, lambda i: (0,) * a.ndim)
    weights = (lw['w_a'], lw['w_b'], lw['w_c'], lw['w_uq'], lw['w_uk'], lw['w_uv'], lw['q_norm'], lw['kv_norm'])
    out_w = (A_W, A_W, A_W, B_HEADS * B_SLOT, B_HEADS * B_SLOT, B_W, 3 * C_GW, 3 * C_GW, 3 * C_GW)
    return pl.pallas_call(
        _in_proj_kernel,
        grid=(n // tm,),
        in_specs=[row(D_MODEL), pl.BlockSpec((tm, 1024), lambda i: (i % nrep, 0))] + [full(w) for w in weights],
        out_specs=[row(w) for w in out_w],
        out_shape=[jax.ShapeDtypeStruct((n, w), BF16) for w in out_w],
        compiler_params=_params("parallel"),
        name="in_proj",
    )(x, tab, *weights)


def _fold_rows(x, op):
    r, c = x.shape
    return op(x.reshape(r // SUBLANES, SUBLANES, c), axis=0)


def _chunk_start(kc):
    return pl.multiple_of(kc * K_CHUNK, K_CHUNK)


def _stage_v_transposed(v_ref, vt_scr):
    @pl.when(pl.program_id(1) == 0)
    def _():
        def body(kc, carry):
            vt_scr[kc] = v_ref[pl.ds(_chunk_start(kc), K_CHUNK), :].astype(F32).T.astype(BF16)
            return carry
        lax.fori_loop(0, vt_scr.shape[0], body, 0)


def _attn_a_kernel(sc_ref, q_ref, k_ref, v_ref, g_ref, o_ref, vt_scr, s_scr):
    lam = sc_ref[0]
    post = sc_ref[1]
    tq = q_ref.shape[0]
    nkc = k_ref.shape[0] // K_CHUNK
    _stage_v_transposed(v_ref, vt_scr)
    qt = q_ref[...].astype(F32).T
    row = lax.broadcasted_iota(jnp.int32, (A_W, tq), 0)
    neg = jnp.full((SUBLANES, tq), -jnp.inf, F32)
    zero = jnp.zeros((SUBLANES, tq), F32)
    heads = []
    for h in range(A_HEADS):
        qts = []
        for c in range(2):
            lo = A_HALF * (2 * h + c)
            qts.append(jnp.where((row >= lo) & (row < lo + A_HALF), qt, 0.0).astype(BF16))

        def scores(kc, carry, qts=qts):
            m1, m2 = carry
            kr = k_ref[pl.ds(_chunk_start(kc), K_CHUNK), :]
            s1 = _dot(kr, qts[0])
            s2 = _dot(kr, qts[1])
            s_scr[0, kc] = s1
            s_scr[1, kc] = s2
            return jnp.maximum(m1, _fold_rows(s1, jnp.max)), jnp.maximum(m2, _fold_rows(s2, jnp.max))

        m1, m2 = lax.fori_loop(0, nkc, scores, (neg, neg), unroll=2)
        m1 = jnp.max(m1, axis=0, keepdims=True)
        m2 = jnp.max(m2, axis=0, keepdims=True)

        def probs(kc, carry, m1=m1, m2=m2):
            l1, l2 = carry
            p1 = jnp.exp2(s_scr[0, kc] - m1)
            p2 = jnp.exp2(s_scr[1, kc] - m2)
            s_scr[0, kc] = p1
            s_scr[1, kc] = p2
            return l1 + _fold_rows(p1, jnp.sum), l2 + _fold_rows(p2, jnp.sum)

        l1, l2 = lax.fori_loop(0, nkc, probs, (zero, zero), unroll=2)
        l1 = jnp.sum(l1, axis=0, keepdims=True)
        l2 = jnp.sum(l2, axis=0, keepdims=True)
        gamma = lam * l1 / l2

        def weighted(kc, acc, gamma=gamma, h=h):
            w = (s_scr[0, kc] - gamma * s_scr[1, kc]).astype(BF16)
            return acc + _dot(vt_scr[kc, HEAD_DIM * h:HEAD_DIM * (h + 1), :], w)

        acc = lax.fori_loop(0, nkc, weighted, jnp.zeros((HEAD_DIM, tq), F32), unroll=2)
        oh = acc * (1.0 / l1)
        ms = jnp.mean(oh * oh, axis=0, keepdims=True)
        heads.append(oh * lax.rsqrt(ms + RMS_EPS))
    out_t = jnp.concatenate(heads, axis=0)
    o_ref[...] = (out_t.T * g_ref[...] * post).astype(BF16)


def _attn_a(sc, q, k, v, g, batch, seq):
    n = q.shape[0]
    tq = Q_BLOCK
    nq = seq // tq
    nkc = seq // K_CHUNK
    return pl.pallas_call(
        _attn_a_kernel,
        grid=(batch, nq),
        in_specs=[pl.BlockSpec(memory_space=pltpu.SMEM),
                  pl.BlockSpec((tq, A_W), lambda b, i: (b * nq + i, 0)),
                  pl.BlockSpec((seq, A_W), lambda b, i: (b, 0)),
                  pl.BlockSpec((seq, A_W), lambda b, i: (b, 0)),
                  pl.BlockSpec((1, A_W), lambda b, i: (0, 0))],
        out_specs=pl.BlockSpec((tq, A_W), lambda b, i: (b * nq + i, 0)),
        out_shape=jax.ShapeDtypeStruct((n, A_W), BF16),
        scratch_shapes=[pltpu.VMEM((nkc, A_W, K_CHUNK), BF16),
                        pltpu.VMEM((2, nkc, K_CHUNK, tq), F32)],
        compiler_params=_params("arbitrary", "arbitrary"),
        name="attn_a",
    )(sc, q, k, v, g)


def _attn_b_kernel(q_ref, k_ref, v_ref, o_ref, vt_scr, s_scr):
    tq = q_ref.shape[0]
    nkc = k_ref.shape[0] // K_CHUNK
    _stage_v_transposed(v_ref, vt_scr)
    neg = jnp.full((SUBLANES, tq), -jnp.inf, F32)
    zero = jnp.zeros((SUBLANES, tq), F32)
    heads = []
    for h in range(B_HEADS):
        sl = slice(B_SLOT * h, B_SLOT * (h + 1))
        qt = q_ref[:, sl].astype(F32).T.astype(BF16)

        def scores(kc, m, qt=qt, sl=sl):
            s = _dot(k_ref[pl.ds(_chunk_start(kc), K_CHUNK), sl], qt)
            s_scr[kc] = s
            return jnp.maximum(m, _fold_rows(s, jnp.max))

        m = jnp.max(lax.fori_loop(0, nkc, scores, neg, unroll=2), axis=0, keepdims=True)

        def weighted(kc, carry, m=m, h=h):
            l, acc = carry
            p = jnp.exp2(s_scr[kc] - m)
            acc = acc + _dot(vt_scr[kc, B_V * h:B_V * (h + 1), :], p.astype(BF16))
            return l + _fold_rows(p, jnp.sum), acc

        l, acc = lax.fori_loop(0, nkc, weighted, (zero, jnp.zeros((B_V, tq), F32)), unroll=2)
        heads.append(acc * (1.0 / jnp.sum(l, axis=0, keepdims=True)))
    o_ref[...] = jnp.concatenate(heads, axis=0).T.astype(BF16)


def _attn_b(q, k, v, batch, seq):
    n = q.shape[0]
    tq = Q_BLOCK
    nq = seq // tq
    nkc = seq // K_CHUNK
    wq = B_HEADS * B_SLOT
    return pl.pallas_call(
        _attn_b_kernel,
        grid=(batch, nq),
        in_specs=[pl.BlockSpec((tq, wq), lambda b, i: (b * nq + i, 0)),
                  pl.BlockSpec((seq, wq), lambda b, i: (b, 0)),
                  pl.BlockSpec((seq, B_W), lambda b, i: (b, 0))],
        out_specs=pl.BlockSpec((tq, B_W), lambda b, i: (b * nq + i, 0)),
        out_shape=jax.ShapeDtypeStruct((n, B_W), BF16),
        scratch_shapes=[pltpu.VMEM((nkc, B_W, K_CHUNK), BF16),
                        pltpu.VMEM((nkc, K_CHUNK, tq), F32)],
        compiler_params=_params("arbitrary", "arbitrary"),
        name="attn_b",
    )(q, k, v)


def _attn_c_kernel(q_ref, kp_ref, ko_ref, kn_ref, vp_ref, vo_ref, vn_ref, o_ref, lse_ref, *, sub_len):
    i = pl.program_id(2)
    tb = C_BLOCK
    q = q_ref[...]
    kcat = jnp.concatenate([kp_ref[...], ko_ref[...], kn_ref[...]], axis=0)
    vcat = jnp.concatenate([vp_ref[...], vo_ref[...], vn_ref[...]], axis=0)
    qpos = i * tb + lax.broadcasted_iota(jnp.int32, (tb, 3 * tb), 0)
    kpos = (i - 1) * tb + lax.broadcasted_iota(jnp.int32, (tb, 3 * tb), 1)
    valid = (jnp.abs(kpos - qpos) <= C_SIDE) & (kpos >= 0) & (kpos < sub_len)
    lane = lax.broadcasted_iota(jnp.int32, (tb, C_GW), 1)
    o = jnp.zeros((tb, C_GW), F32)
    lse = jnp.zeros((tb, C_GW), F32)
    for hh in range(C_HPG):
        head = (lane >= HEAD_DIM * hh) & (lane < HEAD_DIM * (hh + 1))
        qm = jnp.where(head, q, jnp.zeros_like(q))
        s = jnp.where(valid, _dot_nt(qm, kcat), NEG_INF)
        m = jnp.max(s, axis=1, keepdims=True)
        p = jnp.exp2(s - m)
        l = jnp.sum(p, axis=1, keepdims=True)
        oh = _dot(p.astype(BF16), vcat) * (1.0 / l)
        o = jnp.where(head, oh, o)
        lse = jnp.where(head, m + jnp.log(l) * LOG2E, lse)
    o_ref[...] = o
    lse_ref[...] = lse


def _attn_c(c, batch, seq, dil):
    n = c.shape[0]
    sub_len = seq // dil
    nblk = sub_len // C_BLOCK
    c3 = c.reshape(batch, sub_len, dil * 3 * C_GW)
    blk = (None, C_BLOCK, C_GW)
    prev = lambda i: jnp.maximum(i - 1, 0)
    nxt = lambda i: jnp.minimum(i + 1, nblk - 1)
    specs = [pl.BlockSpec(blk, lambda b, j, i: (b, i, 3 * j))]
    for col in (1, 2):
        specs += [pl.BlockSpec(blk, lambda b, j, i, col=col: (b, prev(i), 3 * j + col)),
                  pl.BlockSpec(blk, lambda b, j, i, col=col: (b, i, 3 * j + col)),
                  pl.BlockSpec(blk, lambda b, j, i, col=col: (b, nxt(i), 3 * j + col))]
    out_spec = pl.BlockSpec(blk, lambda b, j, i: (b, i, j))
    o, lse = pl.pallas_call(
        functools.partial(_attn_c_kernel, sub_len=sub_len),
        grid=(batch, dil, nblk),
        in_specs=specs,
        out_specs=[out_spec, out_spec],
        out_shape=[jax.ShapeDtypeStruct((batch, sub_len, dil * C_GW), F32)] * 2,
        compiler_params=_params("parallel", "parallel", "parallel"),
        name=f"attn_c_d{dil}",
    )(c3, c3, c3, c3, c3, c3, c3)
    return o.reshape(n, C_GW), lse.reshape(n, C_GW)


def _layer_norm(z, g, b):
    mu = jnp.mean(z, axis=1, keepdims=True)
    zc = z - mu
    var = jnp.mean(zc * zc, axis=1, keepdims=True)
    return zc * lax.rsqrt(var + LN_EPS) * g + b


def _first_index(hit, lane):
    return jnp.min(jnp.where(hit, lane, LANES), axis=1, keepdims=True)


def _out_proj_kernel(x_ref, oa_ref, ob_ref, oc0_ref, oc1_ref, oc2_ref, l0_ref, l1_ref, l2_ref,
                     wout_ref, g_ref, b_ref, wr_ref, x1_ref, ri_ref, rw_ref):
    la, lb, lc = l0_ref[...], l1_ref[...], l2_ref[...]
    mx = jnp.maximum(jnp.maximum(la, lb), lc)
    ea, eb, ec = jnp.exp2(la - mx), jnp.exp2(lb - mx), jnp.exp2(lc - mx)
    inv = 1.0 / (ea + eb + ec)
    mix = jnp.concatenate(
        [oa_ref[...], ob_ref[...],
         (oc0_ref[...] * (ea * inv)).astype(BF16),
         (oc1_ref[...] * (eb * inv)).astype(BF16),
         (oc2_ref[...] * (ec * inv)).astype(BF16)], axis=1)
    x1 = _layer_norm(DN_ALPHA * x_ref[...] + _dot(mix, wout_ref[...]), g_ref[...], b_ref[...])
    x1_ref[...] = x1

    logits = jnp.dot(x1, wr_ref[...], preferred_element_type=F32, precision=lax.Precision.HIGHEST)
    lane = lax.broadcasted_iota(jnp.int32, logits.shape, 1)
    ninf = -jnp.inf
    cl = jnp.where(lane < N_GROUPS, logits, ninf)
    cmax = jnp.max(cl, axis=1, keepdims=True)
    grp = _first_index(cl == cmax, lane)
    pg = 1.0 / jnp.sum(jnp.exp(cl - cmax), axis=1, keepdims=True)
    lo = N_GROUPS + EXPERTS_PER_GROUP * grp
    fl = jnp.where((lane >= lo) & (lane < lo + EXPERTS_PER_GROUP), logits, ninf)
    v1 = jnp.max(fl, axis=1, keepdims=True)
    i1 = _first_index(fl == v1, lane)
    fl2 = jnp.where(lane == i1, ninf, fl)
    v2 = jnp.max(fl2, axis=1, keepdims=True)
    i2 = _first_index(fl2 == v2, lane)
    e21 = jnp.exp(v2 - v1)
    t1 = pg / (1.0 + e21)
    t2 = t1 * e21
    ri_ref[...] = jnp.where(lane == 0, i1 - N_GROUPS, jnp.where(lane == 1, i2 - N_GROUPS, 0))
    rw_ref[...] = jnp.where(lane == 0, t1, jnp.where(lane == 1, t2, 0.0))


def _out_proj(x, oa, ob, ocs, lses, lw):
    n = x.shape[0]
    tm = ROW_BLOCK
    row = lambda w: pl.BlockSpec((tm, w), lambda i: (i, 0))
    full = lambda a: pl.BlockSpec(a.shape, lambda i: (0,) * a.ndim)
    weights = (lw['w_out'], lw['ln1_g'], lw['ln1_b'], lw['w_router'])
    return pl.pallas_call(
        _out_proj_kernel,
        grid=(n // tm,),
        in_specs=[row(D_MODEL), row(A_W), row(B_W)] + [row(C_GW)] * 6 + [full(w) for w in weights],
        out_specs=[row(D_MODEL), row(LANES), row(LANES)],
        out_shape=[jax.ShapeDtypeStruct((n, D_MODEL), F32),
                   jax.ShapeDtypeStruct((n, LANES), jnp.int32),
                   jax.ShapeDtypeStruct((n, LANES), F32)],
        compiler_params=_params("parallel"),
        name="out_proj",
    )(x, oa, ob, *ocs, *lses, *weights)


def _row_gather_kernel(idx_ref, src_hbm, dst_hbm, idx_smem, sem_i, sem_g):
    i = pl.program_id(0)
    last = pl.num_programs(0) - 1
    rows = idx_smem.shape[0]
    cp = pltpu.make_async_copy(idx_ref.at[0, 0], idx_smem, sem_i)
    cp.start()
    cp.wait()
    base = i * (rows // SUBLANES)
    for r in range(rows):
        t = idx_smem[r]
        pltpu.make_async_copy(
            src_hbm.at[lax.shift_right_logical(t, 3), pl.ds(lax.bitwise_and(t, SUBLANES - 1), 1)],
            dst_hbm.at[base + r // SUBLANES, pl.ds(r % SUBLANES, 1)], sem_g).start()

    def drain():
        def body(r, carry):
            pltpu.make_async_copy(src_hbm.at[0, pl.ds(0, 1)], dst_hbm.at[0, pl.ds(0, 1)], sem_g).wait()
            return carry
        lax.fori_loop(0, rows, body, 0, unroll=8)

    pl.when(i > 0)(drain)
    pl.when(i == last)(drain)


def _row_gather(idx, src):
    m = idx.shape[0]
    n = src.shape[0]
    rows = GATHER_ROWS
    steps = m // rows
    out = pl.pallas_call(
        _row_gather_kernel,
        grid=(steps,),
        in_specs=[pl.BlockSpec((1, 1, rows), lambda i: (i, 0, 0)),
                  pl.BlockSpec(memory_space=pl.ANY)],
        out_specs=pl.BlockSpec(memory_space=pl.ANY),
        out_shape=jax.ShapeDtypeStruct((m // SUBLANES, SUBLANES, D_MODEL), F32),
        scratch_shapes=[pltpu.SMEM((rows,), jnp.int32), pltpu.SemaphoreType.DMA, pltpu.SemaphoreType.DMA],
        compiler_params=_params("arbitrary"),
        name="row_gather",
    )(idx.reshape(steps, 1, rows), src.reshape(n // SUBLANES, SUBLANES, D_MODEL))
    return out.reshape(m, D_MODEL)


def _expert_kernel(be_ref, x_ref, w1_ref, w3_ref, w2_ref, y_ref):
    del be_ref
    xb = x_ref[...].astype(BF16)
    h1 = _dot(xb, w1_ref[...])
    h3 = _dot(xb, w3_ref[...])
    hid = (h1 * (1.0 / (1.0 + jnp.exp(-h1))) * h3).astype(BF16)
    y_ref[...] = _dot(hid, w2_ref[...])


def _experts(blk_exp, xs, lw):
    n_blocks = blk_exp.shape[0]
    tb = MOE_ROWS
    grid_spec = pltpu.PrefetchScalarGridSpec(
        num_scalar_prefetch=1,
        grid=(n_blocks,),
        in_specs=[pl.BlockSpec((tb, D_MODEL), lambda i, be: (i, 0)),
                  pl.BlockSpec((None, D_MODEL, D_EXPERT), lambda i, be: (be[i], 0, 0)),
                  pl.BlockSpec((None, D_MODEL, D_EXPERT), lambda i, be: (be[i], 0, 0)),
                  pl.BlockSpec((None, D_EXPERT, D_MODEL), lambda i, be: (be[i], 0, 0))],
        out_specs=pl.BlockSpec((tb, D_MODEL), lambda i, be: (i, 0)))
    return pl.pallas_call(
        _expert_kernel,
        grid_spec=grid_spec,
        out_shape=jax.ShapeDtypeStruct((n_blocks * tb, D_MODEL), F32),
        compiler_params=_params("arbitrary"),
        name="experts",
    )(blk_exp, xs, lw['w1'], lw['w3'], lw['w2'])


def _combine_kernel(x1_ref, rw_ref, y0_ref, y1_ref, g_ref, b_ref, out_ref):
    rw = rw_ref[...]
    y = rw[:, 0:1] * y0_ref[...] + rw[:, 1:2] * y1_ref[...]
    out_ref[...] = _layer_norm(DN_ALPHA * x1_ref[...] + y, g_ref[...], b_ref[...])


def _combine(x1, rw, y2, lw):
    n = x1.shape[0]
    tm = ROW_BLOCK
    nb = n // tm
    full = lambda a: pl.BlockSpec(a.shape, lambda i: (0,) * a.ndim)
    return pl.pallas_call(
        _combine_kernel,
        grid=(nb,),
        in_specs=[pl.BlockSpec((tm, D_MODEL), lambda i: (i, 0)),
                  pl.BlockSpec((tm, LANES), lambda i: (i, 0)),
                  pl.BlockSpec((tm, D_MODEL), lambda i: (i, 0)),
                  pl.BlockSpec((tm, D_MODEL), lambda i: (i + nb, 0)),
                  full(lw['ln2_g']), full(lw['ln2_b'])],
        out_specs=pl.BlockSpec((tm, D_MODEL), lambda i: (i, 0)),
        out_shape=jax.ShapeDtypeStruct((n, D_MODEL), F32),
        compiler_params=_params("parallel"),
        name="combine",
    )(x1, rw, y2, y2, lw['ln2_g'], lw['ln2_b'])


def _dispatch_plan(eid):
    n = eid.shape[0]
    a = n * TOP_K_INNER
    tb = MOE_ROWS
    e_flat = eid.reshape(a)
    order = jnp.argsort(e_flat).astype(jnp.int32)
    experts = jnp.arange(N_EXPERTS, dtype=jnp.int32)
    counts = jnp.sum((e_flat[:, None] == experts[None, :]).astype(jnp.int32), axis=0)
    padded = (counts + tb - 1) // tb * tb
    pad_end = jnp.cumsum(padded)
    pad_start = pad_end - padded
    start = jnp.cumsum(counts) - counts
    shift = pad_start - start
    n_blocks = -(-(a + N_EXPERTS * (tb - 1)) // tb)
    blk_start = jnp.arange(n_blocks, dtype=jnp.int32) * tb
    blk_exp = jnp.minimum(jnp.sum((blk_start[:, None] >= pad_end[None, :]).astype(jnp.int32), axis=1),
                          N_EXPERTS - 1)
    e_slot = jnp.repeat(blk_exp, tb)
    sorted_pos = jnp.arange(n_blocks * tb, dtype=jnp.int32) - shift[e_slot]
    valid = sorted_pos < (start + counts)[e_slot]
    slot_tok = jnp.where(valid, order[jnp.clip(sorted_pos, 0, a - 1)] // TOP_K_INNER, 0)
    dest = shift[e_flat[order]] + jnp.arange(a, dtype=jnp.int32)
    pos = dest[jnp.argsort(order)].reshape(n, TOP_K_INNER)
    return blk_exp, slot_tok, pos.T.reshape(a)


def _rope_tables(seq, dim):
    inv_freq = 1.0 / (ROPE_THETA ** (jnp.arange(0, dim, 2, dtype=F32) / dim))
    ang = jnp.arange(seq, dtype=F32)[:, None] * inv_freq[None, :]
    return jnp.cos(ang), jnp.sin(ang)


def _rope_table_block(seq):
    c16, s16 = _rope_tables(seq, A_HALF)
    c32, s32 = _rope_tables(seq, HEAD_DIM)
    ones = lambda w: jnp.ones((seq, w), F32)
    zeros = lambda w: jnp.zeros((seq, w), F32)
    cos_a = jnp.tile(jnp.concatenate([c16, c16], 1), (1, 2 * A_HEADS))
    sin_a = jnp.tile(jnp.concatenate([-s16, s16], 1), (1, 2 * A_HEADS))
    cos_b = jnp.concatenate([ones(B_NOPE), c16, c16, ones(B_SLOT - B_NOPE - B_ROPE)], 1)
    sin_b = jnp.concatenate([zeros(B_NOPE), -s16, s16, zeros(B_SLOT - B_NOPE - B_ROPE)], 1)
    cos_c = jnp.tile(jnp.concatenate([c32, c32], 1), (1, C_HPG))
    sin_c = jnp.tile(jnp.concatenate([-s32, s32], 1), (1, C_HPG))
    return jnp.concatenate([cos_a, sin_a, cos_b, sin_b, cos_c, sin_c], 1)


def _layer_weights(l, w_in, diff_lambda, diff_subln, mla_q_norm, mla_w_uq, mla_kv_norm, mla_w_ukv, w_out,
                   ln1_g, ln1_b, moe_w_coarse, moe_w_fine, w1, w3, w2, ln2_g, ln2_b):
    wi = w_in[l]
    zc = lambda rows, w: jnp.zeros((rows, w), F32)
    b0 = COL_A
    w_b = jnp.concatenate([wi[:, b0:b0 + B_Q_RANK + B_KV_RANK], zc(D_MODEL, B_NOPE),
                           wi[:, b0 + B_Q_RANK + B_KV_RANK:b0 + COL_B], zc(D_MODEL, B_SLOT - B_NOPE - B_ROPE)], 1)
    c0 = COL_A + COL_B
    cw = C_HPG * HEAD_DIM * len(C_GROUPS)
    w_c = jnp.concatenate([wi[:, c0 + part * cw + g * C_GW:c0 + part * cw + (g + 1) * C_GW]
                           for g in range(len(C_GROUPS)) for part in range(3)], 1)
    qd = B_NOPE + B_ROPE
    w_uq = jnp.concatenate([jnp.concatenate([mla_w_uq[l][:, h * qd:(h + 1) * qd], zc(B_Q_RANK, B_SLOT - qd)], 1)
                            for h in range(B_HEADS)], 1)
    kvd = B_NOPE + B_V
    w_uk = jnp.concatenate([jnp.concatenate([mla_w_ukv[l][:, h * kvd:h * kvd + B_NOPE],
                                             zc(B_KV_RANK, B_SLOT - B_NOPE)], 1) for h in range(B_HEADS)], 1)
    w_uv = jnp.concatenate([mla_w_ukv[l][:, h * kvd + B_NOPE:(h + 1) * kvd] for h in range(B_HEADS)], 1)
    w_router = jnp.concatenate(
        [moe_w_coarse[l]] + [moe_w_fine[l][g] for g in range(N_GROUPS)]
        + [zc(D_MODEL, LANES - N_GROUPS - N_EXPERTS)], 1)
    lam_init = 0.8 - 0.6 * math.exp(-0.3 * l)
    lv = diff_lambda[l].astype(F32)
    lam = jnp.exp(jnp.sum(lv[0] * lv[1])) - jnp.exp(jnp.sum(lv[2] * lv[3])) + lam_init
    return dict(
        w_a=wi[:, 0:COL_A].astype(BF16), w_b=w_b.astype(BF16), w_c=w_c.astype(BF16),
        w_uq=w_uq.astype(BF16), w_uk=w_uk.astype(BF16), w_uv=w_uv.astype(BF16),
        q_norm=mla_q_norm[l].reshape(1, B_Q_RANK), kv_norm=mla_kv_norm[l].reshape(1, B_KV_RANK),
        diff_sc=jnp.stack([lam, jnp.asarray(1.0 - lam_init, F32)]).astype(F32),
        subln=jnp.tile(diff_subln[l], A_HEADS).reshape(1, A_W),
        w_out=w_out[l].astype(BF16), ln1_g=ln1_g[l].reshape(1, D_MODEL), ln1_b=ln1_b[l].reshape(1, D_MODEL),
        w_router=w_router, w1=w1[l], w3=w3[l], w2=w2[l],
        ln2_g=ln2_g[l].reshape(1, D_MODEL), ln2_b=ln2_b[l].reshape(1, D_MODEL))


def _encoder_layer(x, tab, lw, batch, seq):
    qa, ka, va, qb, kb, vb, c0, c1, c2 = _in_proj(x, tab, lw, seq)
    oa = _attn_a(lw['diff_sc'], qa, ka, va, lw['subln'], batch, seq)
    ob = _attn_b(qb, kb, vb, batch, seq)
    ocs, lses = zip(*[_attn_c(c, batch, seq, dil) for c, (_, dil) in zip((c0, c1, c2), C_GROUPS)])
    x1, ri, rw = _out_proj(x, oa, ob, ocs, lses, lw)
    blk_exp, slot_tok, pos = _dispatch_plan(ri[:, 0:TOP_K_INNER])
    xs = _row_gather(slot_tok, x1)
    ys = _experts(blk_exp, xs, lw)
    y2 = _row_gather(pos, ys)
    return _combine(x1, rw, y2, lw)


def kernel(x_prompt, x_sample, w_in, diff_lambda, diff_subln, mla_q_norm, mla_w_uq, mla_kv_norm, mla_w_ukv,
           w_out, ln1_g, ln1_b, moe_w_coarse, moe_w_fine, moe_w1, moe_w3, moe_w2, ln2_g, ln2_b):
    w1, w3, w2 = moe_w1.astype(BF16), moe_w3.astype(BF16), moe_w2.astype(BF16)
    layers = [_layer_weights(l, w_in, diff_lambda, diff_subln, mla_q_norm, mla_w_uq, mla_kv_norm, mla_w_ukv,
                             w_out, ln1_g, ln1_b, moe_w_coarse, moe_w_fine, w1, w3, w2, ln2_g, ln2_b)
              for l in range(DEPTH)]

    def trunk(x):
        batch, seq, _ = x.shape
        tab = _rope_table_block(seq)
        h = x.reshape(batch * seq, D_MODEL)
        for lw in layers:
            h = _encoder_layer(h, tab, lw, batch, seq)
        return h.reshape(batch, seq, D_MODEL)

    return (trunk(x_prompt), trunk(x_sample))
```

```python
import functools
import math

import jax
import jax.numpy as jnp
from jax import lax
from jax.experimental import pallas as pl
from jax.experimental.pallas import tpu as pltpu

D_MODEL = 1024
DEPTH = 4
HEAD_DIM = 64
ROPE_THETA = 10000.0
LN_EPS = 1e-5
RMS_EPS = 1e-6
NEG_INF = -1e30

A_HEADS = 4
A_HALF = HEAD_DIM // 2
A_W = A_HEADS * HEAD_DIM

B_HEADS = 6
B_Q_RANK = 256
B_KV_RANK = 128
B_NOPE = 64
B_ROPE = 32
B_V = 64
B_W = B_HEADS * B_V
B_SLOT = 128

C_GROUPS = ((128, 1), (512, 4), (2048, 16))
C_HPG = 2
C_GW = C_HPG * HEAD_DIM
C_SIDE = 64
C_BLOCK = 128

COL_A = 3 * A_W
COL_B = B_Q_RANK + B_KV_RANK + B_ROPE

N_GROUPS = 4
EXPERTS_PER_GROUP = 8
N_EXPERTS = N_GROUPS * EXPERTS_PER_GROUP
TOP_K_INNER = 2
D_EXPERT = 512

DN_ALPHA = (2 * DEPTH) ** 0.25

LOG2E = 1.4426950408889634
SCALE_A = (A_HALF ** -0.5) * LOG2E
SCALE_B = ((B_NOPE + B_ROPE) ** -0.5) * LOG2E
SCALE_C = (HEAD_DIM ** -0.5) * LOG2E

LANES = 128
SUBLANES = 8
ROW_BLOCK = 256
Q_BLOCK = 256
K_CHUNK = 512
MOE_ROWS = 256
VMEM_LIMIT = 48 * 1024 * 1024
ATTN_VMEM_LIMIT = 56 * 1024 * 1024

BF16 = jnp.bfloat16
F32 = jnp.float32

_NT = (((1,), (1,)), ((), ()))


def _dot(a, b):
    return jnp.dot(a, b, preferred_element_type=F32)


def _dot_nt(a, b):
    return lax.dot_general(a, b, _NT, preferred_element_type=F32)


def _params(*sem, vmem=VMEM_LIMIT):
    return pltpu.CompilerParams(dimension_semantics=sem, vmem_limit_bytes=vmem)


def _rope(h, cos, sin_signed, half):
    width = h.shape[1]
    lane = lax.broadcasted_iota(jnp.int32, h.shape, 1)
    first = (lane % (2 * half)) < half
    partner = jnp.where(first, pltpu.roll(h, width - half, 1), pltpu.roll(h, half, 1))
    return h * cos + partner * sin_signed


def _rms(x, g):
    return x * lax.rsqrt(jnp.mean(x * x, axis=1, keepdims=True) + RMS_EPS) * g


def _in_proj_kernel(x_ref, tab_ref, wa_ref, wb_ref, wc_ref, wuq_ref, wuk_ref, wuv_ref, qn_ref, kvn_ref,
                    qa_ref, ka_ref, va_ref, qb_ref, kb_ref, vb_ref, c0_ref, c1_ref, c2_ref):
    xb = x_ref[...].astype(BF16)
    cos_a, sin_a = tab_ref[:, 0:256], tab_ref[:, 256:512]
    cos_b, sin_b = tab_ref[:, 512:640], tab_ref[:, 640:768]
    cos_c, sin_c = tab_ref[:, 768:896], tab_ref[:, 896:1024]

    ha = _dot(xb, wa_ref[...])
    qa_ref[...] = (_rope(ha[:, 0:A_W], cos_a, sin_a, A_HALF // 2) * SCALE_A).astype(BF16)
    ka_ref[...] = _rope(ha[:, A_W:2 * A_W], cos_a, sin_a, A_HALF // 2).astype(BF16)
    va_ref[...] = ha[:, 2 * A_W:3 * A_W].astype(BF16)

    hb = _dot(xb, wb_ref[...])
    cq = _rms(hb[:, 0:B_Q_RANK], qn_ref[...]).astype(BF16)
    ckv = _rms(hb[:, B_Q_RANK:B_Q_RANK + B_KV_RANK], kvn_ref[...]).astype(BF16)
    qb = _dot(cq, wuq_ref[...])
    kb = _dot(ckv, wuk_ref[...])
    vb_ref[...] = _dot(ckv, wuv_ref[...]).astype(BF16)
    k_rope = _rope(hb[:, 384:512], cos_b, sin_b, B_ROPE // 2)
    for h in range(B_HEADS):
        sl = slice(B_SLOT * h, B_SLOT * (h + 1))
        qb_ref[:, sl] = (_rope(qb[:, sl], cos_b, sin_b, B_ROPE // 2) * SCALE_B).astype(BF16)
        kb_ref[:, sl] = (kb[:, sl] + k_rope).astype(BF16)

    hc = _dot(xb, wc_ref[...])
    for g, c_ref in enumerate((c0_ref, c1_ref, c2_ref)):
        base = 3 * C_GW * g
        c_ref[:, 0:C_GW] = (_rope(hc[:, base:base + C_GW], cos_c, sin_c, HEAD_DIM // 2) * SCALE_C).astype(BF16)
        c_ref[:, C_GW:2 * C_GW] = _rope(hc[:, base + C_GW:base + 2 * C_GW], cos_c, sin_c,
                                        HEAD_DIM // 2).astype(BF16)
        c_ref[:, 2 * C_GW:3 * C_GW] = hc[:, base + 2 * C_GW:base + 3 * C_GW].astype(BF16)


def _in_proj(x, tab, lw, seq):
    n = x.shape[0]
    tm = ROW_BLOCK
    nrep = seq // tm
    row = lambda w: pl.BlockSpec((tm, w), lambda i: (i, 0))
    full = lambda a: pl.BlockSpec(a.shape, lambda i: (0,) * a.ndim)
    weights = (lw['w_a'], lw['w_b'], lw['w_c'], lw['w_uq'], lw['w_uk'], lw['w_uv'], lw['q_norm'], lw['kv_norm'])
    out_w = (A_W, A_W, A_W, B_HEADS * B_SLOT, B_HEADS * B_SLOT, B_W, 3 * C_GW, 3 * C_GW, 3 * C_GW)
    return pl.pallas_call(
        _in_proj_kernel,
        grid=(n // tm,),
        in_specs=[row(D_MODEL), pl.BlockSpec((tm, 1024), lambda i: (i % nrep, 0))] + [full(w) for w in weights],
        out_specs=[row(w) for w in out_w],
        out_shape=[jax.ShapeDtypeStruct((n, w), BF16) for w in out_w],
        compiler_params=_params("parallel"),
        name="in_proj",
    )(x, tab, *weights)


def _fold_rows(x, op):
    r, c = x.shape
    return op(x.reshape(r // SUBLANES, SUBLANES, c), axis=0)


def _chunk_start(kc):
    return pl.multiple_of(kc * K_CHUNK, K_CHUNK)


def _stage_v_transposed(v_ref, vt_scr):
    @pl.when(pl.program_id(1) == 0)
    def _():
        def body(kc, carry):
            vt_scr[kc] = v_ref[pl.ds(_chunk_start(kc), K_CHUNK), :].astype(F32).T.astype(BF16)
            return carry
        lax.fori_loop(0, vt_scr.shape[0], body, 0)


def _attn_a_kernel(sc_ref, q_ref, k_ref, v_ref, g_ref, o_ref, vt_scr, s_scr):
    lam = sc_ref[0]
    post = sc_ref[1]
    tq = q_ref.shape[0]
    nkc = k_ref.shape[0] // K_CHUNK
    _stage_v_transposed(v_ref, vt_scr)
    qt = q_ref[...].astype(F32).T
    row = lax.broadcasted_iota(jnp.int32, (A_W, tq), 0)
    neg = jnp.full((SUBLANES, tq), -jnp.inf, F32)
    zero = jnp.zeros((SUBLANES, tq), F32)
    maxes, gammas, inv_l1, heads = {}, {}, {}, []
    for st in range(A_HEADS + 2):
        hs, hp, hw = st, st - 1, st - 2
        do_s, do_p, do_w = hs < A_HEADS, 0 <= hp < A_HEADS, 0 <= hw < A_HEADS
        qts = []
        if do_s:
            for c in range(2):
                lo = A_HALF * (2 * hs + c)
                qts.append(jnp.where((row >= lo) & (row < lo + A_HALF), qt, 0.0).astype(BF16))

        def body(kc, carry, qts=qts, hs=hs, hp=hp, hw=hw, do_s=do_s, do_p=do_p, do_w=do_w):
            m1, m2, l1, l2, acc = carry
            if do_s:
                kr = k_ref[pl.ds(_chunk_start(kc), K_CHUNK), :]
                s1 = _dot(kr, qts[0])
                s2 = _dot(kr, qts[1])
                s_scr[hs % 3, 0, kc] = s1
                s_scr[hs % 3, 1, kc] = s2
                m1 = jnp.maximum(m1, _fold_rows(s1, jnp.max))
                m2 = jnp.maximum(m2, _fold_rows(s2, jnp.max))
            if do_p:
                p1 = jnp.exp2(s_scr[hp % 3, 0, kc] - maxes[hp][0])
                p2 = jnp.exp2(s_scr[hp % 3, 1, kc] - maxes[hp][1])
                s_scr[hp % 3, 0, kc] = p1
                s_scr[hp % 3, 1, kc] = p2
                l1 = l1 + _fold_rows(p1, jnp.sum)
                l2 = l2 + _fold_rows(p2, jnp.sum)
            if do_w:
                w = (s_scr[hw % 3, 0, kc] - gammas[hw] * s_scr[hw % 3, 1, kc]).astype(BF16)
                acc = acc + _dot(vt_scr[kc, HEAD_DIM * hw:HEAD_DIM * (hw + 1), :], w)
            return m1, m2, l1, l2, acc

        m1, m2, l1, l2, acc = lax.fori_loop(
            0, nkc, body, (neg, neg, zero, zero, jnp.zeros((HEAD_DIM, tq), F32)), unroll=2)
        if do_s:
            maxes[hs] = (jnp.max(m1, axis=0, keepdims=True), jnp.max(m2, axis=0, keepdims=True))
        if do_p:
            l1 = jnp.sum(l1, axis=0, keepdims=True)
            l2 = jnp.sum(l2, axis=0, keepdims=True)
            gammas[hp] = lam * l1 / l2
            inv_l1[hp] = 1.0 / l1
        if do_w:
            oh = acc * inv_l1[hw]
            ms = jnp.mean(oh * oh, axis=0, keepdims=True)
            heads.append(oh * lax.rsqrt(ms + RMS_EPS))
    out_t = jnp.concatenate(heads, axis=0)
    o_ref[...] = (out_t.T * g_ref[...] * post).astype(BF16)


def _attn_a(sc, q, k, v, g, batch, seq):
    n = q.shape[0]
    tq = Q_BLOCK
    nq = seq // tq
    nkc = seq // K_CHUNK
    return pl.pallas_call(
        _attn_a_kernel,
        grid=(batch, nq),
        in_specs=[pl.BlockSpec(memory_space=pltpu.SMEM),
                  pl.BlockSpec((tq, A_W), lambda b, i: (b * nq + i, 0)),
                  pl.BlockSpec((seq, A_W), lambda b, i: (b, 0)),
                  pl.BlockSpec((seq, A_W), lambda b, i: (b, 0)),
                  pl.BlockSpec((1, A_W), lambda b, i: (0, 0))],
        out_specs=pl.BlockSpec((tq, A_W), lambda b, i: (b * nq + i, 0)),
        out_shape=jax.ShapeDtypeStruct((n, A_W), BF16),
        scratch_shapes=[pltpu.VMEM((nkc, A_W, K_CHUNK), BF16),
                        pltpu.VMEM((3, 2, nkc, K_CHUNK, tq), F32)],
        compiler_params=_params("arbitrary", "arbitrary", vmem=ATTN_VMEM_LIMIT),
        name="attn_a",
    )(sc, q, k, v, g)


def _attn_b_kernel(q_ref, k_ref, v_ref, o_ref, vt_scr, s_scr):
    tq = q_ref.shape[0]
    nkc = k_ref.shape[0] // K_CHUNK
    _stage_v_transposed(v_ref, vt_scr)
    neg = jnp.full((SUBLANES, tq), -jnp.inf, F32)
    zero = jnp.zeros((SUBLANES, tq), F32)
    acc0 = jnp.zeros((B_V, tq), F32)
    n_pairs = B_HEADS // 2
    maxes, heads = {}, []
    for st in range(n_pairs + 1):
        ps, pw = st, st - 1
        do_s, do_w = ps < n_pairs, pw >= 0
        qts = []
        if do_s:
            for j in range(2):
                sl = slice(B_SLOT * (2 * ps + j), B_SLOT * (2 * ps + j + 1))
                qts.append(q_ref[:, sl].astype(F32).T.astype(BF16))

        def body(kc, carry, qts=qts, ps=ps, pw=pw, do_s=do_s, do_w=do_w):
            ms, ls, accs = list(carry[0]), list(carry[1]), list(carry[2])
            for j in range(2):
                if do_s:
                    h = 2 * ps + j
                    s = _dot(k_ref[pl.ds(_chunk_start(kc), K_CHUNK), B_SLOT * h:B_SLOT * (h + 1)], qts[j])
                    s_scr[ps % 2, j, kc] = s
                    ms[j] = jnp.maximum(ms[j], _fold_rows(s, jnp.max))
                if do_w:
                    h = 2 * pw + j
                    p = jnp.exp2(s_scr[pw % 2, j, kc] - maxes[h])
                    accs[j] = accs[j] + _dot(vt_scr[kc, B_V * h:B_V * (h + 1), :], p.astype(BF16))
                    ls[j] = ls[j] + _fold_rows(p, jnp.sum)
            return tuple(ms), tuple(ls), tuple(accs)

        ms, ls, accs = lax.fori_loop(0, nkc, body, ((neg, neg), (zero, zero), (acc0, acc0)), unroll=2)
        for j in range(2):
            if do_s:
                maxes[2 * ps + j] = jnp.max(ms[j], axis=0, keepdims=True)
            if do_w:
                heads.append(accs[j] * (1.0 / jnp.sum(ls[j], axis=0, keepdims=True)))
    o_ref[...] = jnp.concatenate(heads, axis=0).T.astype(BF16)


def _attn_b(q, k, v, batch, seq):
    n = q.shape[0]
    tq = Q_BLOCK
    nq = seq // tq
    nkc = seq // K_CHUNK
    wq = B_HEADS * B_SLOT
    return pl.pallas_call(
        _attn_b_kernel,
        grid=(batch, nq),
        in_specs=[pl.BlockSpec((tq, wq), lambda b, i: (b * nq + i, 0)),
                  pl.BlockSpec((seq, wq), lambda b, i: (b, 0)),
                  pl.BlockSpec((seq, B_W), lambda b, i: (b, 0))],
        out_specs=pl.BlockSpec((tq, B_W), lambda b, i: (b * nq + i, 0)),
        out_shape=jax.ShapeDtypeStruct((n, B_W), BF16),
        scratch_shapes=[pltpu.VMEM((nkc, B_W, K_CHUNK), BF16),
                        pltpu.VMEM((2, 2, nkc, K_CHUNK, tq), F32)],
        compiler_params=_params("arbitrary", "arbitrary", vmem=ATTN_VMEM_LIMIT),
        name="attn_b",
    )(q, k, v)


def _attn_c_kernel(q_ref, kp_ref, ko_ref, kn_ref, vp_ref, vo_ref, vn_ref, o_ref, lse_ref, *, sub_len):
    i = pl.program_id(2)
    tb = C_BLOCK
    q = q_ref[...]
    kcat = jnp.concatenate([kp_ref[...], ko_ref[...], kn_ref[...]], axis=0)
    vcat = jnp.concatenate([vp_ref[...], vo_ref[...], vn_ref[...]], axis=0)
    qpos = i * tb + lax.broadcasted_iota(jnp.int32, (tb, 3 * tb), 0)
    kpos = (i - 1) * tb + lax.broadcasted_iota(jnp.int32, (tb, 3 * tb), 1)
    valid = (jnp.abs(kpos - qpos) <= C_SIDE) & (kpos >= 0) & (kpos < sub_len)
    lane = lax.broadcasted_iota(jnp.int32, (tb, C_GW), 1)
    o = jnp.zeros((tb, C_GW), F32)
    lse = jnp.zeros((tb, C_GW), F32)
    for hh in range(C_HPG):
        head = (lane >= HEAD_DIM * hh) & (lane < HEAD_DIM * (hh + 1))
        qm = jnp.where(head, q, jnp.zeros_like(q))
        s = jnp.where(valid, _dot_nt(qm, kcat), NEG_INF)
        m = jnp.max(s, axis=1, keepdims=True)
        p = jnp.exp2(s - m)
        l = jnp.sum(p, axis=1, keepdims=True)
        oh = _dot(p.astype(BF16), vcat) * (1.0 / l)
        o = jnp.where(head, oh, o)
        lse = jnp.where(head, m + jnp.log(l) * LOG2E, lse)
    o_ref[...] = o
    lse_ref[...] = lse


def _attn_c(c, batch, seq, dil):
    n = c.shape[0]
    sub_len = seq // dil
    nblk = sub_len // C_BLOCK
    c3 = c.reshape(batch, sub_len, dil * 3 * C_GW)
    blk = (None, C_BLOCK, C_GW)
    prev = lambda i: jnp.maximum(i - 1, 0)
    nxt = lambda i: jnp.minimum(i + 1, nblk - 1)
    specs = [pl.BlockSpec(blk, lambda b, j, i: (b, i, 3 * j))]
    for col in (1, 2):
        specs += [pl.BlockSpec(blk, lambda b, j, i, col=col: (b, prev(i), 3 * j + col)),
                  pl.BlockSpec(blk, lambda b, j, i, col=col: (b, i, 3 * j + col)),
                  pl.BlockSpec(blk, lambda b, j, i, col=col: (b, nxt(i), 3 * j + col))]
    out_spec = pl.BlockSpec(blk, lambda b, j, i: (b, i, j))
    o, lse = pl.pallas_call(
        functools.partial(_attn_c_kernel, sub_len=sub_len),
        grid=(batch, dil, nblk),
        in_specs=specs,
        out_specs=[out_spec, out_spec],
        out_shape=[jax.ShapeDtypeStruct((batch, sub_len, dil * C_GW), F32)] * 2,
        compiler_params=_params("parallel", "parallel", "parallel"),
        name=f"attn_c_d{dil}",
    )(c3, c3, c3, c3, c3, c3, c3)
    return o.reshape(n, C_GW), lse.reshape(n, C_GW)


def _layer_norm(z, g, b):
    mu = jnp.mean(z, axis=1, keepdims=True)
    zc = z - mu
    var = jnp.mean(zc * zc, axis=1, keepdims=True)
    return zc * lax.rsqrt(var + LN_EPS) * g + b


def _first_index(hit, lane):
    return jnp.min(jnp.where(hit, lane, LANES), axis=1, keepdims=True)


def _out_proj_kernel(x_ref, oa_ref, ob_ref, oc0_ref, oc1_ref, oc2_ref, l0_ref, l1_ref, l2_ref,
                     wout_ref, g_ref, b_ref, wr_ref, x1_ref, ri_ref, rw_ref):
    la, lb, lc = l0_ref[...], l1_ref[...], l2_ref[...]
    mx = jnp.maximum(jnp.maximum(la, lb), lc)
    ea, eb, ec = jnp.exp2(la - mx), jnp.exp2(lb - mx), jnp.exp2(lc - mx)
    inv = 1.0 / (ea + eb + ec)
    mix = jnp.concatenate(
        [oa_ref[...], ob_ref[...],
         (oc0_ref[...] * (ea * inv)).astype(BF16),
         (oc1_ref[...] * (eb * inv)).astype(BF16),
         (oc2_ref[...] * (ec * inv)).astype(BF16)], axis=1)
    x1 = _layer_norm(DN_ALPHA * x_ref[...] + _dot(mix, wout_ref[...]), g_ref[...], b_ref[...])
    x1_ref[...] = x1

    logits = jnp.dot(x1, wr_ref[...], preferred_element_type=F32, precision=lax.Precision.HIGHEST)
    lane = lax.broadcasted_iota(jnp.int32, logits.shape, 1)
    ninf = -jnp.inf
    cl = jnp.where(lane < N_GROUPS, logits, ninf)
    cmax = jnp.max(cl, axis=1, keepdims=True)
    grp = _first_index(cl == cmax, lane)
    pg = 1.0 / jnp.sum(jnp.exp(cl - cmax), axis=1, keepdims=True)
    lo = N_GROUPS + EXPERTS_PER_GROUP * grp
    fl = jnp.where((lane >= lo) & (lane < lo + EXPERTS_PER_GROUP), logits, ninf)
    v1 = jnp.max(fl, axis=1, keepdims=True)
    i1 = _first_index(fl == v1, lane)
    fl2 = jnp.where(lane == i1, ninf, fl)
    v2 = jnp.max(fl2, axis=1, keepdims=True)
    i2 = _first_index(fl2 == v2, lane)
    e21 = jnp.exp(v2 - v1)
    t1 = pg / (1.0 + e21)
    t2 = t1 * e21
    ri_ref[...] = jnp.where(lane == 0, i1 - N_GROUPS, jnp.where(lane == 1, i2 - N_GROUPS, 0))
    rw_ref[...] = jnp.where(lane == 0, t1, jnp.where(lane == 1, t2, 0.0))


def _out_proj(x, oa, ob, ocs, lses, lw):
    n = x.shape[0]
    tm = ROW_BLOCK
    row = lambda w: pl.BlockSpec((tm, w), lambda i: (i, 0))
    full = lambda a: pl.BlockSpec(a.shape, lambda i: (0,) * a.ndim)
    weights = (lw['w_out'], lw['ln1_g'], lw['ln1_b'], lw['w_router'])
    return pl.pallas_call(
        _out_proj_kernel,
        grid=(n // tm,),
        in_specs=[row(D_MODEL), row(A_W), row(B_W)] + [row(C_GW)] * 6 + [full(w) for w in weights],
        out_specs=[row(D_MODEL), row(LANES), row(LANES)],
        out_shape=[jax.ShapeDtypeStruct((n, D_MODEL), F32),
                   jax.ShapeDtypeStruct((n, LANES), jnp.int32),
                   jax.ShapeDtypeStruct((n, LANES), F32)],
        compiler_params=_params("parallel"),
        name="out_proj",
    )(x, oa, ob, *ocs, *lses, *weights)


def _start_row_gathers(idx_ref, idx_smem, src_hbm, dst, sem_i, sem):
    cp = pltpu.make_async_copy(idx_ref.at[0, 0], idx_smem, sem_i)
    cp.start()
    cp.wait()
    for r in range(dst.shape[0]):
        pltpu.make_async_copy(src_hbm.at[pl.ds(idx_smem[r], 1)], dst.at[pl.ds(r, 1)], sem).start()


def _prefetched_rows(idx_cur_ref, idx_next_ref, src_hbm, buf, idx_smem, sem_i, sem_g):
    i = pl.program_id(0)
    slot = lax.rem(i, 2)
    rows = buf.shape[1]

    @pl.when(i == 0)
    def _():
        _start_row_gathers(idx_cur_ref, idx_smem, src_hbm, buf.at[0], sem_i, sem_g.at[0])

    @pl.when(i + 1 < pl.num_programs(0))
    def _():
        _start_row_gathers(idx_next_ref, idx_smem, src_hbm, buf.at[1 - slot], sem_i, sem_g.at[1 - slot])

    def drain(r, carry):
        pltpu.make_async_copy(src_hbm.at[pl.ds(0, 1)], buf.at[slot, pl.ds(0, 1)], sem_g.at[slot]).wait()
        return carry

    lax.fori_loop(0, rows, drain, 0, unroll=8)
    return slot


def _index_specs(rows, steps, prefetch_args=0):
    if prefetch_args:
        return [pl.BlockSpec((1, 1, rows), lambda i, be: (i, 0, 0)),
                pl.BlockSpec((1, 1, rows), lambda i, be: (jnp.minimum(i + 1, steps - 1), 0, 0))]
    return [pl.BlockSpec((1, 1, rows), lambda i: (i, 0, 0)),
            pl.BlockSpec((1, 1, rows), lambda i: (jnp.minimum(i + 1, steps - 1), 0, 0))]


def _gather_scratch(rows):
    return [pltpu.SMEM((rows,), jnp.int32), pltpu.VMEM((2, rows, D_MODEL), F32),
            pltpu.SemaphoreType.DMA, pltpu.SemaphoreType.DMA((2,))]


def _expert_kernel(be_ref, tok_ref, tok_next_ref, x_hbm, w1_ref, w3_ref, w2_ref, y_ref,
                   idx_smem, xg, sem_i, sem_g):
    del be_ref
    slot = _prefetched_rows(tok_ref, tok_next_ref, x_hbm, xg, idx_smem, sem_i, sem_g)
    xb = xg[slot].astype(BF16)
    h1 = _dot(xb, w1_ref[...])
    h3 = _dot(xb, w3_ref[...])
    hid = (h1 * (1.0 / (1.0 + jnp.exp(-h1))) * h3).astype(BF16)
    y_ref[...] = _dot(hid, w2_ref[...])


def _experts(blk_exp, slot_tok, x1, lw):
    n_blocks = blk_exp.shape[0]
    tb = MOE_ROWS
    tok = slot_tok.reshape(n_blocks, 1, tb)
    grid_spec = pltpu.PrefetchScalarGridSpec(
        num_scalar_prefetch=1,
        grid=(n_blocks,),
        in_specs=_index_specs(tb, n_blocks, prefetch_args=1) + [
            pl.BlockSpec(memory_space=pl.ANY),
            pl.BlockSpec((None, D_MODEL, D_EXPERT), lambda i, be: (be[i], 0, 0)),
            pl.BlockSpec((None, D_MODEL, D_EXPERT), lambda i, be: (be[i], 0, 0)),
            pl.BlockSpec((None, D_EXPERT, D_MODEL), lambda i, be: (be[i], 0, 0))],
        out_specs=pl.BlockSpec((tb, D_MODEL), lambda i, be: (i, 0)),
        scratch_shapes=_gather_scratch(tb))
    return pl.pallas_call(
        _expert_kernel,
        grid_spec=grid_spec,
        out_shape=jax.ShapeDtypeStruct((n_blocks * tb, D_MODEL), F32),
        compiler_params=_params("arbitrary"),
        name="experts",
    )(blk_exp, tok, tok, x1, lw['w1'], lw['w3'], lw['w2'])


def _combine_kernel(pos_ref, pos_next_ref, x1_ref, rw_ref, ys_hbm, g_ref, b_ref, out_ref,
                    idx_smem, yg, sem_i, sem_g):
    tm = x1_ref.shape[0]
    slot = _prefetched_rows(pos_ref, pos_next_ref, ys_hbm, yg, idx_smem, sem_i, sem_g)
    rw = rw_ref[...]
    y = rw[:, 0:1] * yg[slot, 0:tm, :] + rw[:, 1:2] * yg[slot, tm:2 * tm, :]
    out_ref[...] = _layer_norm(DN_ALPHA * x1_ref[...] + y, g_ref[...], b_ref[...])


def _combine(pos, x1, rw, ys, lw):
    n = x1.shape[0]
    tm = ROW_BLOCK
    nb = n // tm
    rows = TOP_K_INNER * tm
    pos_blk = pos.reshape(nb, tm, TOP_K_INNER).transpose(0, 2, 1).reshape(nb, 1, rows)
    full = lambda a: pl.BlockSpec(a.shape, lambda i: (0,) * a.ndim)
    return pl.pallas_call(
        _combine_kernel,
        grid=(nb,),
        in_specs=_index_specs(rows, nb) + [
            pl.BlockSpec((tm, D_MODEL), lambda i: (i, 0)),
            pl.BlockSpec((tm, LANES), lambda i: (i, 0)),
            pl.BlockSpec(memory_space=pl.ANY),
            full(lw['ln2_g']), full(lw['ln2_b'])],
        out_specs=pl.BlockSpec((tm, D_MODEL), lambda i: (i, 0)),
        out_shape=jax.ShapeDtypeStruct((n, D_MODEL), F32),
        scratch_shapes=_gather_scratch(rows),
        compiler_params=_params("arbitrary"),
        name="combine",
    )(pos_blk, pos_blk, x1, rw, ys, lw['ln2_g'], lw['ln2_b'])


def _dispatch_plan(eid):
    n = eid.shape[0]
    a = n * TOP_K_INNER
    tb = MOE_ROWS
    e_flat = eid.reshape(a)
    order = jnp.argsort(e_flat).astype(jnp.int32)
    experts = jnp.arange(N_EXPERTS, dtype=jnp.int32)
    counts = jnp.sum((e_flat[:, None] == experts[None, :]).astype(jnp.int32), axis=0)
    padded = (counts + tb - 1) // tb * tb
    pad_end = jnp.cumsum(padded)
    pad_start = pad_end - padded
    start = jnp.cumsum(counts) - counts
    shift = pad_start - start
    n_blocks = -(-(a + N_EXPERTS * (tb - 1)) // tb)
    blk_start = jnp.arange(n_blocks, dtype=jnp.int32) * tb
    blk_exp = jnp.minimum(jnp.sum((blk_start[:, None] >= pad_end[None, :]).astype(jnp.int32), axis=1),
                          N_EXPERTS - 1)
    e_slot = jnp.repeat(blk_exp, tb)
    sorted_pos = jnp.arange(n_blocks * tb, dtype=jnp.int32) - shift[e_slot]
    valid = sorted_pos < (start + counts)[e_slot]
    slot_tok = jnp.where(valid, order[jnp.clip(sorted_pos, 0, a - 1)] // TOP_K_INNER, 0)
    dest = shift[e_flat[order]] + jnp.arange(a, dtype=jnp.int32)
    pos = dest[jnp.argsort(order)].reshape(n, TOP_K_INNER)
    return blk_exp, slot_tok, pos


def _rope_tables(seq, dim):
    inv_freq = 1.0 / (ROPE_THETA ** (jnp.arange(0, dim, 2, dtype=F32) / dim))
    ang = jnp.arange(seq, dtype=F32)[:, None] * inv_freq[None, :]
    return jnp.cos(ang), jnp.sin(ang)


def _rope_table_block(seq):
    c16, s16 = _rope_tables(seq, A_HALF)
    c32, s32 = _rope_tables(seq, HEAD_DIM)
    ones = lambda w: jnp.ones((seq, w), F32)
    zeros = lambda w: jnp.zeros((seq, w), F32)
    cos_a = jnp.tile(jnp.concatenate([c16, c16], 1), (1, 2 * A_HEADS))
    sin_a = jnp.tile(jnp.concatenate([-s16, s16], 1), (1, 2 * A_HEADS))
    cos_b = jnp.concatenate([ones(B_NOPE), c16, c16, ones(B_SLOT - B_NOPE - B_ROPE)], 1)
    sin_b = jnp.concatenate([zeros(B_NOPE), -s16, s16, zeros(B_SLOT - B_NOPE - B_ROPE)], 1)
    cos_c = jnp.tile(jnp.concatenate([c32, c32], 1), (1, C_HPG))
    sin_c = jnp.tile(jnp.concatenate([-s32, s32], 1), (1, C_HPG))
    return jnp.concatenate([cos_a, sin_a, cos_b, sin_b, cos_c, sin_c], 1)


def _layer_weights(l, w_in, diff_lambda, diff_subln, mla_q_norm, mla_w_uq, mla_kv_norm, mla_w_ukv, w_out,
                   ln1_g, ln1_b, moe_w_coarse, moe_w_fine, w1, w3, w2, ln2_g, ln2_b):
    wi = w_in[l]
    zc = lambda rows, w: jnp.zeros((rows, w), F32)
    b0 = COL_A
    w_b = jnp.concatenate([wi[:, b0:b0 + B_Q_RANK + B_KV_RANK], zc(D_MODEL, B_NOPE),
                           wi[:, b0 + B_Q_RANK + B_KV_RANK:b0 + COL_B], zc(D_MODEL, B_SLOT - B_NOPE - B_ROPE)], 1)
    c0 = COL_A + COL_B
    cw = C_HPG * HEAD_DIM * len(C_GROUPS)
    w_c = jnp.concatenate([wi[:, c0 + part * cw + g * C_GW:c0 + part * cw + (g + 1) * C_GW]
                           for g in range(len(C_GROUPS)) for part in range(3)], 1)
    qd = B_NOPE + B_ROPE
    w_uq = jnp.concatenate([jnp.concatenate([mla_w_uq[l][:, h * qd:(h + 1) * qd], zc(B_Q_RANK, B_SLOT - qd)], 1)
                            for h in range(B_HEADS)], 1)
    kvd = B_NOPE + B_V
    w_uk = jnp.concatenate([jnp.concatenate([mla_w_ukv[l][:, h * kvd:h * kvd + B_NOPE],
                                             zc(B_KV_RANK, B_SLOT - B_NOPE)], 1) for h in range(B_HEADS)], 1)
    w_uv = jnp.concatenate([mla_w_ukv[l][:, h * kvd + B_NOPE:(h + 1) * kvd] for h in range(B_HEADS)], 1)
    w_router = jnp.concatenate(
        [moe_w_coarse[l]] + [moe_w_fine[l][g] for g in range(N_GROUPS)]
        + [zc(D_MODEL, LANES - N_GROUPS - N_EXPERTS)], 1)
    lam_init = 0.8 - 0.6 * math.exp(-0.3 * l)
    lv = diff_lambda[l].astype(F32)
    lam = jnp.exp(jnp.sum(lv[0] * lv[1])) - jnp.exp(jnp.sum(lv[2] * lv[3])) + lam_init
    return dict(
        w_a=wi[:, 0:COL_A].astype(BF16), w_b=w_b.astype(BF16), w_c=w_c.astype(BF16),
        w_uq=w_uq.astype(BF16), w_uk=w_uk.astype(BF16), w_uv=w_uv.astype(BF16),
        q_norm=mla_q_norm[l].reshape(1, B_Q_RANK), kv_norm=mla_kv_norm[l].reshape(1, B_KV_RANK),
        diff_sc=jnp.stack([lam, jnp.asarray(1.0 - lam_init, F32)]).astype(F32),
        subln=jnp.tile(diff_subln[l], A_HEADS).reshape(1, A_W),
        w_out=w_out[l].astype(BF16), ln1_g=ln1_g[l].reshape(1, D_MODEL), ln1_b=ln1_b[l].reshape(1, D_MODEL),
        w_router=w_router, w1=w1[l], w3=w3[l], w2=w2[l],
        ln2_g=ln2_g[l].reshape(1, D_MODEL), ln2_b=ln2_b[l].reshape(1, D_MODEL))


def _encoder_layer(x, tab, lw, batch, seq):
    qa, ka, va, qb, kb, vb, c0, c1, c2 = _in_proj(x, tab, lw, seq)
    oa = _attn_a(lw['diff_sc'], qa, ka, va, lw['subln'], batch, seq)
    ob = _attn_b(qb, kb, vb, batch, seq)
    ocs, lses = zip(*[_attn_c(c, batch, seq, dil) for c, (_, dil) in zip((c0, c1, c2), C_GROUPS)])
    x1, ri, rw = _out_proj(x, oa, ob, ocs, lses, lw)
    blk_exp, slot_tok, pos = _dispatch_plan(ri[:, 0:TOP_K_INNER])
    ys = _experts(blk_exp, slot_tok, x1, lw)
    return _combine(pos, x1, rw, ys, lw)


def kernel(x_prompt, x_sample, w_in, diff_lambda, diff_subln, mla_q_norm, mla_w_uq, mla_kv_norm, mla_w_ukv,
           w_out, ln1_g, ln1_b, moe_w_coarse, moe_w_fine, moe_w1, moe_w3, moe_w2, ln2_g, ln2_b):
    w1, w3, w2 = moe_w1.astype(BF16), moe_w3.astype(BF16), moe_w2.astype(BF16)
    layers = [_layer_weights(l, w_in, diff_lambda, diff_subln, mla_q_norm, mla_w_uq, mla_kv_norm, mla_w_ukv,
                             w_out, ln1_g, ln1_b, moe_w_coarse, moe_w_fine, w1, w3, w2, ln2_g, ln2_b)
              for l in range(DEPTH)]

    def trunk(x):
        batch, seq, _ = x.shape
        tab = _rope_table_block(seq)
        h = x.reshape(batch * seq, D_MODEL)
        for lw in layers:
            h = _encoder_layer(h, tab, lw, batch, seq)
        return h.reshape(batch, seq, D_MODEL)

    return (trunk(x_prompt), trunk(x_sample))
```

```python
import functools
import math

import jax
import jax.numpy as jnp
from jax import lax
from jax.experimental import pallas as pl
from jax.experimental.pallas import tpu as pltpu

D_MODEL = 1024
DEPTH = 4
HEAD_DIM = 64
ROPE_THETA = 10000.0
LN_EPS = 1e-5
RMS_EPS = 1e-6
NEG_INF = -1e30

A_HEADS = 4
A_HALF = HEAD_DIM // 2
A_W = A_HEADS * HEAD_DIM

B_HEADS = 6
B_Q_RANK = 256
B_KV_RANK = 128
B_NOPE = 64
B_ROPE = 32
B_V = 64
B_W = B_HEADS * B_V
B_SLOT = 128
B_VT_ROWS = B_V + 16

C_GROUPS = ((128, 1), (512, 4), (2048, 16))
C_HPG = 2
C_GW = C_HPG * HEAD_DIM
C_SIDE = 64
C_BLOCK = 128
C_QUERY_ROWS = 512

COL_A = 3 * A_W
COL_B = B_Q_RANK + B_KV_RANK + B_ROPE

N_GROUPS = 4
EXPERTS_PER_GROUP = 8
N_EXPERTS = N_GROUPS * EXPERTS_PER_GROUP
TOP_K_INNER = 2
D_EXPERT = 512

DN_ALPHA = (2 * DEPTH) ** 0.25

LOG2E = 1.4426950408889634
SCALE_A = (A_HALF ** -0.5) * LOG2E
SCALE_B = ((B_NOPE + B_ROPE) ** -0.5) * LOG2E
SCALE_C = (HEAD_DIM ** -0.5) * LOG2E

LANES = 128
SUBLANES = 8
ROW_BLOCK = 256
Q_BLOCK = 256
K_CHUNK = 512
MOE_ROWS = 256
VMEM_LIMIT = 48 * 1024 * 1024
ATTN_VMEM_LIMIT = 56 * 1024 * 1024

BF16 = jnp.bfloat16
F32 = jnp.float32

_NT = (((1,), (1,)), ((), ()))


def _dot(a, b):
    return jnp.dot(a, b, preferred_element_type=F32)


def _dot_nt(a, b):
    return lax.dot_general(a, b, _NT, preferred_element_type=F32)


def _params(*sem, vmem=VMEM_LIMIT):
    return pltpu.CompilerParams(dimension_semantics=sem, vmem_limit_bytes=vmem)


def _rope(h, cos, sin_signed, half):
    width = h.shape[1]
    lane = lax.broadcasted_iota(jnp.int32, h.shape, 1)
    first = (lane % (2 * half)) < half
    partner = jnp.where(first, pltpu.roll(h, width - half, 1), pltpu.roll(h, half, 1))
    return h * cos + partner * sin_signed


def _rms(x, g):
    return x * lax.rsqrt(jnp.mean(x * x, axis=1, keepdims=True) + RMS_EPS) * g


def _in_proj_kernel(x_ref, tab_ref, wa_ref, wb_ref, wc_ref, wuq_ref, wuk_ref, wuv_ref, qn_ref, kvn_ref,
                    qa_ref, ka_ref, va_ref, qb_ref, kb_ref, vb_ref, c0_ref, c1_ref, c2_ref, cs_scr):
    xb = x_ref[...].astype(BF16)
    cos_a, sin_a = tab_ref[:, 0:256], tab_ref[:, 256:512]
    cos_b, sin_b = tab_ref[:, 512:640], tab_ref[:, 640:768]
    cos_c, sin_c = tab_ref[:, 768:896], tab_ref[:, 896:1024]

    ha = _dot(xb, wa_ref[...])
    qa_ref[...] = (_rope(ha[:, 0:A_W], cos_a, sin_a, A_HALF // 2) * SCALE_A).astype(BF16)
    ka_ref[...] = _rope(ha[:, A_W:2 * A_W], cos_a, sin_a, A_HALF // 2).astype(BF16)
    va_ref[...] = ha[:, 2 * A_W:3 * A_W].astype(BF16)

    hb = _dot(xb, wb_ref[...])
    cq = _rms(hb[:, 0:B_Q_RANK], qn_ref[...]).astype(BF16)
    ckv = _rms(hb[:, B_Q_RANK:B_Q_RANK + B_KV_RANK], kvn_ref[...]).astype(BF16)
    qb = _dot(cq, wuq_ref[...])
    kb = _dot(ckv, wuk_ref[...])
    vb_ref[...] = _dot(ckv, wuv_ref[...]).astype(BF16)
    k_rope = _rope(hb[:, 384:512], cos_b, sin_b, B_ROPE // 2)
    for h in range(B_HEADS):
        sl = slice(B_SLOT * h, B_SLOT * (h + 1))
        qb_ref[:, sl] = (_rope(qb[:, sl], cos_b, sin_b, B_ROPE // 2) * SCALE_B).astype(BF16)
        kb_ref[:, sl] = (kb[:, sl] + k_rope).astype(BF16)

    hc = _dot(xb, wc_ref[...])
    tm = hc.shape[0]
    for g, c_ref in enumerate((c0_ref, c1_ref, c2_ref)):
        base = 3 * C_GW * g
        dil = C_GROUPS[g][1]
        qkv = [_rope(hc[:, base:base + C_GW], cos_c, sin_c, HEAD_DIM // 2) * SCALE_C,
               _rope(hc[:, base + C_GW:base + 2 * C_GW], cos_c, sin_c, HEAD_DIM // 2),
               hc[:, base + 2 * C_GW:base + 3 * C_GW]]
        for part in range(3):
            cols = slice(C_GW * part, C_GW * (part + 1))
            if dil == 1:
                c_ref[0, :, cols] = qkv[part].astype(BF16)
            else:
                cs_scr[part] = qkv[part]
                for j in range(dil):
                    c_ref[j, :, cols] = cs_scr[part, pl.ds(j, tm // dil, stride=dil), :].astype(BF16)


def _dilated_spec(width, dil, nrep):
    return pl.BlockSpec((None, dil, ROW_BLOCK // dil, width), lambda i: (i // nrep, 0, i % nrep, 0))


def _in_proj(x, tab, lw, batch, seq):
    n = x.shape[0]
    tm = ROW_BLOCK
    nrep = seq // tm
    row = lambda w: pl.BlockSpec((tm, w), lambda i: (i, 0))
    full = lambda a: pl.BlockSpec(a.shape, lambda i: (0,) * a.ndim)
    weights = (lw['w_a'], lw['w_b'], lw['w_c'], lw['w_uq'], lw['w_uk'], lw['w_uv'], lw['q_norm'], lw['kv_norm'])
    out_w = (A_W, A_W, A_W, B_HEADS * B_SLOT, B_HEADS * B_SLOT, B_W)
    return pl.pallas_call(
        _in_proj_kernel,
        grid=(n // tm,),
        in_specs=[row(D_MODEL), pl.BlockSpec((tm, 1024), lambda i: (i % nrep, 0))] + [full(w) for w in weights],
        out_specs=[row(w) for w in out_w] + [_dilated_spec(3 * C_GW, dil, nrep) for _, dil in C_GROUPS],
        out_shape=[jax.ShapeDtypeStruct((n, w), BF16) for w in out_w]
        + [jax.ShapeDtypeStruct((batch, dil, seq // dil, 3 * C_GW), BF16) for _, dil in C_GROUPS],
        scratch_shapes=[pltpu.VMEM((3, tm, C_GW), F32)],
        compiler_params=_params("parallel"),
        name="in_proj",
    )(x, tab, *weights)


def _fold_rows(x, op):
    r, c = x.shape
    return op(x.reshape(r // SUBLANES, SUBLANES, c), axis=0)


def _chunk_start(kc):
    return pl.multiple_of(kc * K_CHUNK, K_CHUNK)


def _stage_v_transposed(v_ref, vt_scr):
    @pl.when(pl.program_id(1) == 0)
    def _():
        def body(kc, carry):
            vt_scr[kc] = v_ref[pl.ds(_chunk_start(kc), K_CHUNK), :].astype(F32).T.astype(BF16)
            return carry
        lax.fori_loop(0, vt_scr.shape[0], body, 0)


def _attn_a_kernel(sc_ref, q_ref, k_ref, v_ref, g_ref, o_ref, vt_scr, s_scr):
    lam = sc_ref[0]
    post = sc_ref[1]
    tq = q_ref.shape[0]
    nkc = k_ref.shape[0] // K_CHUNK
    _stage_v_transposed(v_ref, vt_scr)
    qt = q_ref[...].astype(F32).T
    row = lax.broadcasted_iota(jnp.int32, (A_W, tq), 0)
    neg = jnp.full((SUBLANES, tq), -jnp.inf, F32)
    zero = jnp.zeros((SUBLANES, tq), F32)
    maxes, gammas, inv_l1, heads = {}, {}, {}, []
    for st in range(A_HEADS + 2):
        hs, hp, hw = st, st - 1, st - 2
        do_s, do_p, do_w = hs < A_HEADS, 0 <= hp < A_HEADS, 0 <= hw < A_HEADS
        qts = []
        if do_s:
            for c in range(2):
                lo = A_HALF * (2 * hs + c)
                qts.append(jnp.where((row >= lo) & (row < lo + A_HALF), qt, 0.0).astype(BF16))

        def body(kc, carry, qts=qts, hs=hs, hp=hp, hw=hw, do_s=do_s, do_p=do_p, do_w=do_w):
            m1, m2, l1, l2, acc = carry
            if do_s:
                kr = k_ref[pl.ds(_chunk_start(kc), K_CHUNK), :]
                s1 = _dot(kr, qts[0])
                s2 = _dot(kr, qts[1])
                s_scr[hs % 3, 0, kc] = s1
                s_scr[hs % 3, 1, kc] = s2
                m1 = jnp.maximum(m1, _fold_rows(s1, jnp.max))
                m2 = jnp.maximum(m2, _fold_rows(s2, jnp.max))
            if do_p:
                p1 = jnp.exp2(s_scr[hp % 3, 0, kc] - maxes[hp][0])
                p2 = jnp.exp2(s_scr[hp % 3, 1, kc] - maxes[hp][1])
                s_scr[hp % 3, 0, kc] = p1
                s_scr[hp % 3, 1, kc] = p2
                l1 = l1 + _fold_rows(p1, jnp.sum)
                l2 = l2 + _fold_rows(p2, jnp.sum)
            if do_w:
                w = (s_scr[hw % 3, 0, kc] - gammas[hw] * s_scr[hw % 3, 1, kc]).astype(BF16)
                acc = acc + _dot(vt_scr[kc, HEAD_DIM * hw:HEAD_DIM * (hw + 1), :], w)
            return m1, m2, l1, l2, acc

        m1, m2, l1, l2, acc = lax.fori_loop(
            0, nkc, body, (neg, neg, zero, zero, jnp.zeros((HEAD_DIM, tq), F32)), unroll=2)
        if do_s:
            maxes[hs] = (jnp.max(m1, axis=0, keepdims=True), jnp.max(m2, axis=0, keepdims=True))
        if do_p:
            l1 = jnp.sum(l1, axis=0, keepdims=True)
            l2 = jnp.sum(l2, axis=0, keepdims=True)
            gammas[hp] = lam * l1 / l2
            inv_l1[hp] = 1.0 / l1
        if do_w:
            oh = acc * inv_l1[hw]
            ms = jnp.mean(oh * oh, axis=0, keepdims=True)
            heads.append(oh * lax.rsqrt(ms + RMS_EPS))
    out_t = jnp.concatenate(heads, axis=0)
    o_ref[...] = (out_t.T * g_ref[...] * post).astype(BF16)


def _attn_a(sc, q, k, v, g, batch, seq):
    n = q.shape[0]
    tq = Q_BLOCK
    nq = seq // tq
    nkc = seq // K_CHUNK
    return pl.pallas_call(
        _attn_a_kernel,
        grid=(batch, nq),
        in_specs=[pl.BlockSpec(memory_space=pltpu.SMEM),
                  pl.BlockSpec((tq, A_W), lambda b, i: (b * nq + i, 0)),
                  pl.BlockSpec((seq, A_W), lambda b, i: (b, 0)),
                  pl.BlockSpec((seq, A_W), lambda b, i: (b, 0)),
                  pl.BlockSpec((1, A_W), lambda b, i: (0, 0))],
        out_specs=pl.BlockSpec((tq, A_W), lambda b, i: (b * nq + i, 0)),
        out_shape=jax.ShapeDtypeStruct((n, A_W), BF16),
        scratch_shapes=[pltpu.VMEM((nkc, A_W, K_CHUNK), BF16),
                        pltpu.VMEM((3, 2, nkc, K_CHUNK, tq), F32)],
        compiler_params=_params("arbitrary", "arbitrary", vmem=ATTN_VMEM_LIMIT),
        name="attn_a",
    )(sc, q, k, v, g)


def _attn_b_kernel(q_ref, k_ref, v_ref, o_ref, vt_scr, s_scr):
    tq = q_ref.shape[0]
    nkc = k_ref.shape[0] // K_CHUNK

    @pl.when(pl.program_id(1) == 0)
    def _():
        ones = jnp.ones((B_VT_ROWS - B_V, K_CHUNK), BF16)

        def stage(kc, carry):
            vt = v_ref[pl.ds(_chunk_start(kc), K_CHUNK), :].astype(F32).T.astype(BF16)
            for h in range(B_HEADS):
                vt_scr[kc, B_VT_ROWS * h:B_VT_ROWS * h + B_V, :] = vt[B_V * h:B_V * (h + 1), :]
                vt_scr[kc, B_VT_ROWS * h + B_V:B_VT_ROWS * (h + 1), :] = ones
            return carry
        lax.fori_loop(0, nkc, stage, 0)

    neg = jnp.full((SUBLANES, tq), -jnp.inf, F32)
    acc0 = jnp.zeros((B_VT_ROWS, tq), F32)
    n_pairs = B_HEADS // 2
    maxes, heads = {}, []
    for st in range(n_pairs + 1):
        ps, pw = st, st - 1
        do_s, do_w = ps < n_pairs, pw >= 0
        qts = []
        if do_s:
            for j in range(2):
                sl = slice(B_SLOT * (2 * ps + j), B_SLOT * (2 * ps + j + 1))
                qts.append(q_ref[:, sl].astype(F32).T.astype(BF16))

        def body(kc, carry, qts=qts, ps=ps, pw=pw, do_s=do_s, do_w=do_w):
            ms, accs = list(carry[0]), list(carry[1])
            for j in range(2):
                if do_s:
                    h = 2 * ps + j
                    s = _dot(k_ref[pl.ds(_chunk_start(kc), K_CHUNK), B_SLOT * h:B_SLOT * (h + 1)], qts[j])
                    s_scr[ps % 2, j, kc] = s
                    ms[j] = jnp.maximum(ms[j], _fold_rows(s, jnp.max))
                if do_w:
                    h = 2 * pw + j
                    p = jnp.exp2(s_scr[pw % 2, j, kc] - maxes[h]).astype(BF16)
                    accs[j] = accs[j] + _dot(vt_scr[kc, B_VT_ROWS * h:B_VT_ROWS * (h + 1), :], p)
            return tuple(ms), tuple(accs)

        ms, accs = lax.fori_loop(0, nkc, body, ((neg, neg), (acc0, acc0)), unroll=2)
        for j in range(2):
            if do_s:
                maxes[2 * ps + j] = jnp.max(ms[j], axis=0, keepdims=True)
            if do_w:
                heads.append(accs[j][0:B_V] * (1.0 / accs[j][B_V:B_V + 1]))
    o_ref[...] = jnp.concatenate(heads, axis=0).T.astype(BF16)


def _attn_b(q, k, v, batch, seq):
    n = q.shape[0]
    tq = Q_BLOCK
    nq = seq // tq
    nkc = seq // K_CHUNK
    wq = B_HEADS * B_SLOT
    return pl.pallas_call(
        _attn_b_kernel,
        grid=(batch, nq),
        in_specs=[pl.BlockSpec((tq, wq), lambda b, i: (b * nq + i, 0)),
                  pl.BlockSpec((seq, wq), lambda b, i: (b, 0)),
                  pl.BlockSpec((seq, B_W), lambda b, i: (b, 0))],
        out_specs=pl.BlockSpec((tq, B_W), lambda b, i: (b * nq + i, 0)),
        out_shape=jax.ShapeDtypeStruct((n, B_W), BF16),
        scratch_shapes=[pltpu.VMEM((nkc, B_HEADS * B_VT_ROWS, K_CHUNK), BF16),
                        pltpu.VMEM((2, 2, nkc, K_CHUNK, tq), F32)],
        compiler_params=_params("arbitrary", "arbitrary", vmem=ATTN_VMEM_LIMIT),
        name="attn_b",
    )(q, k, v)


def _attn_c_kernel(own_ref, prev_ref, next_ref, o_ref, lse_ref, *, sub_len):
    i = pl.program_id(2)
    tq = own_ref.shape[0]
    tb = C_BLOCK
    win = 2 * tb
    kv = jnp.concatenate([prev_ref[:, C_GW:], own_ref[:, C_GW:], next_ref[:, C_GW:]], axis=0)
    lane = lax.broadcasted_iota(jnp.int32, (tb, C_GW), 1)
    key = lax.broadcasted_iota(jnp.int32, (tb, win), 1)
    band = jnp.abs(key - C_SIDE - lax.broadcasted_iota(jnp.int32, (tb, win), 0)) <= C_SIDE
    for u in range(tq // tb):
        q = own_ref[tb * u:tb * (u + 1), 0:C_GW]
        kvw = kv[tb * u + C_SIDE:tb * u + C_SIDE + win]
        kpos = i * tq + tb * u - C_SIDE + key
        valid = band & (kpos >= 0) & (kpos < sub_len)
        o = jnp.zeros((tb, C_GW), F32)
        lse = jnp.zeros((tb, C_GW), F32)
        for hh in range(C_HPG):
            head = (lane >= HEAD_DIM * hh) & (lane < HEAD_DIM * (hh + 1))
            qm = jnp.where(head, q, jnp.zeros_like(q))
            s = jnp.where(valid, _dot_nt(qm, kvw[:, 0:C_GW]), NEG_INF)
            m = jnp.max(s, axis=1, keepdims=True)
            p = jnp.exp2(s - m)
            l = jnp.sum(p, axis=1, keepdims=True)
            oh = _dot(p.astype(BF16), kvw[:, C_GW:2 * C_GW]) * (1.0 / l)
            o = jnp.where(head, oh, o)
            lse = jnp.where(head, m + jnp.log(l) * LOG2E, lse)
        o_ref[tb * u:tb * (u + 1), :] = o
        lse_ref[tb * u:tb * (u + 1), :] = lse


def _attn_c(c, batch, seq, dil):
    sub_len = seq // dil
    tq = min(C_QUERY_ROWS, sub_len)
    nsub = tq // C_BLOCK
    last = sub_len // C_BLOCK - 1
    o, lse = pl.pallas_call(
        functools.partial(_attn_c_kernel, sub_len=sub_len),
        grid=(batch, dil, sub_len // tq),
        in_specs=[pl.BlockSpec((None, None, tq, 3 * C_GW), lambda b, j, i: (b, j, i, 0)),
                  pl.BlockSpec((None, None, C_BLOCK, 3 * C_GW),
                               lambda b, j, i: (b, j, jnp.maximum(i * nsub - 1, 0), 0)),
                  pl.BlockSpec((None, None, C_BLOCK, 3 * C_GW),
                               lambda b, j, i: (b, j, jnp.minimum((i + 1) * nsub, last), 0))],
        out_specs=[pl.BlockSpec((None, None, tq, C_GW), lambda b, j, i: (b, j, i, 0))] * 2,
        out_shape=[jax.ShapeDtypeStruct((batch, dil, sub_len, C_GW), F32)] * 2,
        compiler_params=_params("parallel", "parallel", "parallel"),
        name=f"attn_c_d{dil}",
    )(c, c, c)
    return o, lse


def _layer_norm(z, g, b):
    mu = jnp.mean(z, axis=1, keepdims=True)
    zc = z - mu
    var = jnp.mean(zc * zc, axis=1, keepdims=True)
    return zc * lax.rsqrt(var + LN_EPS) * g + b


def _first_index(hit, lane):
    return jnp.min(jnp.where(hit, lane, LANES), axis=1, keepdims=True)


def _token_order(ref, scr):
    dil, sub, _ = ref.shape
    if dil == 1:
        return ref[0]
    for j in range(dil):
        scr[pl.ds(j, sub, stride=dil), :] = ref[j]
    return scr[...]


def _out_proj_kernel(x_ref, oa_ref, ob_ref, oc0_ref, oc1_ref, oc2_ref, l0_ref, l1_ref, l2_ref,
                     wout_ref, g_ref, b_ref, wrh_ref, wrl_ref, x1_ref, ri_ref, rw_ref, ord_scr):
    oc = [_token_order(r, ord_scr.at[k]) for k, r in enumerate((oc0_ref, oc1_ref, oc2_ref))]
    la, lb, lc = [_token_order(r, ord_scr.at[3 + k]) for k, r in enumerate((l0_ref, l1_ref, l2_ref))]
    mx = jnp.maximum(jnp.maximum(la, lb), lc)
    ea, eb, ec = jnp.exp2(la - mx), jnp.exp2(lb - mx), jnp.exp2(lc - mx)
    inv = 1.0 / (ea + eb + ec)
    mix = jnp.concatenate(
        [oa_ref[...], ob_ref[...],
         (oc[0] * (ea * inv)).astype(BF16),
         (oc[1] * (eb * inv)).astype(BF16),
         (oc[2] * (ec * inv)).astype(BF16)], axis=1)
    x1 = _layer_norm(DN_ALPHA * x_ref[...] + _dot(mix, wout_ref[...]), g_ref[...], b_ref[...])
    x1_ref[...] = x1

    x_hi = x1.astype(BF16)
    x_lo = (x1 - x_hi.astype(F32)).astype(BF16)
    logits = _dot(x_hi, wrh_ref[...]) + (_dot(x_lo, wrh_ref[...]) + _dot(x_hi, wrl_ref[...]))
    lane = lax.broadcasted_iota(jnp.int32, logits.shape, 1)
    ninf = -jnp.inf
    cl = jnp.where(lane < N_GROUPS, logits, ninf)
    cmax = jnp.max(cl, axis=1, keepdims=True)
    grp = _first_index(cl == cmax, lane)
    pg = 1.0 / jnp.sum(jnp.exp(cl - cmax), axis=1, keepdims=True)
    lo = N_GROUPS + EXPERTS_PER_GROUP * grp
    fl = jnp.where((lane >= lo) & (lane < lo + EXPERTS_PER_GROUP), logits, ninf)
    v1 = jnp.max(fl, axis=1, keepdims=True)
    i1 = _first_index(fl == v1, lane)
    fl2 = jnp.where(lane == i1, ninf, fl)
    v2 = jnp.max(fl2, axis=1, keepdims=True)
    i2 = _first_index(fl2 == v2, lane)
    e21 = jnp.exp(v2 - v1)
    t1 = pg / (1.0 + e21)
    t2 = t1 * e21
    ri_ref[...] = jnp.where(lane == 0, i1 - N_GROUPS, jnp.where(lane == 1, i2 - N_GROUPS, 0))
    rw_ref[...] = jnp.where(lane == 0, t1, jnp.where(lane == 1, t2, 0.0))


def _out_proj(x, oa, ob, ocs, lses, lw, seq):
    n = x.shape[0]
    tm = ROW_BLOCK
    nrep = seq // tm
    row = lambda w: pl.BlockSpec((tm, w), lambda i: (i, 0))
    full = lambda a: pl.BlockSpec(a.shape, lambda i: (0,) * a.ndim)
    dilated = [_dilated_spec(C_GW, dil, nrep) for _, dil in C_GROUPS]
    weights = (lw['w_out'], lw['ln1_g'], lw['ln1_b'], lw['w_router_hi'], lw['w_router_lo'])
    return pl.pallas_call(
        _out_proj_kernel,
        grid=(n // tm,),
        in_specs=[row(D_MODEL), row(A_W), row(B_W)] + dilated + dilated + [full(w) for w in weights],
        out_specs=[row(D_MODEL), row(LANES), row(LANES)],
        out_shape=[jax.ShapeDtypeStruct((n, D_MODEL), F32),
                   jax.ShapeDtypeStruct((n, LANES), jnp.int32),
                   jax.ShapeDtypeStruct((n, LANES), F32)],
        scratch_shapes=[pltpu.VMEM((2 * len(C_GROUPS), tm, C_GW), F32)],
        compiler_params=_params("parallel"),
        name="out_proj",
    )(x, oa, ob, *ocs, *lses, *weights)


def _start_row_gathers(idx_ref, idx_smem, src_hbm, dst, sem_i, sem):
    cp = pltpu.make_async_copy(idx_ref.at[0, 0], idx_smem, sem_i)
    cp.start()
    cp.wait()
    for r in range(dst.shape[0]):
        pltpu.make_async_copy(src_hbm.at[pl.ds(idx_smem[r], 1)], dst.at[pl.ds(r, 1)], sem).start()


def _prefetched_rows(idx_cur_ref, idx_next_ref, src_hbm, buf, idx_smem, sem_i, sem_g):
    i = pl.program_id(0)
    slot = lax.rem(i, 2)
    rows = buf.shape[1]

    @pl.when(i == 0)
    def _():
        _start_row_gathers(idx_cur_ref, idx_smem, src_hbm, buf.at[0], sem_i, sem_g.at[0])

    @pl.when(i + 1 < pl.num_programs(0))
    def _():
        _start_row_gathers(idx_next_ref, idx_smem, src_hbm, buf.at[1 - slot], sem_i, sem_g.at[1 - slot])

    def drain(r, carry):
        pltpu.make_async_copy(src_hbm.at[pl.ds(0, 1)], buf.at[slot, pl.ds(0, 1)], sem_g.at[slot]).wait()
        return carry

    lax.fori_loop(0, rows, drain, 0, unroll=8)
    return slot


def _index_specs(rows, steps, prefetch_args=0):
    if prefetch_args:
        return [pl.BlockSpec((1, 1, rows), lambda i, be: (i, 0, 0)),
                pl.BlockSpec((1, 1, rows), lambda i, be: (jnp.minimum(i + 1, steps - 1), 0, 0))]
    return [pl.BlockSpec((1, 1, rows), lambda i: (i, 0, 0)),
            pl.BlockSpec((1, 1, rows), lambda i: (jnp.minimum(i + 1, steps - 1), 0, 0))]


def _gather_scratch(rows):
    return [pltpu.SMEM((rows,), jnp.int32), pltpu.VMEM((2, rows, D_MODEL), F32),
            pltpu.SemaphoreType.DMA, pltpu.SemaphoreType.DMA((2,))]


def _expert_kernel(be_ref, tok_ref, tok_next_ref, x_hbm, w1_ref, w3_ref, w2_ref, y_ref,
                   idx_smem, xg, sem_i, sem_g):
    del be_ref
    slot = _prefetched_rows(tok_ref, tok_next_ref, x_hbm, xg, idx_smem, sem_i, sem_g)
    xb = xg[slot].astype(BF16)
    h1 = _dot(xb, w1_ref[...])
    h3 = _dot(xb, w3_ref[...])
    hid = (h1 * (1.0 / (1.0 + jnp.exp(-h1))) * h3).astype(BF16)
    y_ref[...] = _dot(hid, w2_ref[...])


def _experts(blk_exp, slot_tok, x1, lw):
    n_blocks = blk_exp.shape[0]
    tb = MOE_ROWS
    tok = slot_tok.reshape(n_blocks, 1, tb)
    grid_spec = pltpu.PrefetchScalarGridSpec(
        num_scalar_prefetch=1,
        grid=(n_blocks,),
        in_specs=_index_specs(tb, n_blocks, prefetch_args=1) + [
            pl.BlockSpec(memory_space=pl.ANY),
            pl.BlockSpec((None, D_MODEL, D_EXPERT), lambda i, be: (be[i], 0, 0)),
            pl.BlockSpec((None, D_MODEL, D_EXPERT), lambda i, be: (be[i], 0, 0)),
            pl.BlockSpec((None, D_EXPERT, D_MODEL), lambda i, be: (be[i], 0, 0))],
        out_specs=pl.BlockSpec((tb, D_MODEL), lambda i, be: (i, 0)),
        scratch_shapes=_gather_scratch(tb))
    return pl.pallas_call(
        _expert_kernel,
        grid_spec=grid_spec,
        out_shape=jax.ShapeDtypeStruct((n_blocks * tb, D_MODEL), F32),
        compiler_params=_params("arbitrary"),
        name="experts",
    )(blk_exp, tok, tok, x1, lw['w1'], lw['w3'], lw['w2'])


def _combine_kernel(pos_ref, pos_next_ref, x1_ref, rw_ref, ys_hbm, g_ref, b_ref, out_ref,
                    idx_smem, yg, sem_i, sem_g):
    tm = x1_ref.shape[0]
    slot = _prefetched_rows(pos_ref, pos_next_ref, ys_hbm, yg, idx_smem, sem_i, sem_g)
    rw = rw_ref[...]
    y = rw[:, 0:1] * yg[slot, 0:tm, :] + rw[:, 1:2] * yg[slot, tm:2 * tm, :]
    out_ref[...] = _layer_norm(DN_ALPHA * x1_ref[...] + y, g_ref[...], b_ref[...])


def _combine(pos, x1, rw, ys, lw):
    n = x1.shape[0]
    tm = ROW_BLOCK
    nb = n // tm
    rows = TOP_K_INNER * tm
    pos_blk = pos.reshape(nb, tm, TOP_K_INNER).transpose(0, 2, 1).reshape(nb, 1, rows)
    full = lambda a: pl.BlockSpec(a.shape, lambda i: (0,) * a.ndim)
    return pl.pallas_call(
        _combine_kernel,
        grid=(nb,),
        in_specs=_index_specs(rows, nb) + [
            pl.BlockSpec((tm, D_MODEL), lambda i: (i, 0)),
            pl.BlockSpec((tm, LANES), lambda i: (i, 0)),
            pl.BlockSpec(memory_space=pl.ANY),
            full(lw['ln2_g']), full(lw['ln2_b'])],
        out_specs=pl.BlockSpec((tm, D_MODEL), lambda i: (i, 0)),
        out_shape=jax.ShapeDtypeStruct((n, D_MODEL), F32),
        scratch_shapes=_gather_scratch(rows),
        compiler_params=_params("arbitrary"),
        name="combine",
    )(pos_blk, pos_blk, x1, rw, ys, lw['ln2_g'], lw['ln2_b'])


def _dispatch_plan(eid):
    n = eid.shape[0]
    a = n * TOP_K_INNER
    tb = MOE_ROWS
    e_flat = eid.reshape(a)
    order = jnp.argsort(e_flat).astype(jnp.int32)
    experts = jnp.arange(N_EXPERTS, dtype=jnp.int32)
    counts = jnp.sum((e_flat[:, None] == experts[None, :]).astype(jnp.int32), axis=0)
    padded = (counts + tb - 1) // tb * tb
    pad_end = jnp.cumsum(padded)
    pad_start = pad_end - padded
    start = jnp.cumsum(counts) - counts
    shift = pad_start - start
    n_blocks = -(-(a + N_EXPERTS * (tb - 1)) // tb)
    blk_start = jnp.arange(n_blocks, dtype=jnp.int32) * tb
    blk_exp = jnp.minimum(jnp.sum((blk_start[:, None] >= pad_end[None, :]).astype(jnp.int32), axis=1),
                          N_EXPERTS - 1)
    e_slot = jnp.repeat(blk_exp, tb)
    sorted_pos = jnp.arange(n_blocks * tb, dtype=jnp.int32) - shift[e_slot]
    valid = sorted_pos < (start + counts)[e_slot]
    slot_tok = jnp.where(valid, order[jnp.clip(sorted_pos, 0, a - 1)] // TOP_K_INNER, 0)
    dest = shift[e_flat[order]] + jnp.arange(a, dtype=jnp.int32)
    pos = dest[jnp.argsort(order)].reshape(n, TOP_K_INNER)
    return blk_exp, slot_tok, pos


def _rope_tables(seq, dim):
    inv_freq = 1.0 / (ROPE_THETA ** (jnp.arange(0, dim, 2, dtype=F32) / dim))
    ang = jnp.arange(seq, dtype=F32)[:, None] * inv_freq[None, :]
    return jnp.cos(ang), jnp.sin(ang)


def _rope_table_block(seq):
    c16, s16 = _rope_tables(seq, A_HALF)
    c32, s32 = _rope_tables(seq, HEAD_DIM)
    ones = lambda w: jnp.ones((seq, w), F32)
    zeros = lambda w: jnp.zeros((seq, w), F32)
    cos_a = jnp.tile(jnp.concatenate([c16, c16], 1), (1, 2 * A_HEADS))
    sin_a = jnp.tile(jnp.concatenate([-s16, s16], 1), (1, 2 * A_HEADS))
    cos_b = jnp.concatenate([ones(B_NOPE), c16, c16, ones(B_SLOT - B_NOPE - B_ROPE)], 1)
    sin_b = jnp.concatenate([zeros(B_NOPE), -s16, s16, zeros(B_SLOT - B_NOPE - B_ROPE)], 1)
    cos_c = jnp.tile(jnp.concatenate([c32, c32], 1), (1, C_HPG))
    sin_c = jnp.tile(jnp.concatenate([-s32, s32], 1), (1, C_HPG))
    return jnp.concatenate([cos_a, sin_a, cos_b, sin_b, cos_c, sin_c], 1)


def _layer_weights(l, w_in, diff_lambda, diff_subln, mla_q_norm, mla_w_uq, mla_kv_norm, mla_w_ukv, w_out,
                   ln1_g, ln1_b, moe_w_coarse, moe_w_fine, w1, w3, w2, ln2_g, ln2_b):
    wi = w_in[l]
    zc = lambda rows, w: jnp.zeros((rows, w), F32)
    b0 = COL_A
    w_b = jnp.concatenate([wi[:, b0:b0 + B_Q_RANK + B_KV_RANK], zc(D_MODEL, B_NOPE),
                           wi[:, b0 + B_Q_RANK + B_KV_RANK:b0 + COL_B], zc(D_MODEL, B_SLOT - B_NOPE - B_ROPE)], 1)
    c0 = COL_A + COL_B
    cw = C_HPG * HEAD_DIM * len(C_GROUPS)
    w_c = jnp.concatenate([wi[:, c0 + part * cw + g * C_GW:c0 + part * cw + (g + 1) * C_GW]
                           for g in range(len(C_GROUPS)) for part in range(3)], 1)
    qd = B_NOPE + B_ROPE
    w_uq = jnp.concatenate([jnp.concatenate([mla_w_uq[l][:, h * qd:(h + 1) * qd], zc(B_Q_RANK, B_SLOT - qd)], 1)
                            for h in range(B_HEADS)], 1)
    kvd = B_NOPE + B_V
    w_uk = jnp.concatenate([jnp.concatenate([mla_w_ukv[l][:, h * kvd:h * kvd + B_NOPE],
                                             zc(B_KV_RANK, B_SLOT - B_NOPE)], 1) for h in range(B_HEADS)], 1)
    w_uv = jnp.concatenate([mla_w_ukv[l][:, h * kvd + B_NOPE:(h + 1) * kvd] for h in range(B_HEADS)], 1)
    w_router = jnp.concatenate(
        [moe_w_coarse[l]] + [moe_w_fine[l][g] for g in range(N_GROUPS)]
        + [zc(D_MODEL, LANES - N_GROUPS - N_EXPERTS)], 1)
    lam_init = 0.8 - 0.6 * math.exp(-0.3 * l)
    lv = diff_lambda[l].astype(F32)
    lam = jnp.exp(jnp.sum(lv[0] * lv[1])) - jnp.exp(jnp.sum(lv[2] * lv[3])) + lam_init
    return dict(
        w_a=wi[:, 0:COL_A].astype(BF16), w_b=w_b.astype(BF16), w_c=w_c.astype(BF16),
        w_uq=w_uq.astype(BF16), w_uk=w_uk.astype(BF16), w_uv=w_uv.astype(BF16),
        q_norm=mla_q_norm[l].reshape(1, B_Q_RANK), kv_norm=mla_kv_norm[l].reshape(1, B_KV_RANK),
        diff_sc=jnp.stack([lam, jnp.asarray(1.0 - lam_init, F32)]).astype(F32),
        subln=jnp.tile(diff_subln[l], A_HEADS).reshape(1, A_W),
        w_out=w_out[l].astype(BF16), ln1_g=ln1_g[l].reshape(1, D_MODEL), ln1_b=ln1_b[l].reshape(1, D_MODEL),
        w_router_hi=w_router.astype(BF16),
        w_router_lo=(w_router - w_router.astype(BF16).astype(F32)).astype(BF16),
        w1=w1[l], w3=w3[l], w2=w2[l],
        ln2_g=ln2_g[l].reshape(1, D_MODEL), ln2_b=ln2_b[l].reshape(1, D_MODEL))


def _encoder_layer(x, tab, lw, batch, seq):
    qa, ka, va, qb, kb, vb, c0, c1, c2 = _in_proj(x, tab, lw, batch, seq)
    oa = _attn_a(lw['diff_sc'], qa, ka, va, lw['subln'], batch, seq)
    ob = _attn_b(qb, kb, vb, batch, seq)
    ocs, lses = zip(*[_attn_c(c, batch, seq, dil) for c, (_, dil) in zip((c0, c1, c2), C_GROUPS)])
    x1, ri, rw = _out_proj(x, oa, ob, ocs, lses, lw, seq)
    blk_exp, slot_tok, pos = _dispatch_plan(ri[:, 0:TOP_K_INNER])
    ys = _experts(blk_exp, slot_tok, x1, lw)
    return _combine(pos, x1, rw, ys, lw)


def kernel(x_prompt, x_sample, w_in, diff_lambda, diff_subln, mla_q_norm, mla_w_uq, mla_kv_norm, mla_w_ukv,
           w_out, ln1_g, ln1_b, moe_w_coarse, moe_w_fine, moe_w1, moe_w3, moe_w2, ln2_g, ln2_b):
    w1, w3, w2 = moe_w1.astype(BF16), moe_w3.astype(BF16), moe_w2.astype(BF16)
    layers = [_layer_weights(l, w_in, diff_lambda, diff_subln, mla_q_norm, mla_w_uq, mla_kv_norm, mla_w_ukv,
                             w_out, ln1_g, ln1_b, moe_w_coarse, moe_w_fine, w1, w3, w2, ln2_g, ln2_b)
              for l in range(DEPTH)]

    def trunk(x):
        batch, seq, _ = x.shape
        tab = _rope_table_block(seq)
        h = x.reshape(batch * seq, D_MODEL)
        for lw in layers:
            h = _encoder_layer(h, tab, lw, batch, seq)
        return h.reshape(batch, seq, D_MODEL)

    return (trunk(x_prompt), trunk(x_sample))
```

```python
import functools
import math

import jax
import jax.numpy as jnp
from jax import lax
from jax.experimental import pallas as pl
from jax.experimental.pallas import tpu as pltpu

D_MODEL = 1024
DEPTH = 4
HEAD_DIM = 64
ROPE_THETA = 10000.0
LN_EPS = 1e-5
RMS_EPS = 1e-6
NEG_INF = -1e30

A_HEADS = 4
A_HALF = HEAD_DIM // 2
A_W = A_HEADS * HEAD_DIM

B_HEADS = 6
B_Q_RANK = 256
B_KV_RANK = 128
B_NOPE = 64
B_ROPE = 32
B_V = 64
B_W = B_HEADS * B_V
B_SLOT = 128
B_VT_ROWS = B_V + 16

C_GROUPS = ((128, 1), (512, 4), (2048, 16))
C_HPG = 2
C_GW = C_HPG * HEAD_DIM
C_SIDE = 64
C_BLOCK = 128
C_QUERY_ROWS = 512

COL_A = 3 * A_W
COL_B = B_Q_RANK + B_KV_RANK + B_ROPE

N_GROUPS = 4
EXPERTS_PER_GROUP = 8
N_EXPERTS = N_GROUPS * EXPERTS_PER_GROUP
TOP_K_INNER = 2
D_EXPERT = 512

DN_ALPHA = (2 * DEPTH) ** 0.25

LOG2E = 1.4426950408889634
SCALE_A = (A_HALF ** -0.5) * LOG2E
SCALE_B = ((B_NOPE + B_ROPE) ** -0.5) * LOG2E
SCALE_C = (HEAD_DIM ** -0.5) * LOG2E

LANES = 128
SUBLANES = 8
ROW_BLOCK = 256
Q_BLOCK = 256
K_CHUNK = 512
MOE_ROWS = 256
VMEM_LIMIT = 48 * 1024 * 1024
ATTN_VMEM_LIMIT = 56 * 1024 * 1024

BF16 = jnp.bfloat16
F32 = jnp.float32

_NT = (((1,), (1,)), ((), ()))


def _dot(a, b):
    return jnp.dot(a, b, preferred_element_type=F32)


def _dot_nt(a, b):
    return lax.dot_general(a, b, _NT, preferred_element_type=F32)


def _params(*sem, vmem=VMEM_LIMIT):
    return pltpu.CompilerParams(dimension_semantics=sem, vmem_limit_bytes=vmem)


def _rope(h, cos, sin_signed, half):
    width = h.shape[1]
    lane = lax.broadcasted_iota(jnp.int32, h.shape, 1)
    first = (lane % (2 * half)) < half
    partner = jnp.where(first, pltpu.roll(h, width - half, 1), pltpu.roll(h, half, 1))
    return h * cos + partner * sin_signed


def _rms(x, g):
    return x * lax.rsqrt(jnp.mean(x * x, axis=1, keepdims=True) + RMS_EPS) * g


def _in_proj_kernel(x_ref, tab_ref, wa_ref, wb_ref, wc_ref, wuq_ref, wuk_ref, wuv_ref, qn_ref, kvn_ref,
                    qa_ref, ka_ref, va_ref, qb_ref, kb_ref, vb_ref, c0_ref, c1_ref, c2_ref, cs_scr):
    xb = x_ref[...].astype(BF16)
    cos_a, sin_a = tab_ref[:, 0:256], tab_ref[:, 256:512]
    cos_b, sin_b = tab_ref[:, 512:640], tab_ref[:, 640:768]
    cos_c, sin_c = tab_ref[:, 768:896], tab_ref[:, 896:1024]

    ha = _dot(xb, wa_ref[...])
    qa_ref[...] = (_rope(ha[:, 0:A_W], cos_a, sin_a, A_HALF // 2) * SCALE_A).astype(BF16)
    ka_ref[...] = _rope(ha[:, A_W:2 * A_W], cos_a, sin_a, A_HALF // 2).astype(BF16)
    va_ref[...] = ha[:, 2 * A_W:3 * A_W].astype(BF16)

    hb = _dot(xb, wb_ref[...])
    cq = _rms(hb[:, 0:B_Q_RANK], qn_ref[...]).astype(BF16)
    ckv = _rms(hb[:, B_Q_RANK:B_Q_RANK + B_KV_RANK], kvn_ref[...]).astype(BF16)
    qb = _dot(cq, wuq_ref[...])
    kb = _dot(ckv, wuk_ref[...])
    vb_ref[...] = _dot(ckv, wuv_ref[...]).astype(BF16)
    k_rope = _rope(hb[:, 384:512], cos_b, sin_b, B_ROPE // 2)
    for h in range(B_HEADS):
        sl = slice(B_SLOT * h, B_SLOT * (h + 1))
        qb_ref[:, sl] = (_rope(qb[:, sl], cos_b, sin_b, B_ROPE // 2) * SCALE_B).astype(BF16)
        kb_ref[:, sl] = (kb[:, sl] + k_rope).astype(BF16)

    hc = _dot(xb, wc_ref[...])
    tm = hc.shape[0]
    for g, c_ref in enumerate((c0_ref, c1_ref, c2_ref)):
        base = 3 * C_GW * g
        dil = C_GROUPS[g][1]
        qkv = [_rope(hc[:, base:base + C_GW], cos_c, sin_c, HEAD_DIM // 2) * SCALE_C,
               _rope(hc[:, base + C_GW:base + 2 * C_GW], cos_c, sin_c, HEAD_DIM // 2),
               hc[:, base + 2 * C_GW:base + 3 * C_GW]]
        for part in range(3):
            cols = slice(C_GW * part, C_GW * (part + 1))
            if dil == 1:
                c_ref[0, :, cols] = qkv[part].astype(BF16)
            else:
                cs_scr[part] = qkv[part]
                for j in range(dil):
                    c_ref[j, :, cols] = cs_scr[part, pl.ds(j, tm // dil, stride=dil), :].astype(BF16)


def _dilated_spec(width, dil, nrep):
    return pl.BlockSpec((None, dil, ROW_BLOCK // dil, width), lambda i: (i // nrep, 0, i % nrep, 0))


def _in_proj(x, tab, lw, batch, seq):
    n = x.shape[0]
    tm = ROW_BLOCK
    nrep = seq // tm
    row = lambda w: pl.BlockSpec((tm, w), lambda i: (i, 0))
    full = lambda a: pl.BlockSpec(a.shape, lambda i: (0,) * a.ndim)
    weights = (lw['w_a'], lw['w_b'], lw['w_c'], lw['w_uq'], lw['w_uk'], lw['w_uv'], lw['q_norm'], lw['kv_norm'])
    out_w = (A_W, A_W, A_W, B_HEADS * B_SLOT, B_HEADS * B_SLOT, B_W)
    return pl.pallas_call(
        _in_proj_kernel,
        grid=(n // tm,),
        in_specs=[row(D_MODEL), pl.BlockSpec((tm, 1024), lambda i: (i % nrep, 0))] + [full(w) for w in weights],
        out_specs=[row(w) for w in out_w] + [_dilated_spec(3 * C_GW, dil, nrep) for _, dil in C_GROUPS],
        out_shape=[jax.ShapeDtypeStruct((n, w), BF16) for w in out_w]
        + [jax.ShapeDtypeStruct((batch, dil, seq // dil, 3 * C_GW), BF16) for _, dil in C_GROUPS],
        scratch_shapes=[pltpu.VMEM((3, tm, C_GW), F32)],
        compiler_params=_params("parallel"),
        name="in_proj",
    )(x, tab, *weights)


def _fold_rows(x, op):
    r, c = x.shape
    return op(x.reshape(r // SUBLANES, SUBLANES, c), axis=0)


def _chunk_start(kc):
    return pl.multiple_of(kc * K_CHUNK, K_CHUNK)


def _stage_v_transposed(v_ref, vt_scr):
    @pl.when(pl.program_id(1) == 0)
    def _():
        def body(kc, carry):
            vt_scr[kc] = v_ref[pl.ds(_chunk_start(kc), K_CHUNK), :].astype(F32).T.astype(BF16)
            return carry
        lax.fori_loop(0, vt_scr.shape[0], body, 0)


def _attn_a_kernel(sc_ref, q_ref, k_ref, v_ref, g_ref, o_ref, vt_scr, s_scr):
    lam = sc_ref[0]
    post = sc_ref[1]
    tq = q_ref.shape[0]
    nkc = k_ref.shape[0] // K_CHUNK
    _stage_v_transposed(v_ref, vt_scr)
    qt = q_ref[...].astype(F32).T
    row = lax.broadcasted_iota(jnp.int32, (A_W, tq), 0)
    neg = jnp.full((SUBLANES, tq), -jnp.inf, F32)
    zero = jnp.zeros((SUBLANES, tq), F32)
    maxes, gammas, inv_l1, heads = {}, {}, {}, []
    for st in range(A_HEADS + 2):
        hs, hp, hw = st, st - 1, st - 2
        do_s, do_p, do_w = hs < A_HEADS, 0 <= hp < A_HEADS, 0 <= hw < A_HEADS
        qts = []
        if do_s:
            for c in range(2):
                lo = A_HALF * (2 * hs + c)
                qts.append(jnp.where((row >= lo) & (row < lo + A_HALF), qt, 0.0).astype(BF16))

        def body(kc, carry, qts=qts, hs=hs, hp=hp, hw=hw, do_s=do_s, do_p=do_p, do_w=do_w):
            m1, m2, l1, l2, acc = carry
            if do_s:
                kr = k_ref[pl.ds(_chunk_start(kc), K_CHUNK), :]
                s1 = _dot(kr, qts[0])
                s2 = _dot(kr, qts[1])
                s_scr[hs % 3, 0, kc] = s1
                s_scr[hs % 3, 1, kc] = s2
                m1 = jnp.maximum(m1, _fold_rows(s1, jnp.max))
                m2 = jnp.maximum(m2, _fold_rows(s2, jnp.max))
            if do_p:
                p1 = jnp.exp2(s_scr[hp % 3, 0, kc] - maxes[hp][0])
                p2 = jnp.exp2(s_scr[hp % 3, 1, kc] - maxes[hp][1])
                s_scr[hp % 3, 0, kc] = p1
                s_scr[hp % 3, 1, kc] = p2
                l1 = l1 + _fold_rows(p1, jnp.sum)
                l2 = l2 + _fold_rows(p2, jnp.sum)
            if do_w:
                w = (s_scr[hw % 3, 0, kc] - gammas[hw] * s_scr[hw % 3, 1, kc]).astype(BF16)
                acc = acc + _dot(vt_scr[kc, HEAD_DIM * hw:HEAD_DIM * (hw + 1), :], w)
            return m1, m2, l1, l2, acc

        m1, m2, l1, l2, acc = lax.fori_loop(
            0, nkc, body, (neg, neg, zero, zero, jnp.zeros((HEAD_DIM, tq), F32)), unroll=2)
        if do_s:
            maxes[hs] = (jnp.max(m1, axis=0, keepdims=True), jnp.max(m2, axis=0, keepdims=True))
        if do_p:
            l1 = jnp.sum(l1, axis=0, keepdims=True)
            l2 = jnp.sum(l2, axis=0, keepdims=True)
            gammas[hp] = lam * l1 / l2
            inv_l1[hp] = 1.0 / l1
        if do_w:
            oh = acc * inv_l1[hw]
            ms = jnp.mean(oh * oh, axis=0, keepdims=True)
            heads.append(oh * lax.rsqrt(ms + RMS_EPS))
    out_t = jnp.concatenate(heads, axis=0)
    o_ref[...] = (out_t.T * g_ref[...] * post).astype(BF16)


def _attn_a(sc, q, k, v, g, batch, seq):
    n = q.shape[0]
    tq = Q_BLOCK
    nq = seq // tq
    nkc = seq // K_CHUNK
    return pl.pallas_call(
        _attn_a_kernel,
        grid=(batch, nq),
        in_specs=[pl.BlockSpec(memory_space=pltpu.SMEM),
                  pl.BlockSpec((tq, A_W), lambda b, i: (b * nq + i, 0)),
                  pl.BlockSpec((seq, A_W), lambda b, i: (b, 0)),
                  pl.BlockSpec((seq, A_W), lambda b, i: (b, 0)),
                  pl.BlockSpec((1, A_W), lambda b, i: (0, 0))],
        out_specs=pl.BlockSpec((tq, A_W), lambda b, i: (b * nq + i, 0)),
        out_shape=jax.ShapeDtypeStruct((n, A_W), BF16),
        scratch_shapes=[pltpu.VMEM((nkc, A_W, K_CHUNK), BF16),
                        pltpu.VMEM((3, 2, nkc, K_CHUNK, tq), F32)],
        compiler_params=_params("arbitrary", "arbitrary", vmem=ATTN_VMEM_LIMIT),
        name="attn_a",
    )(sc, q, k, v, g)


def _attn_b_kernel(q_ref, k_ref, v_ref, o_ref, vt_scr, s_scr):
    tq = q_ref.shape[0]
    nkc = k_ref.shape[0] // K_CHUNK

    @pl.when(pl.program_id(1) == 0)
    def _():
        ones = jnp.ones((B_VT_ROWS - B_V, K_CHUNK), BF16)

        def stage(kc, carry):
            vt = v_ref[pl.ds(_chunk_start(kc), K_CHUNK), :].astype(F32).T.astype(BF16)
            for h in range(B_HEADS):
                vt_scr[kc, B_VT_ROWS * h:B_VT_ROWS * h + B_V, :] = vt[B_V * h:B_V * (h + 1), :]
                vt_scr[kc, B_VT_ROWS * h + B_V:B_VT_ROWS * (h + 1), :] = ones
            return carry
        lax.fori_loop(0, nkc, stage, 0)

    neg = jnp.full((SUBLANES, tq), -jnp.inf, F32)
    acc0 = jnp.zeros((B_VT_ROWS, tq), F32)
    n_pairs = B_HEADS // 2
    maxes, heads = {}, []
    for st in range(n_pairs + 1):
        ps, pw = st, st - 1
        do_s, do_w = ps < n_pairs, pw >= 0
        qts = []
        if do_s:
            for j in range(2):
                sl = slice(B_SLOT * (2 * ps + j), B_SLOT * (2 * ps + j + 1))
                qts.append(q_ref[:, sl].astype(F32).T.astype(BF16))

        def body(kc, carry, qts=qts, ps=ps, pw=pw, do_s=do_s, do_w=do_w):
            ms, accs = list(carry[0]), list(carry[1])
            for j in range(2):
                if do_s:
                    h = 2 * ps + j
                    s = _dot(k_ref[pl.ds(_chunk_start(kc), K_CHUNK), B_SLOT * h:B_SLOT * (h + 1)], qts[j])
                    s_scr[ps % 2, j, kc] = s
                    ms[j] = jnp.maximum(ms[j], _fold_rows(s, jnp.max))
                if do_w:
                    h = 2 * pw + j
                    p = jnp.exp2(s_scr[pw % 2, j, kc] - maxes[h]).astype(BF16)
                    accs[j] = accs[j] + _dot(vt_scr[kc, B_VT_ROWS * h:B_VT_ROWS * (h + 1), :], p)
            return tuple(ms), tuple(accs)

        ms, accs = lax.fori_loop(0, nkc, body, ((neg, neg), (acc0, acc0)), unroll=2)
        for j in range(2):
            if do_s:
                maxes[2 * ps + j] = jnp.max(ms[j], axis=0, keepdims=True)
            if do_w:
                heads.append(accs[j][0:B_V] * (1.0 / accs[j][B_V:B_V + 1]))
    o_ref[...] = jnp.concatenate(heads, axis=0).T.astype(BF16)


def _attn_b(q, k, v, batch, seq):
    n = q.shape[0]
    tq = Q_BLOCK
    nq = seq // tq
    nkc = seq // K_CHUNK
    wq = B_HEADS * B_SLOT
    return pl.pallas_call(
        _attn_b_kernel,
        grid=(batch, nq),
        in_specs=[pl.BlockSpec((tq, wq), lambda b, i: (b * nq + i, 0)),
                  pl.BlockSpec((seq, wq), lambda b, i: (b, 0)),
                  pl.BlockSpec((seq, B_W), lambda b, i: (b, 0))],
        out_specs=pl.BlockSpec((tq, B_W), lambda b, i: (b * nq + i, 0)),
        out_shape=jax.ShapeDtypeStruct((n, B_W), BF16),
        scratch_shapes=[pltpu.VMEM((nkc, B_HEADS * B_VT_ROWS, K_CHUNK), BF16),
                        pltpu.VMEM((2, 2, nkc, K_CHUNK, tq), F32)],
        compiler_params=_params("arbitrary", "arbitrary", vmem=ATTN_VMEM_LIMIT),
        name="attn_b",
    )(q, k, v)


def _attn_c_kernel(own_ref, prev_ref, next_ref, o_ref, lse_ref, *, sub_len):
    i = pl.program_id(2)
    n_classes, tq, _ = own_ref.shape
    tb = C_BLOCK
    win = 2 * tb
    lane = lax.broadcasted_iota(jnp.int32, (tb, C_GW), 1)
    key = lax.broadcasted_iota(jnp.int32, (tb, win), 1)
    band = jnp.abs(key - C_SIDE - lax.broadcasted_iota(jnp.int32, (tb, win), 0)) <= C_SIDE
    for c in range(n_classes):
        kv = jnp.concatenate([prev_ref[c, :, C_GW:], own_ref[c, :, C_GW:], next_ref[c, :, C_GW:]], axis=0)
        for u in range(tq // tb):
            q = own_ref[c, tb * u:tb * (u + 1), 0:C_GW]
            kvw = kv[tb * u + C_SIDE:tb * u + C_SIDE + win]
            kpos = i * tq + tb * u - C_SIDE + key
            valid = band & (kpos >= 0) & (kpos < sub_len)
            o = jnp.zeros((tb, C_GW), F32)
            lse = jnp.zeros((tb, C_GW), F32)
            for hh in range(C_HPG):
                head = (lane >= HEAD_DIM * hh) & (lane < HEAD_DIM * (hh + 1))
                qm = jnp.where(head, q, jnp.zeros_like(q))
                s = jnp.where(valid, _dot_nt(qm, kvw[:, 0:C_GW]), NEG_INF)
                m = jnp.max(s, axis=1, keepdims=True)
                p = jnp.exp2(s - m)
                l = jnp.sum(p, axis=1, keepdims=True)
                oh = _dot(p.astype(BF16), kvw[:, C_GW:2 * C_GW]) * (1.0 / l)
                o = jnp.where(head, oh, o)
                lse = jnp.where(head, m + jnp.log(l) * LOG2E, lse)
            o_ref[c, tb * u:tb * (u + 1), :] = o
            lse_ref[c, tb * u:tb * (u + 1), :] = lse


def _attn_c(c, batch, seq, dil):
    sub_len = seq // dil
    tq = min(C_QUERY_ROWS, sub_len)
    nc = min(dil, C_QUERY_ROWS // tq)
    nsub = tq // C_BLOCK
    last = sub_len // C_BLOCK - 1
    o, lse = pl.pallas_call(
        functools.partial(_attn_c_kernel, sub_len=sub_len),
        grid=(batch, dil // nc, sub_len // tq),
        in_specs=[pl.BlockSpec((None, nc, tq, 3 * C_GW), lambda b, j, i: (b, j, i, 0)),
                  pl.BlockSpec((None, nc, C_BLOCK, 3 * C_GW),
                               lambda b, j, i: (b, j, jnp.maximum(i * nsub - 1, 0), 0)),
                  pl.BlockSpec((None, nc, C_BLOCK, 3 * C_GW),
                               lambda b, j, i: (b, j, jnp.minimum((i + 1) * nsub, last), 0))],
        out_specs=[pl.BlockSpec((None, nc, tq, C_GW), lambda b, j, i: (b, j, i, 0))] * 2,
        out_shape=[jax.ShapeDtypeStruct((batch, dil, sub_len, C_GW), F32)] * 2,
        compiler_params=_params("parallel", "parallel", "parallel"),
        name=f"attn_c_d{dil}",
    )(c, c, c)
    return o, lse


def _layer_norm(z, g, b):
    mu = jnp.mean(z, axis=1, keepdims=True)
    zc = z - mu
    var = jnp.mean(zc * zc, axis=1, keepdims=True)
    return zc * lax.rsqrt(var + LN_EPS) * g + b


def _first_index(hit, lane):
    return jnp.min(jnp.where(hit, lane, LANES), axis=1, keepdims=True)


def _token_order(ref, scr):
    dil, sub, _ = ref.shape
    if dil == 1:
        return ref[0]
    for j in range(dil):
        scr[pl.ds(j, sub, stride=dil), :] = ref[j]
    return scr[...]


def _out_proj_kernel(x_ref, oa_ref, ob_ref, oc0_ref, oc1_ref, oc2_ref, l0_ref, l1_ref, l2_ref,
                     wout_ref, g_ref, b_ref, wrh_ref, wrl_ref, x1_ref, x1t_ref, ri_ref, rw_ref, ord_scr):
    oc = [_token_order(r, ord_scr.at[k]) for k, r in enumerate((oc0_ref, oc1_ref, oc2_ref))]
    la, lb, lc = [_token_order(r, ord_scr.at[3 + k]) for k, r in enumerate((l0_ref, l1_ref, l2_ref))]
    mx = jnp.maximum(jnp.maximum(la, lb), lc)
    ea, eb, ec = jnp.exp2(la - mx), jnp.exp2(lb - mx), jnp.exp2(lc - mx)
    inv = 1.0 / (ea + eb + ec)
    mix = jnp.concatenate(
        [oa_ref[...], ob_ref[...],
         (oc[0] * (ea * inv)).astype(BF16),
         (oc[1] * (eb * inv)).astype(BF16),
         (oc[2] * (ec * inv)).astype(BF16)], axis=1)
    x1 = _layer_norm(DN_ALPHA * x_ref[...] + _dot(mix, wout_ref[...]), g_ref[...], b_ref[...])
    x1_ref[...] = x1
    _store_token_tiles(x1t_ref, x1)

    x_hi = x1.astype(BF16)
    x_lo = (x1 - x_hi.astype(F32)).astype(BF16)
    logits = _dot(x_hi, wrh_ref[...]) + (_dot(x_lo, wrh_ref[...]) + _dot(x_hi, wrl_ref[...]))
    lane = lax.broadcasted_iota(jnp.int32, logits.shape, 1)
    ninf = -jnp.inf
    cl = jnp.where(lane < N_GROUPS, logits, ninf)
    cmax = jnp.max(cl, axis=1, keepdims=True)
    grp = _first_index(cl == cmax, lane)
    pg = 1.0 / jnp.sum(jnp.exp(cl - cmax), axis=1, keepdims=True)
    lo = N_GROUPS + EXPERTS_PER_GROUP * grp
    fl = jnp.where((lane >= lo) & (lane < lo + EXPERTS_PER_GROUP), logits, ninf)
    v1 = jnp.max(fl, axis=1, keepdims=True)
    i1 = _first_index(fl == v1, lane)
    fl2 = jnp.where(lane == i1, ninf, fl)
    v2 = jnp.max(fl2, axis=1, keepdims=True)
    i2 = _first_index(fl2 == v2, lane)
    e21 = jnp.exp(v2 - v1)
    t1 = pg / (1.0 + e21)
    t2 = t1 * e21
    ri_ref[...] = jnp.where(lane == 0, i1 - N_GROUPS, jnp.where(lane == 1, i2 - N_GROUPS, 0))
    rw_ref[...] = jnp.where(lane == 0, t1, jnp.where(lane == 1, t2, 0.0))


def _out_proj(x, oa, ob, ocs, lses, lw, seq):
    n = x.shape[0]
    tm = ROW_BLOCK
    nrep = seq // tm
    row = lambda w: pl.BlockSpec((tm, w), lambda i: (i, 0))
    full = lambda a: pl.BlockSpec(a.shape, lambda i: (0,) * a.ndim)
    dilated = [_dilated_spec(C_GW, dil, nrep) for _, dil in C_GROUPS]
    weights = (lw['w_out'], lw['ln1_g'], lw['ln1_b'], lw['w_router_hi'], lw['w_router_lo'])
    return pl.pallas_call(
        _out_proj_kernel,
        grid=(n // tm,),
        in_specs=[row(D_MODEL), row(A_W), row(B_W)] + dilated + dilated + [full(w) for w in weights],
        out_specs=[row(D_MODEL), pl.BlockSpec((tm * SUBLANES, LANES), lambda i: (i, 0)), row(LANES), row(LANES)],
        out_shape=[jax.ShapeDtypeStruct((n, D_MODEL), F32),
                   jax.ShapeDtypeStruct((n * SUBLANES, LANES), F32),
                   jax.ShapeDtypeStruct((n, LANES), jnp.int32),
                   jax.ShapeDtypeStruct((n, LANES), F32)],
        scratch_shapes=[pltpu.VMEM((2 * len(C_GROUPS), tm, C_GW), F32)],
        compiler_params=_params("parallel"),
        name="out_proj",
    )(x, oa, ob, *ocs, *lses, *weights)


def _store_token_tiles(ref, x):
    rows = x.shape[0]
    for j in range(SUBLANES):
        ref[pl.ds(j, rows, stride=SUBLANES), :] = x[:, LANES * j:LANES * (j + 1)]


def _load_token_tiles(ref, first, rows):
    return [ref[pl.ds(first * SUBLANES + j, rows, stride=SUBLANES), :] for j in range(SUBLANES)]


def _start_row_gathers(idx_ref, idx_smem, src_hbm, dst, sem_i, sem):
    cp = pltpu.make_async_copy(idx_ref.at[0, 0], idx_smem, sem_i)
    cp.start()
    cp.wait()
    for r in range(dst.shape[0] // SUBLANES):
        first = pl.multiple_of(idx_smem[r], SUBLANES)
        pltpu.make_async_copy(src_hbm.at[pl.ds(first, SUBLANES)],
                              dst.at[pl.ds(r * SUBLANES, SUBLANES)], sem).start()


def _prefetched_rows(idx_cur_ref, idx_next_ref, src_hbm, buf, idx_smem, sem_i, sem_g):
    i = pl.program_id(0)
    slot = lax.rem(i, 2)
    tokens = buf.shape[1] // SUBLANES

    @pl.when(i == 0)
    def _():
        _start_row_gathers(idx_cur_ref, idx_smem, src_hbm, buf.at[0], sem_i, sem_g.at[0])

    @pl.when(i + 1 < pl.num_programs(0))
    def _():
        _start_row_gathers(idx_next_ref, idx_smem, src_hbm, buf.at[1 - slot], sem_i, sem_g.at[1 - slot])

    def drain(r, carry):
        pltpu.make_async_copy(src_hbm.at[pl.ds(0, SUBLANES)], buf.at[slot, pl.ds(0, SUBLANES)],
                              sem_g.at[slot]).wait()
        return carry

    lax.fori_loop(0, tokens, drain, 0, unroll=8)
    return slot


def _index_specs(rows, steps, prefetch_args=0):
    if prefetch_args:
        return [pl.BlockSpec((1, 1, rows), lambda i, be: (i, 0, 0)),
                pl.BlockSpec((1, 1, rows), lambda i, be: (jnp.minimum(i + 1, steps - 1), 0, 0))]
    return [pl.BlockSpec((1, 1, rows), lambda i: (i, 0, 0)),
            pl.BlockSpec((1, 1, rows), lambda i: (jnp.minimum(i + 1, steps - 1), 0, 0))]


def _gather_scratch(rows):
    return [pltpu.SMEM((rows,), jnp.int32), pltpu.VMEM((2, rows * SUBLANES, LANES), F32),
            pltpu.SemaphoreType.DMA, pltpu.SemaphoreType.DMA((2,))]


def _expert_kernel(be_ref, tok_ref, tok_next_ref, x_hbm, w1_ref, w3_ref, w2_ref, y_ref,
                   idx_smem, xg, sem_i, sem_g):
    del be_ref
    slot = _prefetched_rows(tok_ref, tok_next_ref, x_hbm, xg, idx_smem, sem_i, sem_g)
    tb = y_ref.shape[0] // SUBLANES
    xb = jnp.concatenate([c.astype(BF16) for c in _load_token_tiles(xg.at[slot], 0, tb)], axis=1)
    h1 = _dot(xb, w1_ref[...])
    h3 = _dot(xb, w3_ref[...])
    hid = (h1 * (1.0 / (1.0 + jnp.exp(-h1))) * h3).astype(BF16)
    _store_token_tiles(y_ref, _dot(hid, w2_ref[...]))


def _experts(blk_exp, slot_tok, x1t, lw):
    n_blocks = blk_exp.shape[0]
    tb = MOE_ROWS
    tok = (slot_tok * SUBLANES).reshape(n_blocks, 1, tb)
    grid_spec = pltpu.PrefetchScalarGridSpec(
        num_scalar_prefetch=1,
        grid=(n_blocks,),
        in_specs=_index_specs(tb, n_blocks, prefetch_args=1) + [
            pl.BlockSpec(memory_space=pl.ANY),
            pl.BlockSpec((None, D_MODEL, D_EXPERT), lambda i, be: (be[i], 0, 0)),
            pl.BlockSpec((None, D_MODEL, D_EXPERT), lambda i, be: (be[i], 0, 0)),
            pl.BlockSpec((None, D_EXPERT, D_MODEL), lambda i, be: (be[i], 0, 0))],
        out_specs=pl.BlockSpec((tb * SUBLANES, LANES), lambda i, be: (i, 0)),
        scratch_shapes=_gather_scratch(tb))
    return pl.pallas_call(
        _expert_kernel,
        grid_spec=grid_spec,
        out_shape=jax.ShapeDtypeStruct((n_blocks * tb * SUBLANES, LANES), F32),
        compiler_params=_params("arbitrary"),
        name="experts",
    )(blk_exp, tok, tok, x1t, lw['w1'], lw['w3'], lw['w2'])


def _combine_kernel(pos_ref, pos_next_ref, x1_ref, rw_ref, ys_hbm, g_ref, b_ref, out_ref,
                    idx_smem, yg, sem_i, sem_g):
    tm = x1_ref.shape[0]
    slot = _prefetched_rows(pos_ref, pos_next_ref, ys_hbm, yg, idx_smem, sem_i, sem_g)
    rw = rw_ref[...]
    w0, w1 = rw[:, 0:1], rw[:, 1:2]
    y0 = _load_token_tiles(yg.at[slot], 0, tm)
    y1 = _load_token_tiles(yg.at[slot], tm, tm)
    y = jnp.concatenate([w0 * a + w1 * b for a, b in zip(y0, y1)], axis=1)
    out_ref[...] = _layer_norm(DN_ALPHA * x1_ref[...] + y, g_ref[...], b_ref[...])


def _combine(pos, x1, rw, ys, lw):
    n = x1.shape[0]
    tm = ROW_BLOCK
    nb = n // tm
    rows = TOP_K_INNER * tm
    pos_blk = (pos * SUBLANES).reshape(nb, tm, TOP_K_INNER).transpose(0, 2, 1).reshape(nb, 1, rows)
    full = lambda a: pl.BlockSpec(a.shape, lambda i: (0,) * a.ndim)
    return pl.pallas_call(
        _combine_kernel,
        grid=(nb,),
        in_specs=_index_specs(rows, nb) + [
            pl.BlockSpec((tm, D_MODEL), lambda i: (i, 0)),
            pl.BlockSpec((tm, LANES), lambda i: (i, 0)),
            pl.BlockSpec(memory_space=pl.ANY),
            full(lw['ln2_g']), full(lw['ln2_b'])],
        out_specs=pl.BlockSpec((tm, D_MODEL), lambda i: (i, 0)),
        out_shape=jax.ShapeDtypeStruct((n, D_MODEL), F32),
        scratch_shapes=_gather_scratch(rows),
        compiler_params=_params("arbitrary"),
        name="combine",
    )(pos_blk, pos_blk, x1, rw, ys, lw['ln2_g'], lw['ln2_b'])


def _dispatch_plan(eid):
    n = eid.shape[0]
    a = n * TOP_K_INNER
    tb = MOE_ROWS
    e_flat = eid.reshape(a)
    order = jnp.argsort(e_flat).astype(jnp.int32)
    experts = jnp.arange(N_EXPERTS, dtype=jnp.int32)
    counts = jnp.sum((e_flat[:, None] == experts[None, :]).astype(jnp.int32), axis=0)
    padded = (counts + tb - 1) // tb * tb
    pad_end = jnp.cumsum(padded)
    pad_start = pad_end - padded
    start = jnp.cumsum(counts) - counts
    shift = pad_start - start
    n_blocks = -(-(a + N_EXPERTS * (tb - 1)) // tb)
    blk_start = jnp.arange(n_blocks, dtype=jnp.int32) * tb
    blk_exp = jnp.minimum(jnp.sum((blk_start[:, None] >= pad_end[None, :]).astype(jnp.int32), axis=1),
                          N_EXPERTS - 1)
    e_slot = jnp.repeat(blk_exp, tb)
    sorted_pos = jnp.arange(n_blocks * tb, dtype=jnp.int32) - shift[e_slot]
    valid = sorted_pos < (start + counts)[e_slot]
    slot_tok = jnp.where(valid, order[jnp.clip(sorted_pos, 0, a - 1)] // TOP_K_INNER, 0)
    dest = shift[e_flat[order]] + jnp.arange(a, dtype=jnp.int32)
    pos = dest[jnp.argsort(order)].reshape(n, TOP_K_INNER)
    return blk_exp, slot_tok, pos


def _rope_tables(seq, dim):
    inv_freq = 1.0 / (ROPE_THETA ** (jnp.arange(0, dim, 2, dtype=F32) / dim))
    ang = jnp.arange(seq, dtype=F32)[:, None] * inv_freq[None, :]
    return jnp.cos(ang), jnp.sin(ang)


def _rope_table_block(seq):
    c16, s16 = _rope_tables(seq, A_HALF)
    c32, s32 = _rope_tables(seq, HEAD_DIM)
    ones = lambda w: jnp.ones((seq, w), F32)
    zeros = lambda w: jnp.zeros((seq, w), F32)
    cos_a = jnp.tile(jnp.concatenate([c16, c16], 1), (1, 2 * A_HEADS))
    sin_a = jnp.tile(jnp.concatenate([-s16, s16], 1), (1, 2 * A_HEADS))
    cos_b = jnp.concatenate([ones(B_NOPE), c16, c16, ones(B_SLOT - B_NOPE - B_ROPE)], 1)
    sin_b = jnp.concatenate([zeros(B_NOPE), -s16, s16, zeros(B_SLOT - B_NOPE - B_ROPE)], 1)
    cos_c = jnp.tile(jnp.concatenate([c32, c32], 1), (1, C_HPG))
    sin_c = jnp.tile(jnp.concatenate([-s32, s32], 1), (1, C_HPG))
    return jnp.concatenate([cos_a, sin_a, cos_b, sin_b, cos_c, sin_c], 1)


def _layer_weights(l, w_in, diff_lambda, diff_subln, mla_q_norm, mla_w_uq, mla_kv_norm, mla_w_ukv, w_out,
                   ln1_g, ln1_b, moe_w_coarse, moe_w_fine, w1, w3, w2, ln2_g, ln2_b):
    wi = w_in[l]
    zc = lambda rows, w: jnp.zeros((rows, w), F32)
    b0 = COL_A
    w_b = jnp.concatenate([wi[:, b0:b0 + B_Q_RANK + B_KV_RANK], zc(D_MODEL, B_NOPE),
                           wi[:, b0 + B_Q_RANK + B_KV_RANK:b0 + COL_B], zc(D_MODEL, B_SLOT - B_NOPE - B_ROPE)], 1)
    c0 = COL_A + COL_B
    cw = C_HPG * HEAD_DIM * len(C_GROUPS)
    w_c = jnp.concatenate([wi[:, c0 + part * cw + g * C_GW:c0 + part * cw + (g + 1) * C_GW]
                           for g in range(len(C_GROUPS)) for part in range(3)], 1)
    qd = B_NOPE + B_ROPE
    w_uq = jnp.concatenate([jnp.concatenate([mla_w_uq[l][:, h * qd:(h + 1) * qd], zc(B_Q_RANK, B_SLOT - qd)], 1)
                            for h in range(B_HEADS)], 1)
    kvd = B_NOPE + B_V
    w_uk = jnp.concatenate([jnp.concatenate([mla_w_ukv[l][:, h * kvd:h * kvd + B_NOPE],
                                             zc(B_KV_RANK, B_SLOT - B_NOPE)], 1) for h in range(B_HEADS)], 1)
    w_uv = jnp.concatenate([mla_w_ukv[l][:, h * kvd + B_NOPE:(h + 1) * kvd] for h in range(B_HEADS)], 1)
    w_router = jnp.concatenate(
        [moe_w_coarse[l]] + [moe_w_fine[l][g] for g in range(N_GROUPS)]
        + [zc(D_MODEL, LANES - N_GROUPS - N_EXPERTS)], 1)
    lam_init = 0.8 - 0.6 * math.exp(-0.3 * l)
    lv = diff_lambda[l].astype(F32)
    lam = jnp.exp(jnp.sum(lv[0] * lv[1])) - jnp.exp(jnp.sum(lv[2] * lv[3])) + lam_init
    return dict(
        w_a=wi[:, 0:COL_A].astype(BF16), w_b=w_b.astype(BF16), w_c=w_c.astype(BF16),
        w_uq=w_uq.astype(BF16), w_uk=w_uk.astype(BF16), w_uv=w_uv.astype(BF16),
        q_norm=mla_q_norm[l].reshape(1, B_Q_RANK), kv_norm=mla_kv_norm[l].reshape(1, B_KV_RANK),
        diff_sc=jnp.stack([lam, jnp.asarray(1.0 - lam_init, F32)]).astype(F32),
        subln=jnp.tile(diff_subln[l], A_HEADS).reshape(1, A_W),
        w_out=w_out[l].astype(BF16), ln1_g=ln1_g[l].reshape(1, D_MODEL), ln1_b=ln1_b[l].reshape(1, D_MODEL),
        w_router_hi=w_router.astype(BF16),
        w_router_lo=(w_router - w_router.astype(BF16).astype(F32)).astype(BF16),
        w1=w1[l], w3=w3[l], w2=w2[l],
        ln2_g=ln2_g[l].reshape(1, D_MODEL), ln2_b=ln2_b[l].reshape(1, D_MODEL))


def _encoder_layer(x, tab, lw, batch, seq):
    qa, ka, va, qb, kb, vb, c0, c1, c2 = _in_proj(x, tab, lw, batch, seq)
    oa = _attn_a(lw['diff_sc'], qa, ka, va, lw['subln'], batch, seq)
    ob = _attn_b(qb, kb, vb, batch, seq)
    ocs, lses = zip(*[_attn_c(c, batch, seq, dil) for c, (_, dil) in zip((c0, c1, c2), C_GROUPS)])
    x1, x1t, ri, rw = _out_proj(x, oa, ob, ocs, lses, lw, seq)
    blk_exp, slot_tok, pos = _dispatch_plan(ri[:, 0:TOP_K_INNER])
    ys = _experts(blk_exp, slot_tok, x1t, lw)
    return _combine(pos, x1, rw, ys, lw)


def kernel(x_prompt, x_sample, w_in, diff_lambda, diff_subln, mla_q_norm, mla_w_uq, mla_kv_norm, mla_w_ukv,
           w_out, ln1_g, ln1_b, moe_w_coarse, moe_w_fine, moe_w1, moe_w3, moe_w2, ln2_g, ln2_b):
    w1, w3, w2 = moe_w1.astype(BF16), moe_w3.astype(BF16), moe_w2.astype(BF16)
    layers = [_layer_weights(l, w_in, diff_lambda, diff_subln, mla_q_norm, mla_w_uq, mla_kv_norm, mla_w_ukv,
                             w_out, ln1_g, ln1_b, moe_w_coarse, moe_w_fine, w1, w3, w2, ln2_g, ln2_b)
              for l in range(DEPTH)]

    def trunk(x):
        batch, seq, _ = x.shape
        tab = _rope_table_block(seq)
        h = x.reshape(batch * seq, D_MODEL)
        for lw in layers:
            h = _encoder_layer(h, tab, lw, batch, seq)
        return h.reshape(batch, seq, D_MODEL)

    return (trunk(x_prompt), trunk(x_sample))
```

```python
import functools
import math

import jax
import jax.numpy as jnp
from jax import lax
from jax.experimental import pallas as pl
from jax.experimental.pallas import tpu as pltpu

D_MODEL = 1024
DEPTH = 4
HEAD_DIM = 64
ROPE_THETA = 10000.0
LN_EPS = 1e-5
RMS_EPS = 1e-6
NEG_INF = -1e30

A_HEADS = 4
A_HALF = HEAD_DIM // 2
A_W = A_HEADS * HEAD_DIM

B_HEADS = 6
B_Q_RANK = 256
B_KV_RANK = 128
B_NOPE = 64
B_ROPE = 32
B_V = 64
B_W = B_HEADS * B_V
B_SLOT = 128
B_VT_ROWS = B_V + 16

C_GROUPS = ((128, 1), (512, 4), (2048, 16))
C_HPG = 2
C_GW = C_HPG * HEAD_DIM
C_SIDE = 64
C_BLOCK = 128
C_QUERY_ROWS = 512

COL_A = 3 * A_W
COL_B = B_Q_RANK + B_KV_RANK + B_ROPE

N_GROUPS = 4
EXPERTS_PER_GROUP = 8
N_EXPERTS = N_GROUPS * EXPERTS_PER_GROUP
TOP_K_INNER = 2
D_EXPERT = 512

DN_ALPHA = (2 * DEPTH) ** 0.25

LOG2E = 1.4426950408889634
SCALE_A = (A_HALF ** -0.5) * LOG2E
SCALE_B = ((B_NOPE + B_ROPE) ** -0.5) * LOG2E
SCALE_C = (HEAD_DIM ** -0.5) * LOG2E

LANES = 128
SUBLANES = 8
ROW_BLOCK = 256
Q_BLOCK = 256
K_CHUNK = 512
MOE_ROWS = 256
VMEM_LIMIT = 48 * 1024 * 1024
ATTN_VMEM_LIMIT = 56 * 1024 * 1024

BF16 = jnp.bfloat16
F32 = jnp.float32

_NT = (((1,), (1,)), ((), ()))


def _dot(a, b):
    return jnp.dot(a, b, preferred_element_type=F32)


def _dot_nt(a, b):
    return lax.dot_general(a, b, _NT, preferred_element_type=F32)


def _params(*sem, vmem=VMEM_LIMIT):
    return pltpu.CompilerParams(dimension_semantics=sem, vmem_limit_bytes=vmem)


def _rope(h, cos, sin_signed, half):
    width = h.shape[1]
    lane = lax.broadcasted_iota(jnp.int32, h.shape, 1)
    first = (lane % (2 * half)) < half
    partner = jnp.where(first, pltpu.roll(h, width - half, 1), pltpu.roll(h, half, 1))
    return h * cos + partner * sin_signed


def _rms(x, g):
    return x * lax.rsqrt(jnp.mean(x * x, axis=1, keepdims=True) + RMS_EPS) * g


def _in_proj_kernel(x_ref, tab_ref, wa_ref, wb_ref, wc_ref, wuq_ref, wuk_ref, wuv_ref, qn_ref, kvn_ref,
                    qa_ref, ka_ref, va_ref, qb_ref, kb_ref, vb_ref, c0_ref, c1_ref, c2_ref, cs_scr):
    xb = x_ref[...].astype(BF16)
    cos_a, sin_a = tab_ref[:, 0:256], tab_ref[:, 256:512]
    cos_b, sin_b = tab_ref[:, 512:640], tab_ref[:, 640:768]
    cos_c, sin_c = tab_ref[:, 768:896], tab_ref[:, 896:1024]

    ha = _dot(xb, wa_ref[...])
    qa_ref[...] = (_rope(ha[:, 0:A_W], cos_a, sin_a, A_HALF // 2) * SCALE_A).astype(BF16)
    ka_ref[...] = _rope(ha[:, A_W:2 * A_W], cos_a, sin_a, A_HALF // 2).astype(BF16)
    va_ref[...] = ha[:, 2 * A_W:3 * A_W].astype(BF16)

    hb = _dot(xb, wb_ref[...])
    cq = _rms(hb[:, 0:B_Q_RANK], qn_ref[...]).astype(BF16)
    ckv = _rms(hb[:, B_Q_RANK:B_Q_RANK + B_KV_RANK], kvn_ref[...]).astype(BF16)
    qb = _dot(cq, wuq_ref[...])
    kb = _dot(ckv, wuk_ref[...])
    vb_ref[...] = _dot(ckv, wuv_ref[...]).astype(BF16)
    k_rope = _rope(hb[:, 384:512], cos_b, sin_b, B_ROPE // 2)
    for h in range(B_HEADS):
        sl = slice(B_SLOT * h, B_SLOT * (h + 1))
        qb_ref[:, sl] = (_rope(qb[:, sl], cos_b, sin_b, B_ROPE // 2) * SCALE_B).astype(BF16)
        kb_ref[:, sl] = (kb[:, sl] + k_rope).astype(BF16)

    hc = _dot(xb, wc_ref[...])
    tm = hc.shape[0]
    for g, c_ref in enumerate((c0_ref, c1_ref, c2_ref)):
        base = 3 * C_GW * g
        dil = C_GROUPS[g][1]
        qkv = [_rope(hc[:, base:base + C_GW], cos_c, sin_c, HEAD_DIM // 2) * SCALE_C,
               _rope(hc[:, base + C_GW:base + 2 * C_GW], cos_c, sin_c, HEAD_DIM // 2),
               hc[:, base + 2 * C_GW:base + 3 * C_GW]]
        for part in range(3):
            cols = slice(C_GW * part, C_GW * (part + 1))
            if dil == 1:
                c_ref[0, :, cols] = qkv[part].astype(BF16)
            else:
                cs_scr[part] = qkv[part]
                for j in range(dil):
                    c_ref[j, :, cols] = cs_scr[part, pl.ds(j, tm // dil, stride=dil), :].astype(BF16)


def _dilated_spec(width, dil, nrep):
    return pl.BlockSpec((None, dil, ROW_BLOCK // dil, width), lambda i: (i // nrep, 0, i % nrep, 0))


def _in_proj(x, tab, lw, batch, seq):
    n = x.shape[0]
    tm = ROW_BLOCK
    nrep = seq // tm
    row = lambda w: pl.BlockSpec((tm, w), lambda i: (i, 0))
    full = lambda a: pl.BlockSpec(a.shape, lambda i: (0,) * a.ndim)
    weights = (lw['w_a'], lw['w_b'], lw['w_c'], lw['w_uq'], lw['w_uk'], lw['w_uv'], lw['q_norm'], lw['kv_norm'])
    out_w = (A_W, A_W, A_W, B_HEADS * B_SLOT, B_HEADS * B_SLOT, B_W)
    return pl.pallas_call(
        _in_proj_kernel,
        grid=(n // tm,),
        in_specs=[row(D_MODEL), pl.BlockSpec((tm, 1024), lambda i: (i % nrep, 0))] + [full(w) for w in weights],
        out_specs=[row(w) for w in out_w] + [_dilated_spec(3 * C_GW, dil, nrep) for _, dil in C_GROUPS],
        out_shape=[jax.ShapeDtypeStruct((n, w), BF16) for w in out_w]
        + [jax.ShapeDtypeStruct((batch, dil, seq // dil, 3 * C_GW), BF16) for _, dil in C_GROUPS],
        scratch_shapes=[pltpu.VMEM((3, tm, C_GW), F32)],
        compiler_params=_params("parallel"),
        name="in_proj",
    )(x, tab, *weights)


def _fold_rows(x, op):
    r, c = x.shape
    return op(x.reshape(r // SUBLANES, SUBLANES, c), axis=0)


def _chunk_start(kc):
    return pl.multiple_of(kc * K_CHUNK, K_CHUNK)


def _stage_v_transposed(v_ref, vt_scr):
    @pl.when(pl.program_id(1) == 0)
    def _():
        def body(kc, carry):
            vt_scr[kc] = v_ref[pl.ds(_chunk_start(kc), K_CHUNK), :].astype(F32).T.astype(BF16)
            return carry
        lax.fori_loop(0, vt_scr.shape[0], body, 0)


def _attn_a_kernel(sc_ref, q_ref, k_ref, v_ref, g_ref, o_ref, vt_scr, s_scr):
    lam = sc_ref[0]
    post = sc_ref[1]
    tq = q_ref.shape[0]
    nkc = k_ref.shape[0] // K_CHUNK
    _stage_v_transposed(v_ref, vt_scr)
    qt = q_ref[...].astype(F32).T
    row = lax.broadcasted_iota(jnp.int32, (A_W, tq), 0)
    neg = jnp.full((SUBLANES, tq), -jnp.inf, F32)
    zero = jnp.zeros((SUBLANES, tq), F32)
    maxes, gammas, inv_l1, heads = {}, {}, {}, []
    for st in range(A_HEADS + 2):
        hs, hp, hw = st, st - 1, st - 2
        do_s, do_p, do_w = hs < A_HEADS, 0 <= hp < A_HEADS, 0 <= hw < A_HEADS
        qts = []
        if do_s:
            for c in range(2):
                lo = A_HALF * (2 * hs + c)
                qts.append(jnp.where((row >= lo) & (row < lo + A_HALF), qt, 0.0).astype(BF16))

        def body(kc, carry, qts=qts, hs=hs, hp=hp, hw=hw, do_s=do_s, do_p=do_p, do_w=do_w):
            m1, m2, l1, l2, acc = carry
            if do_s:
                kr = k_ref[pl.ds(_chunk_start(kc), K_CHUNK), :]
                s1 = _dot(kr, qts[0])
                s2 = _dot(kr, qts[1])
                s_scr[hs % 3, 0, kc] = s1
                s_scr[hs % 3, 1, kc] = s2
                m1 = jnp.maximum(m1, _fold_rows(s1, jnp.max))
                m2 = jnp.maximum(m2, _fold_rows(s2, jnp.max))
            if do_p:
                p1 = jnp.exp2(s_scr[hp % 3, 0, kc] - maxes[hp][0])
                p2 = jnp.exp2(s_scr[hp % 3, 1, kc] - maxes[hp][1])
                s_scr[hp % 3, 0, kc] = p1
                s_scr[hp % 3, 1, kc] = p2
                l1 = l1 + _fold_rows(p1, jnp.sum)
                l2 = l2 + _fold_rows(p2, jnp.sum)
            if do_w:
                w = (s_scr[hw % 3, 0, kc] - gammas[hw] * s_scr[hw % 3, 1, kc]).astype(BF16)
                acc = acc + _dot(vt_scr[kc, HEAD_DIM * hw:HEAD_DIM * (hw + 1), :], w)
            return m1, m2, l1, l2, acc

        m1, m2, l1, l2, acc = lax.fori_loop(
            0, nkc, body, (neg, neg, zero, zero, jnp.zeros((HEAD_DIM, tq), F32)), unroll=True)
        if do_s:
            maxes[hs] = (jnp.max(m1, axis=0, keepdims=True), jnp.max(m2, axis=0, keepdims=True))
        if do_p:
            l1 = jnp.sum(l1, axis=0, keepdims=True)
            l2 = jnp.sum(l2, axis=0, keepdims=True)
            gammas[hp] = lam * l1 / l2
            inv_l1[hp] = 1.0 / l1
        if do_w:
            oh = acc * inv_l1[hw]
            ms = jnp.mean(oh * oh, axis=0, keepdims=True)
            heads.append(oh * lax.rsqrt(ms + RMS_EPS))
    out_t = jnp.concatenate(heads, axis=0)
    o_ref[...] = (out_t.T * g_ref[...] * post).astype(BF16)


def _attn_a(sc, q, k, v, g, batch, seq):
    n = q.shape[0]
    tq = Q_BLOCK
    nq = seq // tq
    nkc = seq // K_CHUNK
    return pl.pallas_call(
        _attn_a_kernel,
        grid=(batch, nq),
        in_specs=[pl.BlockSpec(memory_space=pltpu.SMEM),
                  pl.BlockSpec((tq, A_W), lambda b, i: (b * nq + i, 0)),
                  pl.BlockSpec((seq, A_W), lambda b, i: (b, 0)),
                  pl.BlockSpec((seq, A_W), lambda b, i: (b, 0)),
                  pl.BlockSpec((1, A_W), lambda b, i: (0, 0))],
        out_specs=pl.BlockSpec((tq, A_W), lambda b, i: (b * nq + i, 0)),
        out_shape=jax.ShapeDtypeStruct((n, A_W), BF16),
        scratch_shapes=[pltpu.VMEM((nkc, A_W, K_CHUNK), BF16),
                        pltpu.VMEM((3, 2, nkc, K_CHUNK, tq), F32)],
        compiler_params=_params("arbitrary", "arbitrary", vmem=ATTN_VMEM_LIMIT),
        name="attn_a",
    )(sc, q, k, v, g)


def _attn_b_kernel(q_ref, k_ref, v_ref, o_ref, vt_scr, s_scr):
    tq = q_ref.shape[0]
    nkc = k_ref.shape[0] // K_CHUNK

    @pl.when(pl.program_id(1) == 0)
    def _():
        ones = jnp.ones((B_VT_ROWS - B_V, K_CHUNK), BF16)

        def stage(kc, carry):
            vt = v_ref[pl.ds(_chunk_start(kc), K_CHUNK), :].astype(F32).T.astype(BF16)
            for h in range(B_HEADS):
                vt_scr[kc, B_VT_ROWS * h:B_VT_ROWS * h + B_V, :] = vt[B_V * h:B_V * (h + 1), :]
                vt_scr[kc, B_VT_ROWS * h + B_V:B_VT_ROWS * (h + 1), :] = ones
            return carry
        lax.fori_loop(0, nkc, stage, 0)

    neg = jnp.full((SUBLANES, tq), -jnp.inf, F32)
    acc0 = jnp.zeros((B_VT_ROWS, tq), F32)
    n_pairs = B_HEADS // 2
    maxes, heads = {}, []
    for st in range(n_pairs + 1):
        ps, pw = st, st - 1
        do_s, do_w = ps < n_pairs, pw >= 0
        qts = []
        if do_s:
            for j in range(2):
                sl = slice(B_SLOT * (2 * ps + j), B_SLOT * (2 * ps + j + 1))
                qts.append(q_ref[:, sl].astype(F32).T.astype(BF16))

        def body(kc, carry, qts=qts, ps=ps, pw=pw, do_s=do_s, do_w=do_w):
            ms, accs = list(carry[0]), list(carry[1])
            for j in range(2):
                if do_s:
                    h = 2 * ps + j
                    s = _dot(k_ref[pl.ds(_chunk_start(kc), K_CHUNK), B_SLOT * h:B_SLOT * (h + 1)], qts[j])
                    s_scr[ps % 2, j, kc] = s
                    ms[j] = jnp.maximum(ms[j], _fold_rows(s, jnp.max))
                if do_w:
                    h = 2 * pw + j
                    p = jnp.exp2(s_scr[pw % 2, j, kc] - maxes[h]).astype(BF16)
                    accs[j] = accs[j] + _dot(vt_scr[kc, B_VT_ROWS * h:B_VT_ROWS * (h + 1), :], p)
            return tuple(ms), tuple(accs)

        ms, accs = lax.fori_loop(0, nkc, body, ((neg, neg), (acc0, acc0)), unroll=4)
        for j in range(2):
            if do_s:
                maxes[2 * ps + j] = jnp.max(ms[j], axis=0, keepdims=True)
            if do_w:
                heads.append(accs[j][0:B_V] * (1.0 / accs[j][B_V:B_V + 1]))
    o_ref[...] = jnp.concatenate(heads, axis=0).T.astype(BF16)


def _attn_b(q, k, v, batch, seq):
    n = q.shape[0]
    tq = Q_BLOCK
    nq = seq // tq
    nkc = seq // K_CHUNK
    wq = B_HEADS * B_SLOT
    return pl.pallas_call(
        _attn_b_kernel,
        grid=(batch, nq),
        in_specs=[pl.BlockSpec((tq, wq), lambda b, i: (b * nq + i, 0)),
                  pl.BlockSpec((seq, wq), lambda b, i: (b, 0)),
                  pl.BlockSpec((seq, B_W), lambda b, i: (b, 0))],
        out_specs=pl.BlockSpec((tq, B_W), lambda b, i: (b * nq + i, 0)),
        out_shape=jax.ShapeDtypeStruct((n, B_W), BF16),
        scratch_shapes=[pltpu.VMEM((nkc, B_HEADS * B_VT_ROWS, K_CHUNK), BF16),
                        pltpu.VMEM((2, 2, nkc, K_CHUNK, tq), F32)],
        compiler_params=_params("arbitrary", "arbitrary", vmem=ATTN_VMEM_LIMIT),
        name="attn_b",
    )(q, k, v)


def _attn_c_kernel(own_ref, prev_ref, next_ref, o_ref, lse_ref, *, sub_len):
    i = pl.program_id(2)
    n_classes, tq, _ = own_ref.shape
    tb = C_BLOCK
    win = 2 * tb
    lane = lax.broadcasted_iota(jnp.int32, (tb, C_GW), 1)
    key = lax.broadcasted_iota(jnp.int32, (tb, win), 1)
    band = jnp.abs(key - C_SIDE - lax.broadcasted_iota(jnp.int32, (tb, win), 0)) <= C_SIDE
    for c in range(n_classes):
        kv = jnp.concatenate([prev_ref[c, :, C_GW:], own_ref[c, :, C_GW:], next_ref[c, :, C_GW:]], axis=0)
        for u in range(tq // tb):
            q = own_ref[c, tb * u:tb * (u + 1), 0:C_GW]
            kvw = kv[tb * u + C_SIDE:tb * u + C_SIDE + win]
            kpos = i * tq + tb * u - C_SIDE + key
            valid = band & (kpos >= 0) & (kpos < sub_len)
            o = jnp.zeros((tb, C_GW), F32)
            lse = jnp.zeros((tb, C_GW), F32)
            for hh in range(C_HPG):
                head = (lane >= HEAD_DIM * hh) & (lane < HEAD_DIM * (hh + 1))
                qm = jnp.where(head, q, jnp.zeros_like(q))
                s = jnp.where(valid, _dot_nt(qm, kvw[:, 0:C_GW]), NEG_INF)
                m = jnp.max(s, axis=1, keepdims=True)
                p = jnp.exp2(s - m)
                l = jnp.sum(p, axis=1, keepdims=True)
                oh = _dot(p.astype(BF16), kvw[:, C_GW:2 * C_GW]) * (1.0 / l)
                o = jnp.where(head, oh, o)
                lse = jnp.where(head, m + jnp.log(l) * LOG2E, lse)
            o_ref[c, tb * u:tb * (u + 1), :] = o
            lse_ref[c, tb * u:tb * (u + 1), :] = lse


def _attn_c(c, batch, seq, dil):
    sub_len = seq // dil
    tq = min(C_QUERY_ROWS, sub_len)
    nc = min(dil, C_QUERY_ROWS // tq)
    nsub = tq // C_BLOCK
    last = sub_len // C_BLOCK - 1
    o, lse = pl.pallas_call(
        functools.partial(_attn_c_kernel, sub_len=sub_len),
        grid=(batch, dil // nc, sub_len // tq),
        in_specs=[pl.BlockSpec((None, nc, tq, 3 * C_GW), lambda b, j, i: (b, j, i, 0)),
                  pl.BlockSpec((None, nc, C_BLOCK, 3 * C_GW),
                               lambda b, j, i: (b, j, jnp.maximum(i * nsub - 1, 0), 0)),
                  pl.BlockSpec((None, nc, C_BLOCK, 3 * C_GW),
                               lambda b, j, i: (b, j, jnp.minimum((i + 1) * nsub, last), 0))],
        out_specs=[pl.BlockSpec((None, nc, tq, C_GW), lambda b, j, i: (b, j, i, 0))] * 2,
        out_shape=[jax.ShapeDtypeStruct((batch, dil, sub_len, C_GW), F32)] * 2,
        compiler_params=_params("parallel", "parallel", "parallel"),
        name=f"attn_c_d{dil}",
    )(c, c, c)
    return o, lse


def _layer_norm(z, g, b):
    mu = jnp.mean(z, axis=1, keepdims=True)
    zc = z - mu
    var = jnp.mean(zc * zc, axis=1, keepdims=True)
    return zc * lax.rsqrt(var + LN_EPS) * g + b


def _first_index(hit, lane):
    return jnp.min(jnp.where(hit, lane, LANES), axis=1, keepdims=True)


def _token_order(ref, scr):
    dil, sub, _ = ref.shape
    if dil == 1:
        return ref[0]
    for j in range(dil):
        scr[pl.ds(j, sub, stride=dil), :] = ref[j]
    return scr[...]


def _out_proj_kernel(x_ref, oa_ref, ob_ref, oc0_ref, oc1_ref, oc2_ref, l0_ref, l1_ref, l2_ref,
                     wout_ref, g_ref, b_ref, wrh_ref, wrl_ref, x1_ref, x1t_ref, ri_ref, rw_ref, ord_scr):
    oc = [_token_order(r, ord_scr.at[k]) for k, r in enumerate((oc0_ref, oc1_ref, oc2_ref))]
    la, lb, lc = [_token_order(r, ord_scr.at[3 + k]) for k, r in enumerate((l0_ref, l1_ref, l2_ref))]
    mx = jnp.maximum(jnp.maximum(la, lb), lc)
    ea, eb, ec = jnp.exp2(la - mx), jnp.exp2(lb - mx), jnp.exp2(lc - mx)
    inv = 1.0 / (ea + eb + ec)
    mix = jnp.concatenate(
        [oa_ref[...], ob_ref[...],
         (oc[0] * (ea * inv)).astype(BF16),
         (oc[1] * (eb * inv)).astype(BF16),
         (oc[2] * (ec * inv)).astype(BF16)], axis=1)
    x1 = _layer_norm(DN_ALPHA * x_ref[...] + _dot(mix, wout_ref[...]), g_ref[...], b_ref[...])
    x1_ref[...] = x1
    _store_token_tiles(x1t_ref, x1)

    x_hi = x1.astype(BF16)
    x_lo = (x1 - x_hi.astype(F32)).astype(BF16)
    logits = _dot(x_hi, wrh_ref[...]) + (_dot(x_lo, wrh_ref[...]) + _dot(x_hi, wrl_ref[...]))
    lane = lax.broadcasted_iota(jnp.int32, logits.shape, 1)
    ninf = -jnp.inf
    cl = jnp.where(lane < N_GROUPS, logits, ninf)
    cmax = jnp.max(cl, axis=1, keepdims=True)
    grp = _first_index(cl == cmax, lane)
    pg = 1.0 / jnp.sum(jnp.exp(cl - cmax), axis=1, keepdims=True)
    lo = N_GROUPS + EXPERTS_PER_GROUP * grp
    fl = jnp.where((lane >= lo) & (lane < lo + EXPERTS_PER_GROUP), logits, ninf)
    v1 = jnp.max(fl, axis=1, keepdims=True)
    i1 = _first_index(fl == v1, lane)
    fl2 = jnp.where(lane == i1, ninf, fl)
    v2 = jnp.max(fl2, axis=1, keepdims=True)
    i2 = _first_index(fl2 == v2, lane)
    e21 = jnp.exp(v2 - v1)
    t1 = pg / (1.0 + e21)
    t2 = t1 * e21
    ri_ref[...] = jnp.where(lane == 0, i1 - N_GROUPS, jnp.where(lane == 1, i2 - N_GROUPS, 0))
    rw_ref[...] = jnp.where(lane == 0, t1, jnp.where(lane == 1, t2, 0.0))


def _out_proj(x, oa, ob, ocs, lses, lw, seq):
    n = x.shape[0]
    tm = ROW_BLOCK
    nrep = seq // tm
    row = lambda w: pl.BlockSpec((tm, w), lambda i: (i, 0))
    full = lambda a: pl.BlockSpec(a.shape, lambda i: (0,) * a.ndim)
    dilated = [_dilated_spec(C_GW, dil, nrep) for _, dil in C_GROUPS]
    weights = (lw['w_out'], lw['ln1_g'], lw['ln1_b'], lw['w_router_hi'], lw['w_router_lo'])
    return pl.pallas_call(
        _out_proj_kernel,
        grid=(n // tm,),
        in_specs=[row(D_MODEL), row(A_W), row(B_W)] + dilated + dilated + [full(w) for w in weights],
        out_specs=[row(D_MODEL), pl.BlockSpec((tm * SUBLANES, LANES), lambda i: (i, 0)), row(LANES), row(LANES)],
        out_shape=[jax.ShapeDtypeStruct((n, D_MODEL), F32),
                   jax.ShapeDtypeStruct((n * SUBLANES, LANES), F32),
                   jax.ShapeDtypeStruct((n, LANES), jnp.int32),
                   jax.ShapeDtypeStruct((n, LANES), F32)],
        scratch_shapes=[pltpu.VMEM((2 * len(C_GROUPS), tm, C_GW), F32)],
        compiler_params=_params("parallel"),
        name="out_proj",
    )(x, oa, ob, *ocs, *lses, *weights)


def _store_token_tiles(ref, x):
    rows = x.shape[0]
    for j in range(SUBLANES):
        ref[pl.ds(j, rows, stride=SUBLANES), :] = x[:, LANES * j:LANES * (j + 1)]


def _load_token_tiles(ref, first, rows):
    return [ref[pl.ds(first * SUBLANES + j, rows, stride=SUBLANES), :] for j in range(SUBLANES)]


def _start_row_gathers(idx_ref, idx_smem, src_hbm, dst, sem_i, sem):
    cp = pltpu.make_async_copy(idx_ref.at[0, 0], idx_smem, sem_i)
    cp.start()
    cp.wait()
    for r in range(dst.shape[0] // SUBLANES):
        first = pl.multiple_of(idx_smem[r], SUBLANES)
        pltpu.make_async_copy(src_hbm.at[pl.ds(first, SUBLANES)],
                              dst.at[pl.ds(r * SUBLANES, SUBLANES)], sem).start(priority=r % 2)


def _prefetched_rows(idx_cur_ref, idx_next_ref, src_hbm, buf, idx_smem, sem_i, sem_g):
    i = pl.program_id(0)
    slot = lax.rem(i, 2)
    tokens = buf.shape[1] // SUBLANES

    @pl.when(i == 0)
    def _():
        _start_row_gathers(idx_cur_ref, idx_smem, src_hbm, buf.at[0], sem_i, sem_g.at[0])

    @pl.when(i + 1 < pl.num_programs(0))
    def _():
        _start_row_gathers(idx_next_ref, idx_smem, src_hbm, buf.at[1 - slot], sem_i, sem_g.at[1 - slot])

    def drain(r, carry):
        pltpu.make_async_copy(src_hbm.at[pl.ds(0, SUBLANES)], buf.at[slot, pl.ds(0, SUBLANES)],
                              sem_g.at[slot]).wait()
        return carry

    lax.fori_loop(0, tokens, drain, 0, unroll=8)
    return slot


def _index_specs(rows, steps, prefetch_args=0):
    if prefetch_args:
        return [pl.BlockSpec((1, 1, rows), lambda i, be: (i, 0, 0)),
                pl.BlockSpec((1, 1, rows), lambda i, be: (jnp.minimum(i + 1, steps - 1), 0, 0))]
    return [pl.BlockSpec((1, 1, rows), lambda i: (i, 0, 0)),
            pl.BlockSpec((1, 1, rows), lambda i: (jnp.minimum(i + 1, steps - 1), 0, 0))]


def _gather_scratch(rows):
    return [pltpu.SMEM((rows,), jnp.int32), pltpu.VMEM((2, rows * SUBLANES, LANES), F32),
            pltpu.SemaphoreType.DMA, pltpu.SemaphoreType.DMA((2,))]


def _expert_kernel(be_ref, tok_ref, tok_next_ref, x_hbm, w1_ref, w3_ref, w2_ref, y_ref,
                   idx_smem, xg, sem_i, sem_g):
    del be_ref
    slot = _prefetched_rows(tok_ref, tok_next_ref, x_hbm, xg, idx_smem, sem_i, sem_g)
    tb = y_ref.shape[0] // SUBLANES
    xb = jnp.concatenate([c.astype(BF16) for c in _load_token_tiles(xg.at[slot], 0, tb)], axis=1)
    h1 = _dot(xb, w1_ref[...])
    h3 = _dot(xb, w3_ref[...])
    hid = (h1 * (1.0 / (1.0 + jnp.exp(-h1))) * h3).astype(BF16)
    _store_token_tiles(y_ref, _dot(hid, w2_ref[...]))


def _experts(blk_exp, slot_tok, x1t, lw):
    n_blocks = blk_exp.shape[0]
    tb = MOE_ROWS
    tok = (slot_tok * SUBLANES).reshape(n_blocks, 1, tb)
    grid_spec = pltpu.PrefetchScalarGridSpec(
        num_scalar_prefetch=1,
        grid=(n_blocks,),
        in_specs=_index_specs(tb, n_blocks, prefetch_args=1) + [
            pl.BlockSpec(memory_space=pl.ANY),
            pl.BlockSpec((None, D_MODEL, D_EXPERT), lambda i, be: (be[i], 0, 0)),
            pl.BlockSpec((None, D_MODEL, D_EXPERT), lambda i, be: (be[i], 0, 0)),
            pl.BlockSpec((None, D_EXPERT, D_MODEL), lambda i, be: (be[i], 0, 0))],
        out_specs=pl.BlockSpec((tb * SUBLANES, LANES), lambda i, be: (i, 0)),
        scratch_shapes=_gather_scratch(tb))
    return pl.pallas_call(
        _expert_kernel,
        grid_spec=grid_spec,
        out_shape=jax.ShapeDtypeStruct((n_blocks * tb * SUBLANES, LANES), F32),
        compiler_params=_params("arbitrary"),
        name="experts",
    )(blk_exp, tok, tok, x1t, lw['w1'], lw['w3'], lw['w2'])


def _combine_kernel(pos_ref, pos_next_ref, x1_ref, rw_ref, ys_hbm, g_ref, b_ref, out_ref,
                    idx_smem, yg, sem_i, sem_g):
    tm = x1_ref.shape[0]
    slot = _prefetched_rows(pos_ref, pos_next_ref, ys_hbm, yg, idx_smem, sem_i, sem_g)
    rw = rw_ref[...]
    w0, w1 = rw[:, 0:1], rw[:, 1:2]
    y0 = _load_token_tiles(yg.at[slot], 0, tm)
    y1 = _load_token_tiles(yg.at[slot], tm, tm)
    y = jnp.concatenate([w0 * a + w1 * b for a, b in zip(y0, y1)], axis=1)
    out_ref[...] = _layer_norm(DN_ALPHA * x1_ref[...] + y, g_ref[...], b_ref[...])


def _combine(pos, x1, rw, ys, lw):
    n = x1.shape[0]
    tm = ROW_BLOCK
    nb = n // tm
    rows = TOP_K_INNER * tm
    pos_blk = (pos * SUBLANES).reshape(nb, tm, TOP_K_INNER).transpose(0, 2, 1).reshape(nb, 1, rows)
    full = lambda a: pl.BlockSpec(a.shape, lambda i: (0,) * a.ndim)
    return pl.pallas_call(
        _combine_kernel,
        grid=(nb,),
        in_specs=_index_specs(rows, nb) + [
            pl.BlockSpec((tm, D_MODEL), lambda i: (i, 0)),
            pl.BlockSpec((tm, LANES), lambda i: (i, 0)),
            pl.BlockSpec(memory_space=pl.ANY),
            full(lw['ln2_g']), full(lw['ln2_b'])],
        out_specs=pl.BlockSpec((tm, D_MODEL), lambda i: (i, 0)),
        out_shape=jax.ShapeDtypeStruct((n, D_MODEL), F32),
        scratch_shapes=_gather_scratch(rows),
        compiler_params=_params("arbitrary"),
        name="combine",
    )(pos_blk, pos_blk, x1, rw, ys, lw['ln2_g'], lw['ln2_b'])


def _dispatch_plan(eid):
    n = eid.shape[0]
    a = n * TOP_K_INNER
    tb = MOE_ROWS
    e_flat = eid.reshape(a)
    order = jnp.argsort(e_flat).astype(jnp.int32)
    experts = jnp.arange(N_EXPERTS, dtype=jnp.int32)
    counts = jnp.sum((e_flat[:, None] == experts[None, :]).astype(jnp.int32), axis=0)
    padded = (counts + tb - 1) // tb * tb
    pad_end = jnp.cumsum(padded)
    pad_start = pad_end - padded
    start = jnp.cumsum(counts) - counts
    shift = pad_start - start
    n_blocks = -(-(a + N_EXPERTS * (tb - 1)) // tb)
    blk_start = jnp.arange(n_blocks, dtype=jnp.int32) * tb
    blk_exp = jnp.minimum(jnp.sum((blk_start[:, None] >= pad_end[None, :]).astype(jnp.int32), axis=1),
                          N_EXPERTS - 1)
    e_slot = jnp.repeat(blk_exp, tb)
    sorted_pos = jnp.arange(n_blocks * tb, dtype=jnp.int32) - shift[e_slot]
    valid = sorted_pos < (start + counts)[e_slot]
    slot_tok = jnp.where(valid, order[jnp.clip(sorted_pos, 0, a - 1)] // TOP_K_INNER, 0)
    dest = shift[e_flat[order]] + jnp.arange(a, dtype=jnp.int32)
    pos = dest[jnp.argsort(order)].reshape(n, TOP_K_INNER)
    return blk_exp, slot_tok, pos


def _rope_tables(seq, dim):
    inv_freq = 1.0 / (ROPE_THETA ** (jnp.arange(0, dim, 2, dtype=F32) / dim))
    ang = jnp.arange(seq, dtype=F32)[:, None] * inv_freq[None, :]
    return jnp.cos(ang), jnp.sin(ang)


def _rope_table_block(seq):
    c16, s16 = _rope_tables(seq, A_HALF)
    c32, s32 = _rope_tables(seq, HEAD_DIM)
    ones = lambda w: jnp.ones((seq, w), F32)
    zeros = lambda w: jnp.zeros((seq, w), F32)
    cos_a = jnp.tile(jnp.concatenate([c16, c16], 1), (1, 2 * A_HEADS))
    sin_a = jnp.tile(jnp.concatenate([-s16, s16], 1), (1, 2 * A_HEADS))
    cos_b = jnp.concatenate([ones(B_NOPE), c16, c16, ones(B_SLOT - B_NOPE - B_ROPE)], 1)
    sin_b = jnp.concatenate([zeros(B_NOPE), -s16, s16, zeros(B_SLOT - B_NOPE - B_ROPE)], 1)
    cos_c = jnp.tile(jnp.concatenate([c32, c32], 1), (1, C_HPG))
    sin_c = jnp.tile(jnp.concatenate([-s32, s32], 1), (1, C_HPG))
    return jnp.concatenate([cos_a, sin_a, cos_b, sin_b, cos_c, sin_c], 1)


def _layer_weights(l, w_in, diff_lambda, diff_subln, mla_q_norm, mla_w_uq, mla_kv_norm, mla_w_ukv, w_out,
                   ln1_g, ln1_b, moe_w_coarse, moe_w_fine, w1, w3, w2, ln2_g, ln2_b):
    wi = w_in[l]
    zc = lambda rows, w: jnp.zeros((rows, w), F32)
    b0 = COL_A
    w_b = jnp.concatenate([wi[:, b0:b0 + B_Q_RANK + B_KV_RANK], zc(D_MODEL, B_NOPE),
                           wi[:, b0 + B_Q_RANK + B_KV_RANK:b0 + COL_B], zc(D_MODEL, B_SLOT - B_NOPE - B_ROPE)], 1)
    c0 = COL_A + COL_B
    cw = C_HPG * HEAD_DIM * len(C_GROUPS)
    w_c = jnp.concatenate([wi[:, c0 + part * cw + g * C_GW:c0 + part * cw + (g + 1) * C_GW]
                           for g in range(len(C_GROUPS)) for part in range(3)], 1)
    qd = B_NOPE + B_ROPE
    w_uq = jnp.concatenate([jnp.concatenate([mla_w_uq[l][:, h * qd:(h + 1) * qd], zc(B_Q_RANK, B_SLOT - qd)], 1)
                            for h in range(B_HEADS)], 1)
    kvd = B_NOPE + B_V
    w_uk = jnp.concatenate([jnp.concatenate([mla_w_ukv[l][:, h * kvd:h * kvd + B_NOPE],
                                             zc(B_KV_RANK, B_SLOT - B_NOPE)], 1) for h in range(B_HEADS)], 1)
    w_uv = jnp.concatenate([mla_w_ukv[l][:, h * kvd + B_NOPE:(h + 1) * kvd] for h in range(B_HEADS)], 1)
    w_router = jnp.concatenate(
        [moe_w_coarse[l]] + [moe_w_fine[l][g] for g in range(N_GROUPS)]
        + [zc(D_MODEL, LANES - N_GROUPS - N_EXPERTS)], 1)
    lam_init = 0.8 - 0.6 * math.exp(-0.3 * l)
    lv = diff_lambda[l].astype(F32)
    lam = jnp.exp(jnp.sum(lv[0] * lv[1])) - jnp.exp(jnp.sum(lv[2] * lv[3])) + lam_init
    return dict(
        w_a=wi[:, 0:COL_A].astype(BF16), w_b=w_b.astype(BF16), w_c=w_c.astype(BF16),
        w_uq=w_uq.astype(BF16), w_uk=w_uk.astype(BF16), w_uv=w_uv.astype(BF16),
        q_norm=mla_q_norm[l].reshape(1, B_Q_RANK), kv_norm=mla_kv_norm[l].reshape(1, B_KV_RANK),
        diff_sc=jnp.stack([lam, jnp.asarray(1.0 - lam_init, F32)]).astype(F32),
        subln=jnp.tile(diff_subln[l], A_HEADS).reshape(1, A_W),
        w_out=w_out[l].astype(BF16), ln1_g=ln1_g[l].reshape(1, D_MODEL), ln1_b=ln1_b[l].reshape(1, D_MODEL),
        w_router_hi=w_router.astype(BF16),
        w_router_lo=(w_router - w_router.astype(BF16).astype(F32)).astype(BF16),
        w1=w1[l], w3=w3[l], w2=w2[l],
        ln2_g=ln2_g[l].reshape(1, D_MODEL), ln2_b=ln2_b[l].reshape(1, D_MODEL))


def _encoder_layer(x, tab, lw, batch, seq):
    qa, ka, va, qb, kb, vb, c0, c1, c2 = _in_proj(x, tab, lw, batch, seq)
    oa = _attn_a(lw['diff_sc'], qa, ka, va, lw['subln'], batch, seq)
    ob = _attn_b(qb, kb, vb, batch, seq)
    ocs, lses = zip(*[_attn_c(c, batch, seq, dil) for c, (_, dil) in zip((c0, c1, c2), C_GROUPS)])
    x1, x1t, ri, rw = _out_proj(x, oa, ob, ocs, lses, lw, seq)
    blk_exp, slot_tok, pos = _dispatch_plan(ri[:, 0:TOP_K_INNER])
    ys = _experts(blk_exp, slot_tok, x1t, lw)
    return _combine(pos, x1, rw, ys, lw)


def kernel(x_prompt, x_sample, w_in, diff_lambda, diff_subln, mla_q_norm, mla_w_uq, mla_kv_norm, mla_w_ukv,
           w_out, ln1_g, ln1_b, moe_w_coarse, moe_w_fine, moe_w1, moe_w3, moe_w2, ln2_g, ln2_b):
    w1, w3, w2 = moe_w1.astype(BF16), moe_w3.astype(BF16), moe_w2.astype(BF16)
    layers = [_layer_weights(l, w_in, diff_lambda, diff_subln, mla_q_norm, mla_w_uq, mla_kv_norm, mla_w_ukv,
                             w_out, ln1_g, ln1_b, moe_w_coarse, moe_w_fine, w1, w3, w2, ln2_g, ln2_b)
              for l in range(DEPTH)]

    def trunk(x):
        batch, seq, _ = x.shape
        tab = _rope_table_block(seq)
        h = x.reshape(batch * seq, D_MODEL)
        for lw in layers:
            h = _encoder_layer(h, tab, lw, batch, seq)
        return h.reshape(batch, seq, D_MODEL)

    return (trunk(x_prompt), trunk(x_sample))
```

```python
import functools
import math

import jax
import jax.numpy as jnp
from jax import lax
from jax.experimental import pallas as pl
from jax.experimental.pallas import tpu as pltpu

D_MODEL = 1024
DEPTH = 4
HEAD_DIM = 64
ROPE_THETA = 10000.0
LN_EPS = 1e-5
RMS_EPS = 1e-6
NEG_INF = -1e30

A_HEADS = 4
A_HALF = HEAD_DIM // 2
A_W = A_HEADS * HEAD_DIM

B_HEADS = 6
B_Q_RANK = 256
B_KV_RANK = 128
B_NOPE = 64
B_ROPE = 32
B_V = 64
B_W = B_HEADS * B_V
B_SLOT = 128
B_VT_ROWS = B_V + 16

C_GROUPS = ((128, 1), (512, 4), (2048, 16))
C_HPG = 2
C_GW = C_HPG * HEAD_DIM
C_SIDE = 64
C_BLOCK = 128
C_QUERY_ROWS = 512

COL_A = 3 * A_W
COL_B = B_Q_RANK + B_KV_RANK + B_ROPE

N_GROUPS = 4
EXPERTS_PER_GROUP = 8
N_EXPERTS = N_GROUPS * EXPERTS_PER_GROUP
TOP_K_INNER = 2
D_EXPERT = 512

DN_ALPHA = (2 * DEPTH) ** 0.25

LOG2E = 1.4426950408889634
SCALE_A = (A_HALF ** -0.5) * LOG2E
SCALE_B = ((B_NOPE + B_ROPE) ** -0.5) * LOG2E
SCALE_C = (HEAD_DIM ** -0.5) * LOG2E

LANES = 128
SUBLANES = 8
ROW_BLOCK = 256
Q_BLOCK = 256
K_CHUNK = 512
MOE_ROWS = 256
VMEM_LIMIT = 48 * 1024 * 1024
ATTN_VMEM_LIMIT = 56 * 1024 * 1024

BF16 = jnp.bfloat16
F32 = jnp.float32

_NT = (((1,), (1,)), ((), ()))


def _dot(a, b):
    return jnp.dot(a, b, preferred_element_type=F32)


def _dot_nt(a, b):
    return lax.dot_general(a, b, _NT, preferred_element_type=F32)


def _params(*sem, vmem=VMEM_LIMIT):
    return pltpu.CompilerParams(dimension_semantics=sem, vmem_limit_bytes=vmem)


def _rope(h, cos, sin_signed, half):
    width = h.shape[1]
    lane = lax.broadcasted_iota(jnp.int32, h.shape, 1)
    first = (lane % (2 * half)) < half
    partner = jnp.where(first, pltpu.roll(h, width - half, 1), pltpu.roll(h, half, 1))
    return h * cos + partner * sin_signed


def _rms(x, g):
    return x * lax.rsqrt(jnp.mean(x * x, axis=1, keepdims=True) + RMS_EPS) * g


def _in_proj_kernel(x_ref, tab_ref, wa_ref, wb_ref, wc_ref, wuq_ref, wuk_ref, wuv_ref, qn_ref, kvn_ref,
                    qa_ref, ka_ref, va_ref, qb_ref, kb_ref, vb_ref, c0_ref, c1_ref, c2_ref, cs_scr):
    xb = x_ref[...].astype(BF16)
    cos_a, sin_a = tab_ref[:, 0:256], tab_ref[:, 256:512]
    cos_b, sin_b = tab_ref[:, 512:640], tab_ref[:, 640:768]
    cos_c, sin_c = tab_ref[:, 768:896], tab_ref[:, 896:1024]

    ha = _dot(xb, wa_ref[...])
    qa_ref[...] = (_rope(ha[:, 0:A_W], cos_a, sin_a, A_HALF // 2) * SCALE_A).astype(BF16)
    ka_ref[...] = _rope(ha[:, A_W:2 * A_W], cos_a, sin_a, A_HALF // 2).astype(BF16)
    va_ref[...] = ha[:, 2 * A_W:3 * A_W].astype(BF16)

    hb = _dot(xb, wb_ref[...])
    cq = _rms(hb[:, 0:B_Q_RANK], qn_ref[...]).astype(BF16)
    ckv = _rms(hb[:, B_Q_RANK:B_Q_RANK + B_KV_RANK], kvn_ref[...]).astype(BF16)
    qb = _dot(cq, wuq_ref[...])
    kb = _dot(ckv, wuk_ref[...])
    vb_ref[...] = _dot(ckv, wuv_ref[...]).astype(BF16)
    k_rope = _rope(hb[:, 384:512], cos_b, sin_b, B_ROPE // 2)
    for h in range(B_HEADS):
        sl = slice(B_SLOT * h, B_SLOT * (h + 1))
        qb_ref[:, sl] = (_rope(qb[:, sl], cos_b, sin_b, B_ROPE // 2) * SCALE_B).astype(BF16)
        kb_ref[:, sl] = (kb[:, sl] + k_rope).astype(BF16)

    hc = _dot(xb, wc_ref[...])
    tm = hc.shape[0]
    for g, c_ref in enumerate((c0_ref, c1_ref, c2_ref)):
        base = 3 * C_GW * g
        dil = C_GROUPS[g][1]
        qkv = [_rope(hc[:, base:base + C_GW], cos_c, sin_c, HEAD_DIM // 2) * SCALE_C,
               _rope(hc[:, base + C_GW:base + 2 * C_GW], cos_c, sin_c, HEAD_DIM // 2),
               hc[:, base + 2 * C_GW:base + 3 * C_GW]]
        for part in range(3):
            cols = slice(C_GW * part, C_GW * (part + 1))
            if dil == 1:
                c_ref[0, :, cols] = qkv[part].astype(BF16)
            else:
                cs_scr[part] = qkv[part]
                for j in range(dil):
                    c_ref[j, :, cols] = cs_scr[part, pl.ds(j, tm // dil, stride=dil), :].astype(BF16)


def _dilated_spec(width, dil, nrep):
    return pl.BlockSpec((None, dil, ROW_BLOCK // dil, width), lambda i: (i // nrep, 0, i % nrep, 0))


def _in_proj(x, tab, lw, batch, seq):
    n = x.shape[0]
    tm = ROW_BLOCK
    nrep = seq // tm
    row = lambda w: pl.BlockSpec((tm, w), lambda i: (i, 0))
    full = lambda a: pl.BlockSpec(a.shape, lambda i: (0,) * a.ndim)
    weights = (lw['w_a'], lw['w_b'], lw['w_c'], lw['w_uq'], lw['w_uk'], lw['w_uv'], lw['q_norm'], lw['kv_norm'])
    out_w = (A_W, A_W, A_W, B_HEADS * B_SLOT, B_HEADS * B_SLOT, B_W)
    return pl.pallas_call(
        _in_proj_kernel,
        grid=(n // tm,),
        in_specs=[row(D_MODEL), pl.BlockSpec((tm, 1024), lambda i: (i % nrep, 0))] + [full(w) for w in weights],
        out_specs=[row(w) for w in out_w] + [_dilated_spec(3 * C_GW, dil, nrep) for _, dil in C_GROUPS],
        out_shape=[jax.ShapeDtypeStruct((n, w), BF16) for w in out_w]
        + [jax.ShapeDtypeStruct((batch, dil, seq // dil, 3 * C_GW), BF16) for _, dil in C_GROUPS],
        scratch_shapes=[pltpu.VMEM((3, tm, C_GW), F32)],
        compiler_params=_params("parallel"),
        name="in_proj",
    )(x, tab, *weights)


def _fold_rows(x, op):
    r, c = x.shape
    return op(x.reshape(r // SUBLANES, SUBLANES, c), axis=0)


def _chunk_start(kc):
    return pl.multiple_of(kc * K_CHUNK, K_CHUNK)


def _stage_v_transposed(v_ref, vt_scr):
    @pl.when(pl.program_id(1) == 0)
    def _():
        def body(kc, carry):
            vt_scr[kc] = v_ref[pl.ds(_chunk_start(kc), K_CHUNK), :].astype(F32).T.astype(BF16)
            return carry
        lax.fori_loop(0, vt_scr.shape[0], body, 0)


def _attn_a_kernel(sc_ref, q_ref, k_ref, v_ref, g_ref, o_ref, vt_scr, s_scr):
    lam = sc_ref[0]
    post = sc_ref[1]
    tq = q_ref.shape[0]
    nkc = k_ref.shape[0] // K_CHUNK
    _stage_v_transposed(v_ref, vt_scr)
    qt = q_ref[...].astype(F32).T
    row = lax.broadcasted_iota(jnp.int32, (A_W, tq), 0)
    neg = jnp.full((SUBLANES, tq), -jnp.inf, F32)
    zero = jnp.zeros((SUBLANES, tq), F32)
    maxes, gammas, inv_l1, heads = {}, {}, {}, []
    for st in range(A_HEADS + 2):
        hs, hp, hw = st, st - 1, st - 2
        do_s, do_p, do_w = hs < A_HEADS, 0 <= hp < A_HEADS, 0 <= hw < A_HEADS
        qts = []
        if do_s:
            for c in range(2):
                lo = A_HALF * (2 * hs + c)
                qts.append(jnp.where((row >= lo) & (row < lo + A_HALF), qt, 0.0).astype(BF16))

        def body(kc, carry, qts=qts, hs=hs, hp=hp, hw=hw, do_s=do_s, do_p=do_p, do_w=do_w):
            m1, m2, l1, l2, acc = carry
            if do_s:
                kr = k_ref[pl.ds(_chunk_start(kc), K_CHUNK), :]
                s1 = _dot(kr, qts[0])
                s2 = _dot(kr, qts[1])
                s_scr[hs % 3, 0, kc] = s1
                s_scr[hs % 3, 1, kc] = s2
                m1 = jnp.maximum(m1, _fold_rows(s1, jnp.max))
                m2 = jnp.maximum(m2, _fold_rows(s2, jnp.max))
            if do_p:
                p1 = jnp.exp2(s_scr[hp % 3, 0, kc] - maxes[hp][0])
                p2 = jnp.exp2(s_scr[hp % 3, 1, kc] - maxes[hp][1])
                s_scr[hp % 3, 0, kc] = p1
                s_scr[hp % 3, 1, kc] = p2
                l1 = l1 + _fold_rows(p1, jnp.sum)
                l2 = l2 + _fold_rows(p2, jnp.sum)
            if do_w:
                w = (s_scr[hw % 3, 0, kc] - gammas[hw] * s_scr[hw % 3, 1, kc]).astype(BF16)
                acc = acc + _dot(vt_scr[kc, HEAD_DIM * hw:HEAD_DIM * (hw + 1), :], w)
            return m1, m2, l1, l2, acc

        m1, m2, l1, l2, acc = lax.fori_loop(
            0, nkc, body, (neg, neg, zero, zero, jnp.zeros((HEAD_DIM, tq), F32)), unroll=True)
        if do_s:
            maxes[hs] = (jnp.max(m1, axis=0, keepdims=True), jnp.max(m2, axis=0, keepdims=True))
        if do_p:
            l1 = jnp.sum(l1, axis=0, keepdims=True)
            l2 = jnp.sum(l2, axis=0, keepdims=True)
            gammas[hp] = lam * l1 / l2
            inv_l1[hp] = 1.0 / l1
        if do_w:
            oh = acc * inv_l1[hw]
            ms = jnp.mean(oh * oh, axis=0, keepdims=True)
            heads.append(oh * lax.rsqrt(ms + RMS_EPS))
    out_t = jnp.concatenate(heads, axis=0)
    o_ref[...] = (out_t.T * g_ref[...] * post).astype(BF16)


def _attn_a(sc, q, k, v, g, batch, seq):
    n = q.shape[0]
    tq = Q_BLOCK
    nq = seq // tq
    nkc = seq // K_CHUNK
    return pl.pallas_call(
        _attn_a_kernel,
        grid=(batch, nq),
        in_specs=[pl.BlockSpec(memory_space=pltpu.SMEM),
                  pl.BlockSpec((tq, A_W), lambda b, i: (b * nq + i, 0)),
                  pl.BlockSpec((seq, A_W), lambda b, i: (b, 0)),
                  pl.BlockSpec((seq, A_W), lambda b, i: (b, 0)),
                  pl.BlockSpec((1, A_W), lambda b, i: (0, 0))],
        out_specs=pl.BlockSpec((tq, A_W), lambda b, i: (b * nq + i, 0)),
        out_shape=jax.ShapeDtypeStruct((n, A_W), BF16),
        scratch_shapes=[pltpu.VMEM((nkc, A_W, K_CHUNK), BF16),
                        pltpu.VMEM((3, 2, nkc, K_CHUNK, tq), F32)],
        compiler_params=_params("arbitrary", "arbitrary", vmem=ATTN_VMEM_LIMIT),
        name="attn_a",
    )(sc, q, k, v, g)


def _attn_b_kernel(q_ref, k_ref, v_ref, o_ref, vt_scr, s_scr):
    tq = q_ref.shape[0]
    nkc = k_ref.shape[0] // K_CHUNK

    @pl.when(pl.program_id(1) == 0)
    def _():
        ones = jnp.ones((B_VT_ROWS - B_V, K_CHUNK), BF16)

        def stage(kc, carry):
            vt = v_ref[pl.ds(_chunk_start(kc), K_CHUNK), :].astype(F32).T.astype(BF16)
            for h in range(B_HEADS):
                vt_scr[kc, B_VT_ROWS * h:B_VT_ROWS * h + B_V, :] = vt[B_V * h:B_V * (h + 1), :]
                vt_scr[kc, B_VT_ROWS * h + B_V:B_VT_ROWS * (h + 1), :] = ones
            return carry
        lax.fori_loop(0, nkc, stage, 0)

    neg = jnp.full((SUBLANES, tq), -jnp.inf, F32)
    acc0 = jnp.zeros((B_VT_ROWS, tq), F32)
    n_pairs = B_HEADS // 2
    maxes, heads = {}, []
    for st in range(n_pairs + 1):
        ps, pw = st, st - 1
        do_s, do_w = ps < n_pairs, pw >= 0
        qts = []
        if do_s:
            for j in range(2):
                sl = slice(B_SLOT * (2 * ps + j), B_SLOT * (2 * ps + j + 1))
                qts.append(q_ref[:, sl].astype(F32).T.astype(BF16))

        def body(kc, carry, qts=qts, ps=ps, pw=pw, do_s=do_s, do_w=do_w):
            ms, accs = list(carry[0]), list(carry[1])
            for j in range(2):
                if do_s:
                    h = 2 * ps + j
                    s = _dot(k_ref[pl.ds(_chunk_start(kc), K_CHUNK), B_SLOT * h:B_SLOT * (h + 1)], qts[j])
                    s_scr[ps % 2, j, kc] = s
                    ms[j] = jnp.maximum(ms[j], _fold_rows(s, jnp.max))
                if do_w:
                    h = 2 * pw + j
                    p = jnp.exp2(s_scr[pw % 2, j, kc] - maxes[h]).astype(BF16)
                    accs[j] = accs[j] + _dot(vt_scr[kc, B_VT_ROWS * h:B_VT_ROWS * (h + 1), :], p)
            return tuple(ms), tuple(accs)

        ms, accs = lax.fori_loop(0, nkc, body, ((neg, neg), (acc0, acc0)), unroll=4)
        for j in range(2):
            if do_s:
                maxes[2 * ps + j] = jnp.max(ms[j], axis=0, keepdims=True)
            if do_w:
                heads.append(accs[j][0:B_V] * (1.0 / accs[j][B_V:B_V + 1]))
    o_ref[...] = jnp.concatenate(heads, axis=0).T.astype(BF16)


def _attn_b(q, k, v, batch, seq):
    n = q.shape[0]
    tq = Q_BLOCK
    nq = seq // tq
    nkc = seq // K_CHUNK
    wq = B_HEADS * B_SLOT
    return pl.pallas_call(
        _attn_b_kernel,
        grid=(batch, nq),
        in_specs=[pl.BlockSpec((tq, wq), lambda b, i: (b * nq + i, 0)),
                  pl.BlockSpec((seq, wq), lambda b, i: (b, 0)),
                  pl.BlockSpec((seq, B_W), lambda b, i: (b, 0))],
        out_specs=pl.BlockSpec((tq, B_W), lambda b, i: (b * nq + i, 0)),
        out_shape=jax.ShapeDtypeStruct((n, B_W), BF16),
        scratch_shapes=[pltpu.VMEM((nkc, B_HEADS * B_VT_ROWS, K_CHUNK), BF16),
                        pltpu.VMEM((2, 2, nkc, K_CHUNK, tq), F32)],
        compiler_params=_params("arbitrary", "arbitrary", vmem=ATTN_VMEM_LIMIT),
        name="attn_b",
    )(q, k, v)


def _attn_c_kernel(own_ref, prev_ref, next_ref, o_ref, lse_ref, *, sub_len):
    i = pl.program_id(2)
    n_classes, tq, _ = own_ref.shape
    tb = C_BLOCK
    win = 2 * tb
    lane = lax.broadcasted_iota(jnp.int32, (tb, C_GW), 1)
    key = lax.broadcasted_iota(jnp.int32, (tb, win), 1)
    band = jnp.abs(key - C_SIDE - lax.broadcasted_iota(jnp.int32, (tb, win), 0)) <= C_SIDE
    for c in range(n_classes):
        kv = jnp.concatenate([prev_ref[c, :, C_GW:], own_ref[c, :, C_GW:], next_ref[c, :, C_GW:]], axis=0)
        for u in range(tq // tb):
            q = own_ref[c, tb * u:tb * (u + 1), 0:C_GW]
            kvw = kv[tb * u + C_SIDE:tb * u + C_SIDE + win]
            kpos = i * tq + tb * u - C_SIDE + key
            valid = band & (kpos >= 0) & (kpos < sub_len)
            o = jnp.zeros((tb, C_GW), F32)
            lse = jnp.zeros((tb, C_GW), F32)
            for hh in range(C_HPG):
                head = (lane >= HEAD_DIM * hh) & (lane < HEAD_DIM * (hh + 1))
                qm = jnp.where(head, q, jnp.zeros_like(q))
                s = jnp.where(valid, _dot_nt(qm, kvw[:, 0:C_GW]), NEG_INF)
                m = jnp.max(s, axis=1, keepdims=True)
                p = jnp.exp2(s - m)
                l = jnp.sum(p, axis=1, keepdims=True)
                oh = _dot(p.astype(BF16), kvw[:, C_GW:2 * C_GW]) * (1.0 / l)
                o = jnp.where(head, oh, o)
                lse = jnp.where(head, m + jnp.log(l) * LOG2E, lse)
            o_ref[c, tb * u:tb * (u + 1), :] = o
            lse_ref[c, tb * u:tb * (u + 1), :] = lse


def _attn_c(c, batch, seq, dil):
    sub_len = seq // dil
    tq = min(C_QUERY_ROWS, sub_len)
    nc = min(dil, C_QUERY_ROWS // tq)
    nsub = tq // C_BLOCK
    last = sub_len // C_BLOCK - 1
    o, lse = pl.pallas_call(
        functools.partial(_attn_c_kernel, sub_len=sub_len),
        grid=(batch, dil // nc, sub_len // tq),
        in_specs=[pl.BlockSpec((None, nc, tq, 3 * C_GW), lambda b, j, i: (b, j, i, 0)),
                  pl.BlockSpec((None, nc, C_BLOCK, 3 * C_GW),
                               lambda b, j, i: (b, j, jnp.maximum(i * nsub - 1, 0), 0)),
                  pl.BlockSpec((None, nc, C_BLOCK, 3 * C_GW),
                               lambda b, j, i: (b, j, jnp.minimum((i + 1) * nsub, last), 0))],
        out_specs=[pl.BlockSpec((None, nc, tq, C_GW), lambda b, j, i: (b, j, i, 0))] * 2,
        out_shape=[jax.ShapeDtypeStruct((batch, dil, sub_len, C_GW), F32)] * 2,
        compiler_params=_params("parallel", "parallel", "parallel"),
        name=f"attn_c_d{dil}",
    )(c, c, c)
    return o, lse


def _layer_norm(z, g, b):
    mu = jnp.mean(z, axis=1, keepdims=True)
    zc = z - mu
    var = jnp.mean(zc * zc, axis=1, keepdims=True)
    return zc * lax.rsqrt(var + LN_EPS) * g + b


def _first_index(hit, lane):
    return jnp.min(jnp.where(hit, lane, LANES), axis=1, keepdims=True)


def _token_order(ref, scr):
    dil, sub, _ = ref.shape
    if dil == 1:
        return ref[0]
    for j in range(dil):
        scr[pl.ds(j, sub, stride=dil), :] = ref[j]
    return scr[...]


def _out_proj_kernel(x_ref, oa_ref, ob_ref, oc0_ref, oc1_ref, oc2_ref, l0_ref, l1_ref, l2_ref,
                     wout_ref, g_ref, b_ref, wrh_ref, wrl_ref, x1_ref, x1t_ref, ri_ref, rw_ref, ord_scr):
    oc = [_token_order(r, ord_scr.at[k]) for k, r in enumerate((oc0_ref, oc1_ref, oc2_ref))]
    la, lb, lc = [_token_order(r, ord_scr.at[3 + k]) for k, r in enumerate((l0_ref, l1_ref, l2_ref))]
    mx = jnp.maximum(jnp.maximum(la, lb), lc)
    ea, eb, ec = jnp.exp2(la - mx), jnp.exp2(lb - mx), jnp.exp2(lc - mx)
    inv = 1.0 / (ea + eb + ec)
    mix = jnp.concatenate(
        [oa_ref[...], ob_ref[...],
         (oc[0] * (ea * inv)).astype(BF16),
         (oc[1] * (eb * inv)).astype(BF16),
         (oc[2] * (ec * inv)).astype(BF16)], axis=1)
    x1 = _layer_norm(DN_ALPHA * x_ref[...] + _dot(mix, wout_ref[...]), g_ref[...], b_ref[...])
    x1_ref[...] = x1
    _store_token_tiles(x1t_ref, x1)

    x_hi = x1.astype(BF16)
    x_lo = (x1 - x_hi.astype(F32)).astype(BF16)
    logits = _dot(x_hi, wrh_ref[...]) + (_dot(x_lo, wrh_ref[...]) + _dot(x_hi, wrl_ref[...]))
    lane = lax.broadcasted_iota(jnp.int32, logits.shape, 1)
    ninf = -jnp.inf
    cl = jnp.where(lane < N_GROUPS, logits, ninf)
    cmax = jnp.max(cl, axis=1, keepdims=True)
    grp = _first_index(cl == cmax, lane)
    pg = 1.0 / jnp.sum(jnp.exp(cl - cmax), axis=1, keepdims=True)
    lo = N_GROUPS + EXPERTS_PER_GROUP * grp
    fl = jnp.where((lane >= lo) & (lane < lo + EXPERTS_PER_GROUP), logits, ninf)
    v1 = jnp.max(fl, axis=1, keepdims=True)
    i1 = _first_index(fl == v1, lane)
    fl2 = jnp.where(lane == i1, ninf, fl)
    v2 = jnp.max(fl2, axis=1, keepdims=True)
    i2 = _first_index(fl2 == v2, lane)
    e21 = jnp.exp(v2 - v1)
    t1 = pg / (1.0 + e21)
    t2 = t1 * e21
    ri_ref[...] = jnp.where(lane == 0, i1 - N_GROUPS, jnp.where(lane == 1, i2 - N_GROUPS, 0))
    rw_ref[...] = jnp.where(lane == 0, t1, jnp.where(lane == 1, t2, 0.0))


def _out_proj(x, oa, ob, ocs, lses, lw, seq):
    n = x.shape[0]
    tm = ROW_BLOCK
    nrep = seq // tm
    row = lambda w: pl.BlockSpec((tm, w), lambda i: (i, 0))
    full = lambda a: pl.BlockSpec(a.shape, lambda i: (0,) * a.ndim)
    dilated = [_dilated_spec(C_GW, dil, nrep) for _, dil in C_GROUPS]
    weights = (lw['w_out'], lw['ln1_g'], lw['ln1_b'], lw['w_router_hi'], lw['w_router_lo'])
    return pl.pallas_call(
        _out_proj_kernel,
        grid=(n // tm,),
        in_specs=[row(D_MODEL), row(A_W), row(B_W)] + dilated + dilated + [full(w) for w in weights],
        out_specs=[row(D_MODEL), pl.BlockSpec((tm * SUBLANES, LANES), lambda i: (i, 0)), row(LANES), row(LANES)],
        out_shape=[jax.ShapeDtypeStruct((n, D_MODEL), F32),
                   jax.ShapeDtypeStruct((n * SUBLANES, LANES), F32),
                   jax.ShapeDtypeStruct((n, LANES), jnp.int32),
                   jax.ShapeDtypeStruct((n, LANES), F32)],
        scratch_shapes=[pltpu.VMEM((2 * len(C_GROUPS), tm, C_GW), F32)],
        compiler_params=_params("parallel"),
        name="out_proj",
    )(x, oa, ob, *ocs, *lses, *weights)


def _store_token_tiles(ref, x):
    rows = x.shape[0]
    for j in range(SUBLANES):
        ref[pl.ds(j, rows, stride=SUBLANES), :] = x[:, LANES * j:LANES * (j + 1)]


def _load_token_tiles(ref, first, rows):
    return [ref[pl.ds(first * SUBLANES + j, rows, stride=SUBLANES), :] for j in range(SUBLANES)]


def _start_row_gathers(idx_ref, src_hbm, dst, sem):
    for r in range(dst.shape[0] // SUBLANES):
        first = pl.multiple_of(idx_ref[0, 0, r], SUBLANES)
        pltpu.make_async_copy(src_hbm.at[pl.ds(first, SUBLANES)],
                              dst.at[pl.ds(r * SUBLANES, SUBLANES)], sem).start()


def _prefetched_rows(idx_cur_ref, idx_next_ref, src_hbm, buf, sem_g):
    i = pl.program_id(0)
    slot = lax.rem(i, 2)
    tokens = buf.shape[1] // SUBLANES

    @pl.when(i == 0)
    def _():
        _start_row_gathers(idx_cur_ref, src_hbm, buf.at[0], sem_g.at[0])

    @pl.when(i + 1 < pl.num_programs(0))
    def _():
        _start_row_gathers(idx_next_ref, src_hbm, buf.at[1 - slot], sem_g.at[1 - slot])

    def drain(r, carry):
        pltpu.make_async_copy(src_hbm.at[pl.ds(0, SUBLANES)], buf.at[slot, pl.ds(0, SUBLANES)],
                              sem_g.at[slot]).wait()
        return carry

    lax.fori_loop(0, tokens, drain, 0, unroll=8)
    return slot


def _index_specs(rows, steps, prefetch_args=0):
    if prefetch_args:
        return [pl.BlockSpec((1, 1, rows), lambda i, be: (i, 0, 0)),
                pl.BlockSpec((1, 1, rows), lambda i, be: (jnp.minimum(i + 1, steps - 1), 0, 0))]
    return [pl.BlockSpec((1, 1, rows), lambda i: (i, 0, 0)),
            pl.BlockSpec((1, 1, rows), lambda i: (jnp.minimum(i + 1, steps - 1), 0, 0))]


def _gather_scratch(rows):
    return [pltpu.VMEM((2, rows * SUBLANES, LANES), F32), pltpu.SemaphoreType.DMA((2,))]


def _expert_kernel(be_ref, tok_ref, tok_next_ref, x_hbm, w1_ref, w3_ref, w2_ref, y_ref, xg, sem_g):
    del be_ref
    slot = _prefetched_rows(tok_ref, tok_next_ref, x_hbm, xg, sem_g)
    tb = y_ref.shape[0] // SUBLANES
    xb = jnp.concatenate([c.astype(BF16) for c in _load_token_tiles(xg.at[slot], 0, tb)], axis=1)
    h1 = _dot(xb, w1_ref[...])
    h3 = _dot(xb, w3_ref[...])
    hid = (h1 * (1.0 / (1.0 + jnp.exp(-h1))) * h3).astype(BF16)
    _store_token_tiles(y_ref, _dot(hid, w2_ref[...]))


def _experts(blk_exp, slot_tok, x1t, lw):
    n_blocks = blk_exp.shape[0]
    tb = MOE_ROWS
    tok = (slot_tok * SUBLANES).reshape(n_blocks, 1, tb)
    grid_spec = pltpu.PrefetchScalarGridSpec(
        num_scalar_prefetch=1,
        grid=(n_blocks,),
        in_specs=_index_specs(tb, n_blocks, prefetch_args=1) + [
            pl.BlockSpec(memory_space=pl.ANY),
            pl.BlockSpec((None, D_MODEL, D_EXPERT), lambda i, be: (be[i], 0, 0)),
            pl.BlockSpec((None, D_MODEL, D_EXPERT), lambda i, be: (be[i], 0, 0)),
            pl.BlockSpec((None, D_EXPERT, D_MODEL), lambda i, be: (be[i], 0, 0))],
        out_specs=pl.BlockSpec((tb * SUBLANES, LANES), lambda i, be: (i, 0)),
        scratch_shapes=_gather_scratch(tb))
    return pl.pallas_call(
        _expert_kernel,
        grid_spec=grid_spec,
        out_shape=jax.ShapeDtypeStruct((n_blocks * tb * SUBLANES, LANES), F32),
        compiler_params=_params("arbitrary"),
        name="experts",
    )(blk_exp, tok, tok, x1t, lw['w1'], lw['w3'], lw['w2'])


def _combine_kernel(pos_ref, pos_next_ref, x1_ref, rw_ref, ys_hbm, g_ref, b_ref, out_ref, yg, sem_g):
    tm = x1_ref.shape[0]
    slot = _prefetched_rows(pos_ref, pos_next_ref, ys_hbm, yg, sem_g)
    rw = rw_ref[...]
    w0, w1 = rw[:, 0:1], rw[:, 1:2]
    y0 = _load_token_tiles(yg.at[slot], 0, tm)
    y1 = _load_token_tiles(yg.at[slot], tm, tm)
    y = jnp.concatenate([w0 * a + w1 * b for a, b in zip(y0, y1)], axis=1)
    out_ref[...] = _layer_norm(DN_ALPHA * x1_ref[...] + y, g_ref[...], b_ref[...])


def _combine(pos, x1, rw, ys, lw):
    n = x1.shape[0]
    tm = ROW_BLOCK
    nb = n // tm
    rows = TOP_K_INNER * tm
    pos_blk = (pos * SUBLANES).reshape(nb, tm, TOP_K_INNER).transpose(0, 2, 1).reshape(nb, 1, rows)
    full = lambda a: pl.BlockSpec(a.shape, lambda i: (0,) * a.ndim)
    return pl.pallas_call(
        _combine_kernel,
        grid=(nb,),
        in_specs=_index_specs(rows, nb) + [
            pl.BlockSpec((tm, D_MODEL), lambda i: (i, 0)),
            pl.BlockSpec((tm, LANES), lambda i: (i, 0)),
            pl.BlockSpec(memory_space=pl.ANY),
            full(lw['ln2_g']), full(lw['ln2_b'])],
        out_specs=pl.BlockSpec((tm, D_MODEL), lambda i: (i, 0)),
        out_shape=jax.ShapeDtypeStruct((n, D_MODEL), F32),
        scratch_shapes=_gather_scratch(rows),
        compiler_params=_params("arbitrary"),
        name="combine",
    )(pos_blk, pos_blk, x1, rw, ys, lw['ln2_g'], lw['ln2_b'])


def _dispatch_plan(eid):
    n = eid.shape[0]
    a = n * TOP_K_INNER
    tb = MOE_ROWS
    e_flat = eid.reshape(a)
    order = jnp.argsort(e_flat).astype(jnp.int32)
    experts = jnp.arange(N_EXPERTS, dtype=jnp.int32)
    counts = jnp.sum((e_flat[:, None] == experts[None, :]).astype(jnp.int32), axis=0)
    padded = (counts + tb - 1) // tb * tb
    pad_end = jnp.cumsum(padded)
    pad_start = pad_end - padded
    start = jnp.cumsum(counts) - counts
    shift = pad_start - start
    n_blocks = -(-(a + N_EXPERTS * (tb - 1)) // tb)
    blk_start = jnp.arange(n_blocks, dtype=jnp.int32) * tb
    blk_exp = jnp.minimum(jnp.sum((blk_start[:, None] >= pad_end[None, :]).astype(jnp.int32), axis=1),
                          N_EXPERTS - 1)
    e_slot = jnp.repeat(blk_exp, tb)
    sorted_pos = jnp.arange(n_blocks * tb, dtype=jnp.int32) - shift[e_slot]
    valid = sorted_pos < (start + counts)[e_slot]
    slot_tok = jnp.where(valid, order[jnp.clip(sorted_pos, 0, a - 1)] // TOP_K_INNER, 0)
    dest = shift[e_flat[order]] + jnp.arange(a, dtype=jnp.int32)
    pos = dest[jnp.argsort(order)].reshape(n, TOP_K_INNER)
    return blk_exp, slot_tok, pos


def _rope_tables(seq, dim):
    inv_freq = 1.0 / (ROPE_THETA ** (jnp.arange(0, dim, 2, dtype=F32) / dim))
    ang = jnp.arange(seq, dtype=F32)[:, None] * inv_freq[None, :]
    return jnp.cos(ang), jnp.sin(ang)


def _rope_table_block(seq):
    c16, s16 = _rope_tables(seq, A_HALF)
    c32, s32 = _rope_tables(seq, HEAD_DIM)
    ones = lambda w: jnp.ones((seq, w), F32)
    zeros = lambda w: jnp.zeros((seq, w), F32)
    cos_a = jnp.tile(jnp.concatenate([c16, c16], 1), (1, 2 * A_HEADS))
    sin_a = jnp.tile(jnp.concatenate([-s16, s16], 1), (1, 2 * A_HEADS))
    cos_b = jnp.concatenate([ones(B_NOPE), c16, c16, ones(B_SLOT - B_NOPE - B_ROPE)], 1)
    sin_b = jnp.concatenate([zeros(B_NOPE), -s16, s16, zeros(B_SLOT - B_NOPE - B_ROPE)], 1)
    cos_c = jnp.tile(jnp.concatenate([c32, c32], 1), (1, C_HPG))
    sin_c = jnp.tile(jnp.concatenate([-s32, s32], 1), (1, C_HPG))
    return jnp.concatenate([cos_a, sin_a, cos_b, sin_b, cos_c, sin_c], 1)


def _layer_weights(l, w_in, diff_lambda, diff_subln, mla_q_norm, mla_w_uq, mla_kv_norm, mla_w_ukv, w_out,
                   ln1_g, ln1_b, moe_w_coarse, moe_w_fine, w1, w3, w2, ln2_g, ln2_b):
    wi = w_in[l]
    zc = lambda rows, w: jnp.zeros((rows, w), F32)
    b0 = COL_A
    w_b = jnp.concatenate([wi[:, b0:b0 + B_Q_RANK + B_KV_RANK], zc(D_MODEL, B_NOPE),
                           wi[:, b0 + B_Q_RANK + B_KV_RANK:b0 + COL_B], zc(D_MODEL, B_SLOT - B_NOPE - B_ROPE)], 1)
    c0 = COL_A + COL_B
    cw = C_HPG * HEAD_DIM * len(C_GROUPS)
    w_c = jnp.concatenate([wi[:, c0 + part * cw + g * C_GW:c0 + part * cw + (g + 1) * C_GW]
                           for g in range(len(C_GROUPS)) for part in range(3)], 1)
    qd = B_NOPE + B_ROPE
    w_uq = jnp.concatenate([jnp.concatenate([mla_w_uq[l][:, h * qd:(h + 1) * qd], zc(B_Q_RANK, B_SLOT - qd)], 1)
                            for h in range(B_HEADS)], 1)
    kvd = B_NOPE + B_V
    w_uk = jnp.concatenate([jnp.concatenate([mla_w_ukv[l][:, h * kvd:h * kvd + B_NOPE],
                                             zc(B_KV_RANK, B_SLOT - B_NOPE)], 1) for h in range(B_HEADS)], 1)
    w_uv = jnp.concatenate([mla_w_ukv[l][:, h * kvd + B_NOPE:(h + 1) * kvd] for h in range(B_HEADS)], 1)
    w_router = jnp.concatenate(
        [moe_w_coarse[l]] + [moe_w_fine[l][g] for g in range(N_GROUPS)]
        + [zc(D_MODEL, LANES - N_GROUPS - N_EXPERTS)], 1)
    lam_init = 0.8 - 0.6 * math.exp(-0.3 * l)
    lv = diff_lambda[l].astype(F32)
    lam = jnp.exp(jnp.sum(lv[0] * lv[1])) - jnp.exp(jnp.sum(lv[2] * lv[3])) + lam_init
    return dict(
        w_a=wi[:, 0:COL_A].astype(BF16), w_b=w_b.astype(BF16), w_c=w_c.astype(BF16),
        w_uq=w_uq.astype(BF16), w_uk=w_uk.astype(BF16), w_uv=w_uv.astype(BF16),
        q_norm=mla_q_norm[l].reshape(1, B_Q_RANK), kv_norm=mla_kv_norm[l].reshape(1, B_KV_RANK),
        diff_sc=jnp.stack([lam, jnp.asarray(1.0 - lam_init, F32)]).astype(F32),
        subln=jnp.tile(diff_subln[l], A_HEADS).reshape(1, A_W),
        w_out=w_out[l].astype(BF16), ln1_g=ln1_g[l].reshape(1, D_MODEL), ln1_b=ln1_b[l].reshape(1, D_MODEL),
        w_router_hi=w_router.astype(BF16),
        w_router_lo=(w_router - w_router.astype(BF16).astype(F32)).astype(BF16),
        w1=w1[l], w3=w3[l], w2=w2[l],
        ln2_g=ln2_g[l].reshape(1, D_MODEL), ln2_b=ln2_b[l].reshape(1, D_MODEL))


def _encoder_layer(x, tab, lw, batch, seq):
    qa, ka, va, qb, kb, vb, c0, c1, c2 = _in_proj(x, tab, lw, batch, seq)
    oa = _attn_a(lw['diff_sc'], qa, ka, va, lw['subln'], batch, seq)
    ob = _attn_b(qb, kb, vb, batch, seq)
    ocs, lses = zip(*[_attn_c(c, batch, seq, dil) for c, (_, dil) in zip((c0, c1, c2), C_GROUPS)])
    x1, x1t, ri, rw = _out_proj(x, oa, ob, ocs, lses, lw, seq)
    blk_exp, slot_tok, pos = _dispatch_plan(ri[:, 0:TOP_K_INNER])
    ys = _experts(blk_exp, slot_tok, x1t, lw)
    return _combine(pos, x1, rw, ys, lw)


def kernel(x_prompt, x_sample, w_in, diff_lambda, diff_subln, mla_q_norm, mla_w_uq, mla_kv_norm, mla_w_ukv,
           w_out, ln1_g, ln1_b, moe_w_coarse, moe_w_fine, moe_w1, moe_w3, moe_w2, ln2_g, ln2_b):
    w1, w3, w2 = moe_w1.astype(BF16), moe_w3.astype(BF16), moe_w2.astype(BF16)
    layers = [_layer_weights(l, w_in, diff_lambda, diff_subln, mla_q_norm, mla_w_uq, mla_kv_norm, mla_w_ukv,
                             w_out, ln1_g, ln1_b, moe_w_coarse, moe_w_fine, w1, w3, w2, ln2_g, ln2_b)
              for l in range(DEPTH)]

    def trunk(x):
        batch, seq, _ = x.shape
        tab = _rope_table_block(seq)
        h = x.reshape(batch * seq, D_MODEL)
        for lw in layers:
            h = _encoder_layer(h, tab, lw, batch, seq)
        return h.reshape(batch, seq, D_MODEL)

    return (trunk(x_prompt), trunk(x_sample))
```

```python
import functools
import math

import jax
import jax.numpy as jnp
from jax import lax
from jax.experimental import pallas as pl
from jax.experimental.pallas import tpu as pltpu

D_MODEL = 1024
DEPTH = 4
HEAD_DIM = 64
ROPE_THETA = 10000.0
LN_EPS = 1e-5
RMS_EPS = 1e-6
NEG_INF = -1e30

A_HEADS = 4
A_HALF = HEAD_DIM // 2
A_W = A_HEADS * HEAD_DIM

B_HEADS = 6
B_Q_RANK = 256
B_KV_RANK = 128
B_NOPE = 64
B_ROPE = 32
B_V = 64
B_W = B_HEADS * B_V
B_SLOT = 128
B_VT_ROWS = B_V + 16

C_GROUPS = ((128, 1), (512, 4), (2048, 16))
C_HPG = 2
C_GW = C_HPG * HEAD_DIM
C_SIDE = 64
C_BLOCK = 128
C_QUERY_ROWS = 512

COL_A = 3 * A_W
COL_B = B_Q_RANK + B_KV_RANK + B_ROPE

N_GROUPS = 4
EXPERTS_PER_GROUP = 8
N_EXPERTS = N_GROUPS * EXPERTS_PER_GROUP
TOP_K_INNER = 2
D_EXPERT = 512

DN_ALPHA = (2 * DEPTH) ** 0.25

LOG2E = 1.4426950408889634
SCALE_A = (A_HALF ** -0.5) * LOG2E
SCALE_B = ((B_NOPE + B_ROPE) ** -0.5) * LOG2E
SCALE_C = (HEAD_DIM ** -0.5) * LOG2E

LANES = 128
SUBLANES = 8
ROW_BLOCK = 256
Q_BLOCK = 256
K_CHUNK = 256
MOE_ROWS = 256
VMEM_LIMIT = 48 * 1024 * 1024
ATTN_VMEM_LIMIT = 56 * 1024 * 1024

BF16 = jnp.bfloat16
F32 = jnp.float32

_NT = (((1,), (1,)), ((), ()))


def _dot(a, b):
    return jnp.dot(a, b, preferred_element_type=F32)


def _dot_nt(a, b):
    return lax.dot_general(a, b, _NT, preferred_element_type=F32)


def _params(*sem, vmem=VMEM_LIMIT):
    return pltpu.CompilerParams(dimension_semantics=sem, vmem_limit_bytes=vmem)


def _rope(h, cos, sin_signed, half):
    width = h.shape[1]
    lane = lax.broadcasted_iota(jnp.int32, h.shape, 1)
    first = (lane % (2 * half)) < half
    partner = jnp.where(first, pltpu.roll(h, width - half, 1), pltpu.roll(h, half, 1))
    return h * cos + partner * sin_signed


def _rms(x, g):
    return x * lax.rsqrt(jnp.mean(x * x, axis=1, keepdims=True) + RMS_EPS) * g


def _in_proj_kernel(x_ref, tab_ref, wa_ref, wb_ref, wc_ref, wuq_ref, wuk_ref, wuv_ref, qn_ref, kvn_ref,
                    qa_ref, ka_ref, va_ref, qb_ref, kb_ref, vb_ref, c0_ref, c1_ref, c2_ref, cs_scr):
    xb = x_ref[...].astype(BF16)
    cos_a, sin_a = tab_ref[:, 0:256], tab_ref[:, 256:512]
    cos_b, sin_b = tab_ref[:, 512:640], tab_ref[:, 640:768]
    cos_c, sin_c = tab_ref[:, 768:896], tab_ref[:, 896:1024]

    ha = _dot(xb, wa_ref[...])
    qa_ref[...] = (_rope(ha[:, 0:A_W], cos_a, sin_a, A_HALF // 2) * SCALE_A).astype(BF16)
    ka_ref[...] = _rope(ha[:, A_W:2 * A_W], cos_a, sin_a, A_HALF // 2).astype(BF16)
    va_ref[...] = ha[:, 2 * A_W:3 * A_W].astype(BF16)

    hb = _dot(xb, wb_ref[...])
    cq = _rms(hb[:, 0:B_Q_RANK], qn_ref[...]).astype(BF16)
    ckv = _rms(hb[:, B_Q_RANK:B_Q_RANK + B_KV_RANK], kvn_ref[...]).astype(BF16)
    qb = _dot(cq, wuq_ref[...])
    kb = _dot(ckv, wuk_ref[...])
    vb_ref[...] = _dot(ckv, wuv_ref[...]).astype(BF16)
    k_rope = _rope(hb[:, 384:512], cos_b, sin_b, B_ROPE // 2)
    for h in range(B_HEADS):
        sl = slice(B_SLOT * h, B_SLOT * (h + 1))
        qb_ref[:, sl] = (_rope(qb[:, sl], cos_b, sin_b, B_ROPE // 2) * SCALE_B).astype(BF16)
        kb_ref[:, sl] = (kb[:, sl] + k_rope).astype(BF16)

    hc = _dot(xb, wc_ref[...])
    tm = hc.shape[0]
    for g, c_ref in enumerate((c0_ref, c1_ref, c2_ref)):
        base = 3 * C_GW * g
        dil = C_GROUPS[g][1]
        qkv = [_rope(hc[:, base:base + C_GW], cos_c, sin_c, HEAD_DIM // 2) * SCALE_C,
               _rope(hc[:, base + C_GW:base + 2 * C_GW], cos_c, sin_c, HEAD_DIM // 2),
               hc[:, base + 2 * C_GW:base + 3 * C_GW]]
        for part in range(3):
            cols = slice(C_GW * part, C_GW * (part + 1))
            if dil == 1:
                c_ref[0, :, cols] = qkv[part].astype(BF16)
            else:
                cs_scr[part] = qkv[part]
                for j in range(dil):
                    c_ref[j, :, cols] = cs_scr[part, pl.ds(j, tm // dil, stride=dil), :].astype(BF16)


def _dilated_spec(width, dil, nrep):
    return pl.BlockSpec((None, dil, ROW_BLOCK // dil, width), lambda i: (i // nrep, 0, i % nrep, 0))


def _in_proj(x, tab, lw, batch, seq):
    n = x.shape[0]
    tm = ROW_BLOCK
    nrep = seq // tm
    row = lambda w: pl.BlockSpec((tm, w), lambda i: (i, 0))
    full = lambda a: pl.BlockSpec(a.shape, lambda i: (0,) * a.ndim)
    weights = (lw['w_a'], lw['w_b'], lw['w_c'], lw['w_uq'], lw['w_uk'], lw['w_uv'], lw['q_norm'], lw['kv_norm'])
    out_w = (A_W, A_W, A_W, B_HEADS * B_SLOT, B_HEADS * B_SLOT, B_W)
    return pl.pallas_call(
        _in_proj_kernel,
        grid=(n // tm,),
        in_specs=[row(D_MODEL), pl.BlockSpec((tm, 1024), lambda i: (i % nrep, 0))] + [full(w) for w in weights],
        out_specs=[row(w) for w in out_w] + [_dilated_spec(3 * C_GW, dil, nrep) for _, dil in C_GROUPS],
        out_shape=[jax.ShapeDtypeStruct((n, w), BF16) for w in out_w]
        + [jax.ShapeDtypeStruct((batch, dil, seq // dil, 3 * C_GW), BF16) for _, dil in C_GROUPS],
        scratch_shapes=[pltpu.VMEM((3, tm, C_GW), F32)],
        compiler_params=_params("parallel"),
        name="in_proj",
    )(x, tab, *weights)


def _fold_rows(x, op):
    r, c = x.shape
    return op(x.reshape(r // SUBLANES, SUBLANES, c), axis=0)


def _chunk_start(kc):
    return pl.multiple_of(kc * K_CHUNK, K_CHUNK)


def _stage_v_transposed(v_ref, vt_scr):
    @pl.when(pl.program_id(1) == 0)
    def _():
        def body(kc, carry):
            vt_scr[kc] = v_ref[pl.ds(_chunk_start(kc), K_CHUNK), :].astype(F32).T.astype(BF16)
            return carry
        lax.fori_loop(0, vt_scr.shape[0], body, 0)


def _attn_a_kernel(sc_ref, q_ref, k_ref, v_ref, g_ref, o_ref, vt_scr, s_scr):
    lam = sc_ref[0]
    post = sc_ref[1]
    tq = q_ref.shape[0]
    nkc = k_ref.shape[0] // K_CHUNK
    _stage_v_transposed(v_ref, vt_scr)
    qt = q_ref[...].astype(F32).T
    row = lax.broadcasted_iota(jnp.int32, (A_W, tq), 0)
    neg = jnp.full((SUBLANES, tq), -jnp.inf, F32)
    zero = jnp.zeros((SUBLANES, tq), F32)
    maxes, gammas, inv_l1, heads = {}, {}, {}, []
    for st in range(A_HEADS + 2):
        hs, hp, hw = st, st - 1, st - 2
        do_s, do_p, do_w = hs < A_HEADS, 0 <= hp < A_HEADS, 0 <= hw < A_HEADS
        qts = []
        if do_s:
            for c in range(2):
                lo = A_HALF * (2 * hs + c)
                qts.append(jnp.where((row >= lo) & (row < lo + A_HALF), qt, 0.0).astype(BF16))

        def body(kc, carry, qts=qts, hs=hs, hp=hp, hw=hw, do_s=do_s, do_p=do_p, do_w=do_w):
            m1, m2, l1, l2, acc = carry
            if do_s:
                kr = k_ref[pl.ds(_chunk_start(kc), K_CHUNK), :]
                s1 = _dot(kr, qts[0])
                s2 = _dot(kr, qts[1])
                s_scr[hs % 3, 0, kc] = s1
                s_scr[hs % 3, 1, kc] = s2
                m1 = jnp.maximum(m1, _fold_rows(s1, jnp.max))
                m2 = jnp.maximum(m2, _fold_rows(s2, jnp.max))
            if do_p:
                p1 = jnp.exp2(s_scr[hp % 3, 0, kc] - maxes[hp][0])
                p2 = jnp.exp2(s_scr[hp % 3, 1, kc] - maxes[hp][1])
                s_scr[hp % 3, 0, kc] = p1
                s_scr[hp % 3, 1, kc] = p2
                l1 = l1 + _fold_rows(p1, jnp.sum)
                l2 = l2 + _fold_rows(p2, jnp.sum)
            if do_w:
                w = (s_scr[hw % 3, 0, kc] - gammas[hw] * s_scr[hw % 3, 1, kc]).astype(BF16)
                acc = acc + _dot(vt_scr[kc, HEAD_DIM * hw:HEAD_DIM * (hw + 1), :], w)
            return m1, m2, l1, l2, acc

        m1, m2, l1, l2, acc = lax.fori_loop(
            0, nkc, body, (neg, neg, zero, zero, jnp.zeros((HEAD_DIM, tq), F32)), unroll=True)
        if do_s:
            maxes[hs] = (jnp.max(m1, axis=0, keepdims=True), jnp.max(m2, axis=0, keepdims=True))
        if do_p:
            l1 = jnp.sum(l1, axis=0, keepdims=True)
            l2 = jnp.sum(l2, axis=0, keepdims=True)
            gammas[hp] = lam * l1 / l2
            inv_l1[hp] = 1.0 / l1
        if do_w:
            oh = acc * inv_l1[hw]
            ms = jnp.mean(oh * oh, axis=0, keepdims=True)
            heads.append(oh * lax.rsqrt(ms + RMS_EPS))
    out_t = jnp.concatenate(heads, axis=0)
    o_ref[...] = (out_t.T * g_ref[...] * post).astype(BF16)


def _attn_a(sc, q, k, v, g, batch, seq):
    n = q.shape[0]
    tq = Q_BLOCK
    nq = seq // tq
    nkc = seq // K_CHUNK
    return pl.pallas_call(
        _attn_a_kernel,
        grid=(batch, nq),
        in_specs=[pl.BlockSpec(memory_space=pltpu.SMEM),
                  pl.BlockSpec((tq, A_W), lambda b, i: (b * nq + i, 0)),
                  pl.BlockSpec((seq, A_W), lambda b, i: (b, 0)),
                  pl.BlockSpec((seq, A_W), lambda b, i: (b, 0)),
                  pl.BlockSpec((1, A_W), lambda b, i: (0, 0))],
        out_specs=pl.BlockSpec((tq, A_W), lambda b, i: (b * nq + i, 0)),
        out_shape=jax.ShapeDtypeStruct((n, A_W), BF16),
        scratch_shapes=[pltpu.VMEM((nkc, A_W, K_CHUNK), BF16),
                        pltpu.VMEM((3, 2, nkc, K_CHUNK, tq), F32)],
        compiler_params=_params("arbitrary", "arbitrary", vmem=ATTN_VMEM_LIMIT),
        name="attn_a",
    )(sc, q, k, v, g)


def _attn_b_kernel(q_ref, k_ref, v_ref, o_ref, vt_scr, s_scr):
    tq = q_ref.shape[0]
    nkc = k_ref.shape[0] // K_CHUNK

    @pl.when(pl.program_id(1) == 0)
    def _():
        ones = jnp.ones((B_VT_ROWS - B_V, K_CHUNK), BF16)

        def stage(kc, carry):
            vt = v_ref[pl.ds(_chunk_start(kc), K_CHUNK), :].astype(F32).T.astype(BF16)
            for h in range(B_HEADS):
                vt_scr[kc, B_VT_ROWS * h:B_VT_ROWS * h + B_V, :] = vt[B_V * h:B_V * (h + 1), :]
                vt_scr[kc, B_VT_ROWS * h + B_V:B_VT_ROWS * (h + 1), :] = ones
            return carry
        lax.fori_loop(0, nkc, stage, 0)

    neg = jnp.full((SUBLANES, tq), -jnp.inf, F32)
    acc0 = jnp.zeros((B_VT_ROWS, tq), F32)
    n_pairs = B_HEADS // 2
    maxes, heads = {}, []
    for st in range(n_pairs + 1):
        ps, pw = st, st - 1
        do_s, do_w = ps < n_pairs, pw >= 0
        qts = []
        if do_s:
            for j in range(2):
                sl = slice(B_SLOT * (2 * ps + j), B_SLOT * (2 * ps + j + 1))
                qts.append(q_ref[:, sl].astype(F32).T.astype(BF16))

        def body(kc, carry, qts=qts, ps=ps, pw=pw, do_s=do_s, do_w=do_w):
            ms, accs = list(carry[0]), list(carry[1])
            for j in range(2):
                if do_s:
                    h = 2 * ps + j
                    s = _dot(k_ref[pl.ds(_chunk_start(kc), K_CHUNK), B_SLOT * h:B_SLOT * (h + 1)], qts[j])
                    s_scr[ps % 2, j, kc] = s
                    ms[j] = jnp.maximum(ms[j], _fold_rows(s, jnp.max))
                if do_w:
                    h = 2 * pw + j
                    p = jnp.exp2(s_scr[pw % 2, j, kc] - maxes[h]).astype(BF16)
                    accs[j] = accs[j] + _dot(vt_scr[kc, B_VT_ROWS * h:B_VT_ROWS * (h + 1), :], p)
            return tuple(ms), tuple(accs)

        ms, accs = lax.fori_loop(0, nkc, body, ((neg, neg), (acc0, acc0)), unroll=8)
        for j in range(2):
            if do_s:
                maxes[2 * ps + j] = jnp.max(ms[j], axis=0, keepdims=True)
            if do_w:
                heads.append(accs[j][0:B_V] * (1.0 / accs[j][B_V:B_V + 1]))
    o_ref[...] = jnp.concatenate(heads, axis=0).T.astype(BF16)


def _attn_b(q, k, v, batch, seq):
    n = q.shape[0]
    tq = Q_BLOCK
    nq = seq // tq
    nkc = seq // K_CHUNK
    wq = B_HEADS * B_SLOT
    return pl.pallas_call(
        _attn_b_kernel,
        grid=(batch, nq),
        in_specs=[pl.BlockSpec((tq, wq), lambda b, i: (b * nq + i, 0)),
                  pl.BlockSpec((seq, wq), lambda b, i: (b, 0)),
                  pl.BlockSpec((seq, B_W), lambda b, i: (b, 0))],
        out_specs=pl.BlockSpec((tq, B_W), lambda b, i: (b * nq + i, 0)),
        out_shape=jax.ShapeDtypeStruct((n, B_W), BF16),
        scratch_shapes=[pltpu.VMEM((nkc, B_HEADS * B_VT_ROWS, K_CHUNK), BF16),
                        pltpu.VMEM((2, 2, nkc, K_CHUNK, tq), F32)],
        compiler_params=_params("arbitrary", "arbitrary", vmem=ATTN_VMEM_LIMIT),
        name="attn_b",
    )(q, k, v)


def _attn_c_kernel(own_ref, prev_ref, next_ref, o_ref, lse_ref, *, sub_len):
    i = pl.program_id(2)
    n_classes, tq, _ = own_ref.shape
    tb = C_BLOCK
    win = 2 * tb
    lane = lax.broadcasted_iota(jnp.int32, (tb, C_GW), 1)
    key = lax.broadcasted_iota(jnp.int32, (tb, win), 1)
    band = jnp.abs(key - C_SIDE - lax.broadcasted_iota(jnp.int32, (tb, win), 0)) <= C_SIDE
    for c in range(n_classes):
        kv = jnp.concatenate([prev_ref[c, :, C_GW:], own_ref[c, :, C_GW:], next_ref[c, :, C_GW:]], axis=0)
        for u in range(tq // tb):
            q = own_ref[c, tb * u:tb * (u + 1), 0:C_GW]
            kvw = kv[tb * u + C_SIDE:tb * u + C_SIDE + win]
            kpos = i * tq + tb * u - C_SIDE + key
            valid = band & (kpos >= 0) & (kpos < sub_len)
            o = jnp.zeros((tb, C_GW), F32)
            lse = jnp.zeros((tb, C_GW), F32)
            for hh in range(C_HPG):
                head = (lane >= HEAD_DIM * hh) & (lane < HEAD_DIM * (hh + 1))
                qm = jnp.where(head, q, jnp.zeros_like(q))
                s = jnp.where(valid, _dot_nt(qm, kvw[:, 0:C_GW]), NEG_INF)
                m = jnp.max(s, axis=1, keepdims=True)
                p = jnp.exp2(s - m)
                l = jnp.sum(p, axis=1, keepdims=True)
                oh = _dot(p.astype(BF16), kvw[:, C_GW:2 * C_GW]) * (1.0 / l)
                o = jnp.where(head, oh, o)
                lse = jnp.where(head, m + jnp.log(l) * LOG2E, lse)
            o_ref[c, tb * u:tb * (u + 1), :] = o
            lse_ref[c, tb * u:tb * (u + 1), :] = lse


def _attn_c(c, batch, seq, dil):
    sub_len = seq // dil
    tq = min(C_QUERY_ROWS, sub_len)
    nc = min(dil, C_QUERY_ROWS // tq)
    nsub = tq // C_BLOCK
    last = sub_len // C_BLOCK - 1
    o, lse = pl.pallas_call(
        functools.partial(_attn_c_kernel, sub_len=sub_len),
        grid=(batch, dil // nc, sub_len // tq),
        in_specs=[pl.BlockSpec((None, nc, tq, 3 * C_GW), lambda b, j, i: (b, j, i, 0)),
                  pl.BlockSpec((None, nc, C_BLOCK, 3 * C_GW),
                               lambda b, j, i: (b, j, jnp.maximum(i * nsub - 1, 0), 0)),
                  pl.BlockSpec((None, nc, C_BLOCK, 3 * C_GW),
                               lambda b, j, i: (b, j, jnp.minimum((i + 1) * nsub, last), 0))],
        out_specs=[pl.BlockSpec((None, nc, tq, C_GW), lambda b, j, i: (b, j, i, 0))] * 2,
        out_shape=[jax.ShapeDtypeStruct((batch, dil, sub_len, C_GW), F32)] * 2,
        compiler_params=_params("parallel", "parallel", "parallel"),
        name=f"attn_c_d{dil}",
    )(c, c, c)
    return o, lse


def _layer_norm(z, g, b):
    mu = jnp.mean(z, axis=1, keepdims=True)
    zc = z - mu
    var = jnp.mean(zc * zc, axis=1, keepdims=True)
    return zc * lax.rsqrt(var + LN_EPS) * g + b


def _first_index(hit, lane):
    return jnp.min(jnp.where(hit, lane, LANES), axis=1, keepdims=True)


def _token_order(ref, scr):
    dil, sub, _ = ref.shape
    if dil == 1:
        return ref[0]
    for j in range(dil):
        scr[pl.ds(j, sub, stride=dil), :] = ref[j]
    return scr[...]


def _out_proj_kernel(x_ref, oa_ref, ob_ref, oc0_ref, oc1_ref, oc2_ref, l0_ref, l1_ref, l2_ref,
                     wout_ref, g_ref, b_ref, wrh_ref, wrl_ref, x1_ref, x1t_ref, ri_ref, rw_ref, ord_scr):
    oc = [_token_order(r, ord_scr.at[k]) for k, r in enumerate((oc0_ref, oc1_ref, oc2_ref))]
    la, lb, lc = [_token_order(r, ord_scr.at[3 + k]) for k, r in enumerate((l0_ref, l1_ref, l2_ref))]
    mx = jnp.maximum(jnp.maximum(la, lb), lc)
    ea, eb, ec = jnp.exp2(la - mx), jnp.exp2(lb - mx), jnp.exp2(lc - mx)
    inv = 1.0 / (ea + eb + ec)
    mix = jnp.concatenate(
        [oa_ref[...], ob_ref[...],
         (oc[0] * (ea * inv)).astype(BF16),
         (oc[1] * (eb * inv)).astype(BF16),
         (oc[2] * (ec * inv)).astype(BF16)], axis=1)
    x1 = _layer_norm(DN_ALPHA * x_ref[...] + _dot(mix, wout_ref[...]), g_ref[...], b_ref[...])
    x1_ref[...] = x1
    _store_token_tiles(x1t_ref, x1)

    x_hi = x1.astype(BF16)
    x_lo = (x1 - x_hi.astype(F32)).astype(BF16)
    logits = _dot(x_hi, wrh_ref[...]) + (_dot(x_lo, wrh_ref[...]) + _dot(x_hi, wrl_ref[...]))
    lane = lax.broadcasted_iota(jnp.int32, logits.shape, 1)
    ninf = -jnp.inf
    cl = jnp.where(lane < N_GROUPS, logits, ninf)
    cmax = jnp.max(cl, axis=1, keepdims=True)
    grp = _first_index(cl == cmax, lane)
    pg = 1.0 / jnp.sum(jnp.exp(cl - cmax), axis=1, keepdims=True)
    lo = N_GROUPS + EXPERTS_PER_GROUP * grp
    fl = jnp.where((lane >= lo) & (lane < lo + EXPERTS_PER_GROUP), logits, ninf)
    v1 = jnp.max(fl, axis=1, keepdims=True)
    i1 = _first_index(fl == v1, lane)
    fl2 = jnp.where(lane == i1, ninf, fl)
    v2 = jnp.max(fl2, axis=1, keepdims=True)
    i2 = _first_index(fl2 == v2, lane)
    e21 = jnp.exp(v2 - v1)
    t1 = pg / (1.0 + e21)
    t2 = t1 * e21
    ri_ref[...] = jnp.where(lane == 0, i1 - N_GROUPS, jnp.where(lane == 1, i2 - N_GROUPS, 0))
    rw_ref[...] = jnp.where(lane == 0, t1, jnp.where(lane == 1, t2, 0.0))


def _out_proj(x, oa, ob, ocs, lses, lw, seq):
    n = x.shape[0]
    tm = ROW_BLOCK
    nrep = seq // tm
    row = lambda w: pl.BlockSpec((tm, w), lambda i: (i, 0))
    full = lambda a: pl.BlockSpec(a.shape, lambda i: (0,) * a.ndim)
    dilated = [_dilated_spec(C_GW, dil, nrep) for _, dil in C_GROUPS]
    weights = (lw['w_out'], lw['ln1_g'], lw['ln1_b'], lw['w_router_hi'], lw['w_router_lo'])
    return pl.pallas_call(
        _out_proj_kernel,
        grid=(n // tm,),
        in_specs=[row(D_MODEL), row(A_W), row(B_W)] + dilated + dilated + [full(w) for w in weights],
        out_specs=[row(D_MODEL), pl.BlockSpec((tm * SUBLANES, LANES), lambda i: (i, 0)), row(LANES), row(LANES)],
        out_shape=[jax.ShapeDtypeStruct((n, D_MODEL), F32),
                   jax.ShapeDtypeStruct((n * SUBLANES, LANES), F32),
                   jax.ShapeDtypeStruct((n, LANES), jnp.int32),
                   jax.ShapeDtypeStruct((n, LANES), F32)],
        scratch_shapes=[pltpu.VMEM((2 * len(C_GROUPS), tm, C_GW), F32)],
        compiler_params=_params("parallel"),
        name="out_proj",
    )(x, oa, ob, *ocs, *lses, *weights)


def _store_token_tiles(ref, x):
    rows = x.shape[0]
    for j in range(SUBLANES):
        ref[pl.ds(j, rows, stride=SUBLANES), :] = x[:, LANES * j:LANES * (j + 1)]


def _load_token_tiles(ref, first, rows):
    return [ref[pl.ds(first * SUBLANES + j, rows, stride=SUBLANES), :] for j in range(SUBLANES)]


def _start_row_gathers(idx_ref, src_hbm, dst, sem):
    for r in range(dst.shape[0] // SUBLANES):
        first = pl.multiple_of(idx_ref[0, 0, r], SUBLANES)
        pltpu.make_async_copy(src_hbm.at[pl.ds(first, SUBLANES)],
                              dst.at[pl.ds(r * SUBLANES, SUBLANES)], sem).start()


def _prefetched_rows(idx_cur_ref, idx_next_ref, src_hbm, buf, sem_g):
    i = pl.program_id(0)
    slot = lax.rem(i, 2)
    tokens = buf.shape[1] // SUBLANES

    @pl.when(i == 0)
    def _():
        _start_row_gathers(idx_cur_ref, src_hbm, buf.at[0], sem_g.at[0])

    @pl.when(i + 1 < pl.num_programs(0))
    def _():
        _start_row_gathers(idx_next_ref, src_hbm, buf.at[1 - slot], sem_g.at[1 - slot])

    def drain(r, carry):
        pltpu.make_async_copy(src_hbm.at[pl.ds(0, SUBLANES)], buf.at[slot, pl.ds(0, SUBLANES)],
                              sem_g.at[slot]).wait()
        return carry

    lax.fori_loop(0, tokens, drain, 0, unroll=8)
    return slot


def _index_specs(rows, steps, prefetch_args=0):
    if prefetch_args:
        return [pl.BlockSpec((1, 1, rows), lambda i, be: (i, 0, 0)),
                pl.BlockSpec((1, 1, rows), lambda i, be: (jnp.minimum(i + 1, steps - 1), 0, 0))]
    return [pl.BlockSpec((1, 1, rows), lambda i: (i, 0, 0)),
            pl.BlockSpec((1, 1, rows), lambda i: (jnp.minimum(i + 1, steps - 1), 0, 0))]


def _gather_scratch(rows):
    return [pltpu.VMEM((2, rows * SUBLANES, LANES), F32), pltpu.SemaphoreType.DMA((2,))]


def _expert_kernel(be_ref, tok_ref, tok_next_ref, x_hbm, w1_ref, w3_ref, w2_ref, y_ref, xg, sem_g):
    del be_ref
    slot = _prefetched_rows(tok_ref, tok_next_ref, x_hbm, xg, sem_g)
    tb = y_ref.shape[0] // SUBLANES
    xb = jnp.concatenate([c.astype(BF16) for c in _load_token_tiles(xg.at[slot], 0, tb)], axis=1)
    h1 = _dot(xb, w1_ref[...])
    h3 = _dot(xb, w3_ref[...])
    hid = (h1 * (1.0 / (1.0 + jnp.exp(-h1))) * h3).astype(BF16)
    _store_token_tiles(y_ref, _dot(hid, w2_ref[...]))


def _experts(blk_exp, slot_tok, x1t, lw):
    n_blocks = blk_exp.shape[0]
    tb = MOE_ROWS
    tok = (slot_tok * SUBLANES).reshape(n_blocks, 1, tb)
    grid_spec = pltpu.PrefetchScalarGridSpec(
        num_scalar_prefetch=1,
        grid=(n_blocks,),
        in_specs=_index_specs(tb, n_blocks, prefetch_args=1) + [
            pl.BlockSpec(memory_space=pl.ANY),
            pl.BlockSpec((None, D_MODEL, D_EXPERT), lambda i, be: (be[i], 0, 0)),
            pl.BlockSpec((None, D_MODEL, D_EXPERT), lambda i, be: (be[i], 0, 0)),
            pl.BlockSpec((None, D_EXPERT, D_MODEL), lambda i, be: (be[i], 0, 0))],
        out_specs=pl.BlockSpec((tb * SUBLANES, LANES), lambda i, be: (i, 0)),
        scratch_shapes=_gather_scratch(tb))
    return pl.pallas_call(
        _expert_kernel,
        grid_spec=grid_spec,
        out_shape=jax.ShapeDtypeStruct((n_blocks * tb * SUBLANES, LANES), F32),
        compiler_params=_params("arbitrary"),
        name="experts",
    )(blk_exp, tok, tok, x1t, lw['w1'], lw['w3'], lw['w2'])


def _combine_kernel(pos_ref, pos_next_ref, x1_ref, rw_ref, ys_hbm, g_ref, b_ref, out_ref, yg, sem_g):
    tm = x1_ref.shape[0]
    slot = _prefetched_rows(pos_ref, pos_next_ref, ys_hbm, yg, sem_g)
    rw = rw_ref[...]
    w0, w1 = rw[:, 0:1], rw[:, 1:2]
    y0 = _load_token_tiles(yg.at[slot], 0, tm)
    y1 = _load_token_tiles(yg.at[slot], tm, tm)
    y = jnp.concatenate([w0 * a + w1 * b for a, b in zip(y0, y1)], axis=1)
    out_ref[...] = _layer_norm(DN_ALPHA * x1_ref[...] + y, g_ref[...], b_ref[...])


def _combine(pos, x1, rw, ys, lw):
    n = x1.shape[0]
    tm = ROW_BLOCK
    nb = n // tm
    rows = TOP_K_INNER * tm
    pos_blk = (pos * SUBLANES).reshape(nb, tm, TOP_K_INNER).transpose(0, 2, 1).reshape(nb, 1, rows)
    full = lambda a: pl.BlockSpec(a.shape, lambda i: (0,) * a.ndim)
    return pl.pallas_call(
        _combine_kernel,
        grid=(nb,),
        in_specs=_index_specs(rows, nb) + [
            pl.BlockSpec((tm, D_MODEL), lambda i: (i, 0)),
            pl.BlockSpec((tm, LANES), lambda i: (i, 0)),
            pl.BlockSpec(memory_space=pl.ANY),
            full(lw['ln2_g']), full(lw['ln2_b'])],
        out_specs=pl.BlockSpec((tm, D_MODEL), lambda i: (i, 0)),
        out_shape=jax.ShapeDtypeStruct((n, D_MODEL), F32),
        scratch_shapes=_gather_scratch(rows),
        compiler_params=_params("arbitrary"),
        name="combine",
    )(pos_blk, pos_blk, x1, rw, ys, lw['ln2_g'], lw['ln2_b'])


def _dispatch_plan(eid):
    n = eid.shape[0]
    a = n * TOP_K_INNER
    tb = MOE_ROWS
    e_flat = eid.reshape(a)
    order = jnp.argsort(e_flat).astype(jnp.int32)
    experts = jnp.arange(N_EXPERTS, dtype=jnp.int32)
    counts = jnp.sum((e_flat[:, None] == experts[None, :]).astype(jnp.int32), axis=0)
    padded = (counts + tb - 1) // tb * tb
    pad_end = jnp.cumsum(padded)
    pad_start = pad_end - padded
    start = jnp.cumsum(counts) - counts
    shift = pad_start - start
    n_blocks = -(-(a + N_EXPERTS * (tb - 1)) // tb)
    blk_start = jnp.arange(n_blocks, dtype=jnp.int32) * tb
    blk_exp = jnp.minimum(jnp.sum((blk_start[:, None] >= pad_end[None, :]).astype(jnp.int32), axis=1),
                          N_EXPERTS - 1)
    e_slot = jnp.repeat(blk_exp, tb)
    sorted_pos = jnp.arange(n_blocks * tb, dtype=jnp.int32) - shift[e_slot]
    valid = sorted_pos < (start + counts)[e_slot]
    slot_tok = jnp.where(valid, order[jnp.clip(sorted_pos, 0, a - 1)] // TOP_K_INNER, 0)
    dest = shift[e_flat[order]] + jnp.arange(a, dtype=jnp.int32)
    pos = dest[jnp.argsort(order)].reshape(n, TOP_K_INNER)
    return blk_exp, slot_tok, pos


def _rope_tables(seq, dim):
    inv_freq = 1.0 / (ROPE_THETA ** (jnp.arange(0, dim, 2, dtype=F32) / dim))
    ang = jnp.arange(seq, dtype=F32)[:, None] * inv_freq[None, :]
    return jnp.cos(ang), jnp.sin(ang)


def _rope_table_block(seq):
    c16, s16 = _rope_tables(seq, A_HALF)
    c32, s32 = _rope_tables(seq, HEAD_DIM)
    ones = lambda w: jnp.ones((seq, w), F32)
    zeros = lambda w: jnp.zeros((seq, w), F32)
    cos_a = jnp.tile(jnp.concatenate([c16, c16], 1), (1, 2 * A_HEADS))
    sin_a = jnp.tile(jnp.concatenate([-s16, s16], 1), (1, 2 * A_HEADS))
    cos_b = jnp.concatenate([ones(B_NOPE), c16, c16, ones(B_SLOT - B_NOPE - B_ROPE)], 1)
    sin_b = jnp.concatenate([zeros(B_NOPE), -s16, s16, zeros(B_SLOT - B_NOPE - B_ROPE)], 1)
    cos_c = jnp.tile(jnp.concatenate([c32, c32], 1), (1, C_HPG))
    sin_c = jnp.tile(jnp.concatenate([-s32, s32], 1), (1, C_HPG))
    return jnp.concatenate([cos_a, sin_a, cos_b, sin_b, cos_c, sin_c], 1)


def _layer_weights(l, w_in, diff_lambda, diff_subln, mla_q_norm, mla_w_uq, mla_kv_norm, mla_w_ukv, w_out,
                   ln1_g, ln1_b, moe_w_coarse, moe_w_fine, w1, w3, w2, ln2_g, ln2_b):
    wi = w_in[l]
    zc = lambda rows, w: jnp.zeros((rows, w), F32)
    b0 = COL_A
    w_b = jnp.concatenate([wi[:, b0:b0 + B_Q_RANK + B_KV_RANK], zc(D_MODEL, B_NOPE),
                           wi[:, b0 + B_Q_RANK + B_KV_RANK:b0 + COL_B], zc(D_MODEL, B_SLOT - B_NOPE - B_ROPE)], 1)
    c0 = COL_A + COL_B
    cw = C_HPG * HEAD_DIM * len(C_GROUPS)
    w_c = jnp.concatenate([wi[:, c0 + part * cw + g * C_GW:c0 + part * cw + (g + 1) * C_GW]
                           for g in range(len(C_GROUPS)) for part in range(3)], 1)
    qd = B_NOPE + B_ROPE
    w_uq = jnp.concatenate([jnp.concatenate([mla_w_uq[l][:, h * qd:(h + 1) * qd], zc(B_Q_RANK, B_SLOT - qd)], 1)
                            for h in range(B_HEADS)], 1)
    kvd = B_NOPE + B_V
    w_uk = jnp.concatenate([jnp.concatenate([mla_w_ukv[l][:, h * kvd:h * kvd + B_NOPE],
                                             zc(B_KV_RANK, B_SLOT - B_NOPE)], 1) for h in range(B_HEADS)], 1)
    w_uv = jnp.concatenate([mla_w_ukv[l][:, h * kvd + B_NOPE:(h + 1) * kvd] for h in range(B_HEADS)], 1)
    w_router = jnp.concatenate(
        [moe_w_coarse[l]] + [moe_w_fine[l][g] for g in range(N_GROUPS)]
        + [zc(D_MODEL, LANES - N_GROUPS - N_EXPERTS)], 1)
    lam_init = 0.8 - 0.6 * math.exp(-0.3 * l)
    lv = diff_lambda[l].astype(F32)
    lam = jnp.exp(jnp.sum(lv[0] * lv[1])) - jnp.exp(jnp.sum(lv[2] * lv[3])) + lam_init
    return dict(
        w_a=wi[:, 0:COL_A].astype(BF16), w_b=w_b.astype(BF16), w_c=w_c.astype(BF16),
        w_uq=w_uq.astype(BF16), w_uk=w_uk.astype(BF16), w_uv=w_uv.astype(BF16),
        q_norm=mla_q_norm[l].reshape(1, B_Q_RANK), kv_norm=mla_kv_norm[l].reshape(1, B_KV_RANK),
        diff_sc=jnp.stack([lam, jnp.asarray(1.0 - lam_init, F32)]).astype(F32),
        subln=jnp.tile(diff_subln[l], A_HEADS).reshape(1, A_W),
        w_out=w_out[l].astype(BF16), ln1_g=ln1_g[l].reshape(1, D_MODEL), ln1_b=ln1_b[l].reshape(1, D_MODEL),
        w_router_hi=w_router.astype(BF16),
        w_router_lo=(w_router - w_router.astype(BF16).astype(F32)).astype(BF16),
        w1=w1[l], w3=w3[l], w2=w2[l],
        ln2_g=ln2_g[l].reshape(1, D_MODEL), ln2_b=ln2_b[l].reshape(1, D_MODEL))


def _encoder_layer(x, tab, lw, batch, seq):
    qa, ka, va, qb, kb, vb, c0, c1, c2 = _in_proj(x, tab, lw, batch, seq)
    oa = _attn_a(lw['diff_sc'], qa, ka, va, lw['subln'], batch, seq)
    ob = _attn_b(qb, kb, vb, batch, seq)
    ocs, lses = zip(*[_attn_c(c, batch, seq, dil) for c, (_, dil) in zip((c0, c1, c2), C_GROUPS)])
    x1, x1t, ri, rw = _out_proj(x, oa, ob, ocs, lses, lw, seq)
    blk_exp, slot_tok, pos = _dispatch_plan(ri[:, 0:TOP_K_INNER])
    ys = _experts(blk_exp, slot_tok, x1t, lw)
    return _combine(pos, x1, rw, ys, lw)


def kernel(x_prompt, x_sample, w_in, diff_lambda, diff_subln, mla_q_norm, mla_w_uq, mla_kv_norm, mla_w_ukv,
           w_out, ln1_g, ln1_b, moe_w_coarse, moe_w_fine, moe_w1, moe_w3, moe_w2, ln2_g, ln2_b):
    w1, w3, w2 = moe_w1.astype(BF16), moe_w3.astype(BF16), moe_w2.astype(BF16)
    layers = [_layer_weights(l, w_in, diff_lambda, diff_subln, mla_q_norm, mla_w_uq, mla_kv_norm, mla_w_ukv,
                             w_out, ln1_g, ln1_b, moe_w_coarse, moe_w_fine, w1, w3, w2, ln2_g, ln2_b)
              for l in range(DEPTH)]

    def trunk(x):
        batch, seq, _ = x.shape
        tab = _rope_table_block(seq)
        h = x.reshape(batch * seq, D_MODEL)
        for lw in layers:
            h = _encoder_layer(h, tab, lw, batch, seq)
        return h.reshape(batch, seq, D_MODEL)

    return (trunk(x_prompt), trunk(x_sample))
```

```python
import functools
import math

import jax
import jax.numpy as jnp
from jax import lax
from jax.experimental import pallas as pl
from jax.experimental.pallas import tpu as pltpu

D_MODEL = 1024
DEPTH = 4
HEAD_DIM = 64
ROPE_THETA = 10000.0
LN_EPS = 1e-5
RMS_EPS = 1e-6
NEG_INF = -1e30

A_HEADS = 4
A_HALF = HEAD_DIM // 2
A_W = A_HEADS * HEAD_DIM

B_HEADS = 6
B_Q_RANK = 256
B_KV_RANK = 128
B_NOPE = 64
B_ROPE = 32
B_V = 64
B_W = B_HEADS * B_V
B_SLOT = 128
B_VT_ROWS = B_V + 16

C_GROUPS = ((128, 1), (512, 4), (2048, 16))
C_HPG = 2
C_GW = C_HPG * HEAD_DIM
C_SIDE = 64
C_BLOCK = 128
C_QUERY_ROWS = 512

COL_A = 3 * A_W
COL_B = B_Q_RANK + B_KV_RANK + B_ROPE

N_GROUPS = 4
EXPERTS_PER_GROUP = 8
N_EXPERTS = N_GROUPS * EXPERTS_PER_GROUP
TOP_K_INNER = 2
D_EXPERT = 512

DN_ALPHA = (2 * DEPTH) ** 0.25

LOG2E = 1.4426950408889634
SCALE_A = (A_HALF ** -0.5) * LOG2E
SCALE_B = ((B_NOPE + B_ROPE) ** -0.5) * LOG2E
SCALE_C = (HEAD_DIM ** -0.5) * LOG2E

LANES = 128
SUBLANES = 8
ROW_BLOCK = 256
Q_BLOCK = 256
K_CHUNK = 256
MOE_ROWS = 256
VMEM_LIMIT = 48 * 1024 * 1024
ATTN_VMEM_LIMIT = 56 * 1024 * 1024

BF16 = jnp.bfloat16
F32 = jnp.float32

_NT = (((1,), (1,)), ((), ()))


def _dot(a, b):
    return jnp.dot(a, b, preferred_element_type=F32)


def _dot_nt(a, b):
    return lax.dot_general(a, b, _NT, preferred_element_type=F32)


def _params(*sem, vmem=VMEM_LIMIT):
    return pltpu.CompilerParams(dimension_semantics=sem, vmem_limit_bytes=vmem)


def _rope(h, cos, sin_signed, half):
    width = h.shape[1]
    lane = lax.broadcasted_iota(jnp.int32, h.shape, 1)
    first = (lane % (2 * half)) < half
    partner = jnp.where(first, pltpu.roll(h, width - half, 1), pltpu.roll(h, half, 1))
    return h * cos + partner * sin_signed


def _rms(x, g):
    return x * lax.rsqrt(jnp.mean(x * x, axis=1, keepdims=True) + RMS_EPS) * g


def _in_proj_kernel(x_ref, tab_ref, wa_ref, wb_ref, wc_ref, wuq_ref, wuk_ref, wuv_ref, qn_ref, kvn_ref,
                    qa_ref, ka_ref, va_ref, qb_ref, kb_ref, vb_ref, c0_ref, c1_ref, c2_ref, cs_scr):
    xb = x_ref[...].astype(BF16)
    cos_a, sin_a = tab_ref[:, 0:256], tab_ref[:, 256:512]
    cos_b, sin_b = tab_ref[:, 512:640], tab_ref[:, 640:768]
    cos_c, sin_c = tab_ref[:, 768:896], tab_ref[:, 896:1024]

    ha = _dot(xb, wa_ref[...])
    qa_ref[...] = (_rope(ha[:, 0:A_W], cos_a, sin_a, A_HALF // 2) * SCALE_A).astype(BF16)
    ka_ref[...] = _rope(ha[:, A_W:2 * A_W], cos_a, sin_a, A_HALF // 2).astype(BF16)
    va_ref[...] = ha[:, 2 * A_W:3 * A_W].astype(BF16)

    hb = _dot(xb, wb_ref[...])
    cq = _rms(hb[:, 0:B_Q_RANK], qn_ref[...]).astype(BF16)
    ckv = _rms(hb[:, B_Q_RANK:B_Q_RANK + B_KV_RANK], kvn_ref[...]).astype(BF16)
    qb = _dot(cq, wuq_ref[...])
    kb = _dot(ckv, wuk_ref[...])
    vb_ref[...] = _dot(ckv, wuv_ref[...]).astype(BF16)
    k_rope = _rope(hb[:, 384:512], cos_b, sin_b, B_ROPE // 2)
    for h in range(B_HEADS):
        sl = slice(B_SLOT * h, B_SLOT * (h + 1))
        qb_ref[:, sl] = (_rope(qb[:, sl], cos_b, sin_b, B_ROPE // 2) * SCALE_B).astype(BF16)
        kb_ref[:, sl] = (kb[:, sl] + k_rope).astype(BF16)

    hc = _dot(xb, wc_ref[...])
    tm = hc.shape[0]
    for g, c_ref in enumerate((c0_ref, c1_ref, c2_ref)):
        base = 3 * C_GW * g
        dil = C_GROUPS[g][1]
        qkv = [_rope(hc[:, base:base + C_GW], cos_c, sin_c, HEAD_DIM // 2) * SCALE_C,
               _rope(hc[:, base + C_GW:base + 2 * C_GW], cos_c, sin_c, HEAD_DIM // 2),
               hc[:, base + 2 * C_GW:base + 3 * C_GW]]
        for part in range(3):
            cols = slice(C_GW * part, C_GW * (part + 1))
            if dil == 1:
                c_ref[0, :, cols] = qkv[part].astype(BF16)
            else:
                cs_scr[part] = qkv[part]
                for j in range(dil):
                    c_ref[j, :, cols] = cs_scr[part, pl.ds(j, tm // dil, stride=dil), :].astype(BF16)


def _dilated_spec(width, dil, nrep):
    return pl.BlockSpec((None, dil, ROW_BLOCK // dil, width), lambda i: (i // nrep, 0, i % nrep, 0))


def _in_proj(x, tab, lw, batch, seq):
    n = x.shape[0]
    tm = ROW_BLOCK
    nrep = seq // tm
    row = lambda w: pl.BlockSpec((tm, w), lambda i: (i, 0))
    full = lambda a: pl.BlockSpec(a.shape, lambda i: (0,) * a.ndim)
    weights = (lw['w_a'], lw['w_b'], lw['w_c'], lw['w_uq'], lw['w_uk'], lw['w_uv'], lw['q_norm'], lw['kv_norm'])
    out_w = (A_W, A_W, A_W, B_HEADS * B_SLOT, B_HEADS * B_SLOT, B_W)
    return pl.pallas_call(
        _in_proj_kernel,
        grid=(n // tm,),
        in_specs=[row(D_MODEL), pl.BlockSpec((tm, 1024), lambda i: (i % nrep, 0))] + [full(w) for w in weights],
        out_specs=[row(w) for w in out_w] + [_dilated_spec(3 * C_GW, dil, nrep) for _, dil in C_GROUPS],
        out_shape=[jax.ShapeDtypeStruct((n, w), BF16) for w in out_w]
        + [jax.ShapeDtypeStruct((batch, dil, seq // dil, 3 * C_GW), BF16) for _, dil in C_GROUPS],
        scratch_shapes=[pltpu.VMEM((3, tm, C_GW), F32)],
        compiler_params=_params("parallel"),
        name="in_proj",
    )(x, tab, *weights)


def _fold_rows(x, op):
    r, c = x.shape
    return op(x.reshape(r // SUBLANES, SUBLANES, c), axis=0)


def _chunk_start(kc):
    return pl.multiple_of(kc * K_CHUNK, K_CHUNK)


def _stage_v_transposed(v_ref, vt_scr):
    @pl.when(pl.program_id(1) == 0)
    def _():
        def body(kc, carry):
            vt_scr[kc] = v_ref[pl.ds(_chunk_start(kc), K_CHUNK), :].astype(F32).T.astype(BF16)
            return carry
        lax.fori_loop(0, vt_scr.shape[0], body, 0)


def _attn_a_kernel(sc_ref, q_ref, k_ref, v_ref, g_ref, o_ref, vt_scr, s_scr):
    lam = sc_ref[0]
    post = sc_ref[1]
    tq = q_ref.shape[0]
    nkc = k_ref.shape[0] // K_CHUNK
    _stage_v_transposed(v_ref, vt_scr)
    qt = q_ref[...].astype(F32).T
    row = lax.broadcasted_iota(jnp.int32, (A_W, tq), 0)
    neg = jnp.full((SUBLANES, tq), -jnp.inf, F32)
    zero = jnp.zeros((SUBLANES, tq), F32)
    maxes, gammas, inv_l1, heads = {}, {}, {}, []
    for st in range(A_HEADS + 2):
        hs, hp, hw = st, st - 1, st - 2
        do_s, do_p, do_w = hs < A_HEADS, 0 <= hp < A_HEADS, 0 <= hw < A_HEADS
        qts = []
        if do_s:
            for c in range(2):
                lo = A_HALF * (2 * hs + c)
                qts.append(jnp.where((row >= lo) & (row < lo + A_HALF), qt, 0.0).astype(BF16))

        def body(kc, carry, qts=qts, hs=hs, hp=hp, hw=hw, do_s=do_s, do_p=do_p, do_w=do_w):
            m1, m2, l1, l2, acc = carry
            if do_s:
                kr = k_ref[pl.ds(_chunk_start(kc), K_CHUNK), :]
                s1 = _dot(kr, qts[0])
                s2 = _dot(kr, qts[1])
                s_scr[hs % 3, 0, kc] = s1
                s_scr[hs % 3, 1, kc] = s2
                m1 = jnp.maximum(m1, _fold_rows(s1, jnp.max))
                m2 = jnp.maximum(m2, _fold_rows(s2, jnp.max))
            if do_p:
                p1 = jnp.exp2(s_scr[hp % 3, 0, kc] - maxes[hp][0])
                p2 = jnp.exp2(s_scr[hp % 3, 1, kc] - maxes[hp][1])
                s_scr[hp % 3, 0, kc] = p1
                s_scr[hp % 3, 1, kc] = p2
                l1 = l1 + _fold_rows(p1, jnp.sum)
                l2 = l2 + _fold_rows(p2, jnp.sum)
            if do_w:
                w = (s_scr[hw % 3, 0, kc] - gammas[hw] * s_scr[hw % 3, 1, kc]).astype(BF16)
                acc = acc + _dot(vt_scr[kc, HEAD_DIM * hw:HEAD_DIM * (hw + 1), :], w)
            return m1, m2, l1, l2, acc

        m1, m2, l1, l2, acc = lax.fori_loop(
            0, nkc, body, (neg, neg, zero, zero, jnp.zeros((HEAD_DIM, tq), F32)), unroll=True)
        if do_s:
            maxes[hs] = (jnp.max(m1, axis=0, keepdims=True), jnp.max(m2, axis=0, keepdims=True))
        if do_p:
            l1 = jnp.sum(l1, axis=0, keepdims=True)
            l2 = jnp.sum(l2, axis=0, keepdims=True)
            gammas[hp] = lam * l1 / l2
            inv_l1[hp] = 1.0 / l1
        if do_w:
            oh = acc * inv_l1[hw]
            ms = jnp.mean(oh * oh, axis=0, keepdims=True)
            heads.append(oh * lax.rsqrt(ms + RMS_EPS))
    out_t = jnp.concatenate(heads, axis=0)
    o_ref[...] = (out_t.T * g_ref[...] * post).astype(BF16)


def _attn_a(sc, q, k, v, g, batch, seq):
    n = q.shape[0]
    tq = Q_BLOCK
    nq = seq // tq
    nkc = seq // K_CHUNK
    return pl.pallas_call(
        _attn_a_kernel,
        grid=(batch, nq),
        in_specs=[pl.BlockSpec(memory_space=pltpu.SMEM),
                  pl.BlockSpec((tq, A_W), lambda b, i: (b * nq + i, 0)),
                  pl.BlockSpec((seq, A_W), lambda b, i: (b, 0)),
                  pl.BlockSpec((seq, A_W), lambda b, i: (b, 0)),
                  pl.BlockSpec((1, A_W), lambda b, i: (0, 0))],
        out_specs=pl.BlockSpec((tq, A_W), lambda b, i: (b * nq + i, 0)),
        out_shape=jax.ShapeDtypeStruct((n, A_W), BF16),
        scratch_shapes=[pltpu.VMEM((nkc, A_W, K_CHUNK), BF16),
                        pltpu.VMEM((3, 2, nkc, K_CHUNK, tq), F32)],
        compiler_params=_params("arbitrary", "arbitrary", vmem=ATTN_VMEM_LIMIT),
        name="attn_a",
    )(sc, q, k, v, g)


def _attn_b_kernel(q_ref, k_ref, v_ref, o_ref, vt_scr, s_scr):
    tq = q_ref.shape[0]
    nkc = k_ref.shape[0] // K_CHUNK

    @pl.when(pl.program_id(1) == 0)
    def _():
        ones = jnp.ones((B_VT_ROWS - B_V, K_CHUNK), BF16)

        def stage(kc, carry):
            vt = v_ref[pl.ds(_chunk_start(kc), K_CHUNK), :].astype(F32).T.astype(BF16)
            for h in range(B_HEADS):
                vt_scr[kc, B_VT_ROWS * h:B_VT_ROWS * h + B_V, :] = vt[B_V * h:B_V * (h + 1), :]
                vt_scr[kc, B_VT_ROWS * h + B_V:B_VT_ROWS * (h + 1), :] = ones
            return carry
        lax.fori_loop(0, nkc, stage, 0)

    neg = jnp.full((SUBLANES, tq), -jnp.inf, F32)
    acc0 = jnp.zeros((B_VT_ROWS, tq), F32)
    n_pairs = B_HEADS // 2
    maxes, heads = {}, []
    for st in range(n_pairs + 1):
        ps, pw = st, st - 1
        do_s, do_w = ps < n_pairs, pw >= 0
        qts = []
        if do_s:
            for j in range(2):
                sl = slice(B_SLOT * (2 * ps + j), B_SLOT * (2 * ps + j + 1))
                qts.append(q_ref[:, sl].astype(F32).T.astype(BF16))

        def body(kc, carry, qts=qts, ps=ps, pw=pw, do_s=do_s, do_w=do_w):
            ms, accs = list(carry[0]), list(carry[1])
            for j in range(2):
                if do_s:
                    h = 2 * ps + j
                    s = _dot(k_ref[pl.ds(_chunk_start(kc), K_CHUNK), B_SLOT * h:B_SLOT * (h + 1)], qts[j])
                    s_scr[ps % 2, j, kc] = s
                    ms[j] = jnp.maximum(ms[j], _fold_rows(s, jnp.max))
                if do_w:
                    h = 2 * pw + j
                    p = jnp.exp2(s_scr[pw % 2, j, kc] - maxes[h]).astype(BF16)
                    accs[j] = accs[j] + _dot(vt_scr[kc, B_VT_ROWS * h:B_VT_ROWS * (h + 1), :], p)
            return tuple(ms), tuple(accs)

        ms, accs = lax.fori_loop(0, nkc, body, ((neg, neg), (acc0, acc0)), unroll=8)
        for j in range(2):
            if do_s:
                maxes[2 * ps + j] = jnp.max(ms[j], axis=0, keepdims=True)
            if do_w:
                heads.append(accs[j][0:B_V] * (1.0 / accs[j][B_V:B_V + 1]))
    o_ref[...] = jnp.concatenate(heads, axis=0).T.astype(BF16)


def _attn_b(q, k, v, batch, seq):
    n = q.shape[0]
    tq = Q_BLOCK
    nq = seq // tq
    nkc = seq // K_CHUNK
    wq = B_HEADS * B_SLOT
    return pl.pallas_call(
        _attn_b_kernel,
        grid=(batch, nq),
        in_specs=[pl.BlockSpec((tq, wq), lambda b, i: (b * nq + i, 0)),
                  pl.BlockSpec((seq, wq), lambda b, i: (b, 0)),
                  pl.BlockSpec((seq, B_W), lambda b, i: (b, 0))],
        out_specs=pl.BlockSpec((tq, B_W), lambda b, i: (b * nq + i, 0)),
        out_shape=jax.ShapeDtypeStruct((n, B_W), BF16),
        scratch_shapes=[pltpu.VMEM((nkc, B_HEADS * B_VT_ROWS, K_CHUNK), BF16),
                        pltpu.VMEM((2, 2, nkc, K_CHUNK, tq), F32)],
        compiler_params=_params("arbitrary", "arbitrary", vmem=ATTN_VMEM_LIMIT),
        name="attn_b",
    )(q, k, v)


def _attn_c_kernel(own_ref, prev_ref, next_ref, o_ref, lse_ref, *, sub_len):
    i = pl.program_id(2)
    n_classes, tq, _ = own_ref.shape
    tb = C_BLOCK
    win = 2 * tb
    lane = lax.broadcasted_iota(jnp.int32, (tb, C_GW), 1)
    key = lax.broadcasted_iota(jnp.int32, (tb, win), 1)
    band = jnp.abs(key - C_SIDE - lax.broadcasted_iota(jnp.int32, (tb, win), 0)) <= C_SIDE
    for c in range(n_classes):
        kv = jnp.concatenate([prev_ref[c, :, C_GW:], own_ref[c, :, C_GW:], next_ref[c, :, C_GW:]], axis=0)
        for u in range(tq // tb):
            q = own_ref[c, tb * u:tb * (u + 1), 0:C_GW]
            kvw = kv[tb * u + C_SIDE:tb * u + C_SIDE + win]
            kpos = i * tq + tb * u - C_SIDE + key
            valid = band & (kpos >= 0) & (kpos < sub_len)
            o = jnp.zeros((tb, C_GW), F32)
            lse = jnp.zeros((tb, C_GW), F32)
            for hh in range(C_HPG):
                head = (lane >= HEAD_DIM * hh) & (lane < HEAD_DIM * (hh + 1))
                qm = jnp.where(head, q, jnp.zeros_like(q))
                s = jnp.where(valid, _dot_nt(qm, kvw[:, 0:C_GW]), NEG_INF)
                m = jnp.max(s, axis=1, keepdims=True)
                p = jnp.exp2(s - m)
                l = jnp.sum(p, axis=1, keepdims=True)
                oh = _dot(p.astype(BF16), kvw[:, C_GW:2 * C_GW]) * (1.0 / l)
                o = jnp.where(head, oh, o)
                lse = jnp.where(head, m + jnp.log(l) * LOG2E, lse)
            o_ref[c, tb * u:tb * (u + 1), :] = o
            lse_ref[c, tb * u:tb * (u + 1), :] = lse


def _attn_c(c, batch, seq, dil):
    sub_len = seq // dil
    tq = min(C_QUERY_ROWS, sub_len)
    nc = min(dil, C_QUERY_ROWS // tq)
    nsub = tq // C_BLOCK
    last = sub_len // C_BLOCK - 1
    o, lse = pl.pallas_call(
        functools.partial(_attn_c_kernel, sub_len=sub_len),
        grid=(batch, dil // nc, sub_len // tq),
        in_specs=[pl.BlockSpec((None, nc, tq, 3 * C_GW), lambda b, j, i: (b, j, i, 0)),
                  pl.BlockSpec((None, nc, C_BLOCK, 3 * C_GW),
                               lambda b, j, i: (b, j, jnp.maximum(i * nsub - 1, 0), 0)),
                  pl.BlockSpec((None, nc, C_BLOCK, 3 * C_GW),
                               lambda b, j, i: (b, j, jnp.minimum((i + 1) * nsub, last), 0))],
        out_specs=[pl.BlockSpec((None, nc, tq, C_GW), lambda b, j, i: (b, j, i, 0))] * 2,
        out_shape=[jax.ShapeDtypeStruct((batch, dil, sub_len, C_GW), F32)] * 2,
        compiler_params=_params("parallel", "parallel", "parallel"),
        name=f"attn_c_d{dil}",
    )(c, c, c)
    return o, lse


def _layer_norm(z, g, b):
    mu = jnp.mean(z, axis=1, keepdims=True)
    zc = z - mu
    var = jnp.mean(zc * zc, axis=1, keepdims=True)
    return zc * lax.rsqrt(var + LN_EPS) * g + b


def _first_index(hit, lane):
    return jnp.min(jnp.where(hit, lane, LANES), axis=1, keepdims=True)


def _token_order(ref, scr):
    dil, sub, _ = ref.shape
    if dil == 1:
        return ref[0]
    for j in range(dil):
        scr[pl.ds(j, sub, stride=dil), :] = ref[j]
    return scr[...]


def _out_proj_kernel(x_ref, oa_ref, ob_ref, oc0_ref, oc1_ref, oc2_ref, l0_ref, l1_ref, l2_ref,
                     wout_ref, g_ref, b_ref, wrh_ref, wrl_ref, x1_ref, x1t_ref, ri_ref, rw_ref, ord_scr):
    oc = [_token_order(r, ord_scr.at[k]) for k, r in enumerate((oc0_ref, oc1_ref, oc2_ref))]
    la, lb, lc = [_token_order(r, ord_scr.at[3 + k]) for k, r in enumerate((l0_ref, l1_ref, l2_ref))]
    mx = jnp.maximum(jnp.maximum(la, lb), lc)
    ea, eb, ec = jnp.exp2(la - mx), jnp.exp2(lb - mx), jnp.exp2(lc - mx)
    inv = 1.0 / (ea + eb + ec)
    mix = jnp.concatenate(
        [oa_ref[...], ob_ref[...],
         (oc[0] * (ea * inv)).astype(BF16),
         (oc[1] * (eb * inv)).astype(BF16),
         (oc[2] * (ec * inv)).astype(BF16)], axis=1)
    x1 = _layer_norm(DN_ALPHA * x_ref[...] + _dot(mix, wout_ref[...]), g_ref[...], b_ref[...])
    x1_ref[...] = x1
    _store_token_tiles(x1t_ref, x1)

    x_hi = x1.astype(BF16)
    x_lo = (x1 - x_hi.astype(F32)).astype(BF16)
    logits = _dot(x_hi, wrh_ref[...]) + (_dot(x_lo, wrh_ref[...]) + _dot(x_hi, wrl_ref[...]))
    lane = lax.broadcasted_iota(jnp.int32, logits.shape, 1)
    ninf = -jnp.inf
    cl = jnp.where(lane < N_GROUPS, logits, ninf)
    cmax = jnp.max(cl, axis=1, keepdims=True)
    grp = _first_index(cl == cmax, lane)
    pg = 1.0 / jnp.sum(jnp.exp(cl - cmax), axis=1, keepdims=True)
    lo = N_GROUPS + EXPERTS_PER_GROUP * grp
    fl = jnp.where((lane >= lo) & (lane < lo + EXPERTS_PER_GROUP), logits, ninf)
    v1 = jnp.max(fl, axis=1, keepdims=True)
    i1 = _first_index(fl == v1, lane)
    fl2 = jnp.where(lane == i1, ninf, fl)
    v2 = jnp.max(fl2, axis=1, keepdims=True)
    i2 = _first_index(fl2 == v2, lane)
    e21 = jnp.exp(v2 - v1)
    t1 = pg / (1.0 + e21)
    t2 = t1 * e21
    ri_ref[...] = jnp.where(lane == 0, i1 - N_GROUPS, jnp.where(lane == 1, i2 - N_GROUPS, 0))
    rw_ref[...] = jnp.where(lane == 0, t1, jnp.where(lane == 1, t2, 0.0))


def _out_proj(x, oa, ob, ocs, lses, lw, seq):
    n = x.shape[0]
    tm = ROW_BLOCK
    nrep = seq // tm
    row = lambda w: pl.BlockSpec((tm, w), lambda i: (i, 0))
    full = lambda a: pl.BlockSpec(a.shape, lambda i: (0,) * a.ndim)
    dilated = [_dilated_spec(C_GW, dil, nrep) for _, dil in C_GROUPS]
    weights = (lw['w_out'], lw['ln1_g'], lw['ln1_b'], lw['w_router_hi'], lw['w_router_lo'])
    return pl.pallas_call(
        _out_proj_kernel,
        grid=(n // tm,),
        in_specs=[row(D_MODEL), row(A_W), row(B_W)] + dilated + dilated + [full(w) for w in weights],
        out_specs=[row(D_MODEL), pl.BlockSpec((tm * SUBLANES, LANES), lambda i: (i, 0)), row(LANES), row(LANES)],
        out_shape=[jax.ShapeDtypeStruct((n, D_MODEL), F32),
                   jax.ShapeDtypeStruct((n * SUBLANES, LANES), F32),
                   jax.ShapeDtypeStruct((n, LANES), jnp.int32),
                   jax.ShapeDtypeStruct((n, LANES), F32)],
        scratch_shapes=[pltpu.VMEM((2 * len(C_GROUPS), tm, C_GW), F32)],
        compiler_params=_params("parallel"),
        name="out_proj",
    )(x, oa, ob, *ocs, *lses, *weights)


def _store_token_tiles(ref, x):
    rows = x.shape[0]
    for j in range(SUBLANES):
        ref[pl.ds(j, rows, stride=SUBLANES), :] = x[:, LANES * j:LANES * (j + 1)]


def _load_token_tiles(ref, first, rows):
    return [ref[pl.ds(first * SUBLANES + j, rows, stride=SUBLANES), :] for j in range(SUBLANES)]


def _start_row_gathers(idx_ref, src_hbm, dst, sem, lo, hi):
    for r in range(lo, hi):
        first = pl.multiple_of(idx_ref[0, 0, r], SUBLANES)
        pltpu.make_async_copy(src_hbm.at[pl.ds(first, SUBLANES)],
                              dst.at[pl.ds(r * SUBLANES, SUBLANES)], sem).start()


def _prefetched_rows(idx_cur_ref, idx_next_ref, src_hbm, buf, sem_g):
    i = pl.program_id(0)
    slot = lax.rem(i, 2)
    tokens = buf.shape[1] // SUBLANES

    @pl.when(i == 0)
    def _():
        _start_row_gathers(idx_cur_ref, src_hbm, buf.at[0], sem_g.at[0], 0, tokens)

    @pl.when(i + 1 < pl.num_programs(0))
    def _():
        _start_row_gathers(idx_next_ref, src_hbm, buf.at[1 - slot], sem_g.at[1 - slot], 0, tokens // 2)

    def drain(r, carry):
        pltpu.make_async_copy(src_hbm.at[pl.ds(0, SUBLANES)], buf.at[slot, pl.ds(0, SUBLANES)],
                              sem_g.at[slot]).wait()
        return carry

    lax.fori_loop(0, tokens, drain, 0, unroll=8)
    return slot


def _prefetch_rest(idx_next_ref, src_hbm, buf, sem_g):
    i = pl.program_id(0)
    slot = lax.rem(i, 2)
    tokens = buf.shape[1] // SUBLANES

    @pl.when(i + 1 < pl.num_programs(0))
    def _():
        _start_row_gathers(idx_next_ref, src_hbm, buf.at[1 - slot], sem_g.at[1 - slot], tokens // 2, tokens)


def _index_specs(rows, steps, prefetch_args=0):
    if prefetch_args:
        return [pl.BlockSpec((1, 1, rows), lambda i, be: (i, 0, 0)),
                pl.BlockSpec((1, 1, rows), lambda i, be: (jnp.minimum(i + 1, steps - 1), 0, 0))]
    return [pl.BlockSpec((1, 1, rows), lambda i: (i, 0, 0)),
            pl.BlockSpec((1, 1, rows), lambda i: (jnp.minimum(i + 1, steps - 1), 0, 0))]


def _gather_scratch(rows):
    return [pltpu.VMEM((2, rows * SUBLANES, LANES), F32), pltpu.SemaphoreType.DMA((2,))]


def _expert_kernel(be_ref, tok_ref, tok_next_ref, x_hbm, w1_ref, w3_ref, w2_ref, y_ref, xg, sem_g):
    del be_ref
    slot = _prefetched_rows(tok_ref, tok_next_ref, x_hbm, xg, sem_g)
    tb = y_ref.shape[0] // SUBLANES
    xb = jnp.concatenate([c.astype(BF16) for c in _load_token_tiles(xg.at[slot], 0, tb)], axis=1)
    h1 = _dot(xb, w1_ref[...])
    h3 = _dot(xb, w3_ref[...])
    hid = (h1 * (1.0 / (1.0 + jnp.exp(-h1))) * h3).astype(BF16)
    _store_token_tiles(y_ref, _dot(hid, w2_ref[...]))
    _prefetch_rest(tok_next_ref, x_hbm, xg, sem_g)


def _experts(blk_exp, slot_tok, x1t, lw):
    n_blocks = blk_exp.shape[0]
    tb = MOE_ROWS
    tok = (slot_tok * SUBLANES).reshape(n_blocks, 1, tb)
    grid_spec = pltpu.PrefetchScalarGridSpec(
        num_scalar_prefetch=1,
        grid=(n_blocks,),
        in_specs=_index_specs(tb, n_blocks, prefetch_args=1) + [
            pl.BlockSpec(memory_space=pl.ANY),
            pl.BlockSpec((None, D_MODEL, D_EXPERT), lambda i, be: (be[i], 0, 0)),
            pl.BlockSpec((None, D_MODEL, D_EXPERT), lambda i, be: (be[i], 0, 0)),
            pl.BlockSpec((None, D_EXPERT, D_MODEL), lambda i, be: (be[i], 0, 0))],
        out_specs=pl.BlockSpec((tb * SUBLANES, LANES), lambda i, be: (i, 0)),
        scratch_shapes=_gather_scratch(tb))
    return pl.pallas_call(
        _expert_kernel,
        grid_spec=grid_spec,
        out_shape=jax.ShapeDtypeStruct((n_blocks * tb * SUBLANES, LANES), F32),
        compiler_params=_params("arbitrary"),
        name="experts",
    )(blk_exp, tok, tok, x1t, lw['w1'], lw['w3'], lw['w2'])


def _combine_kernel(pos_ref, pos_next_ref, x1_ref, rw_ref, ys_hbm, g_ref, b_ref, out_ref, yg, sem_g):
    tm = x1_ref.shape[0]
    slot = _prefetched_rows(pos_ref, pos_next_ref, ys_hbm, yg, sem_g)
    rw = rw_ref[...]
    w0, w1 = rw[:, 0:1], rw[:, 1:2]
    y0 = _load_token_tiles(yg.at[slot], 0, tm)
    y1 = _load_token_tiles(yg.at[slot], tm, tm)
    y = jnp.concatenate([w0 * a + w1 * b for a, b in zip(y0, y1)], axis=1)
    out_ref[...] = _layer_norm(DN_ALPHA * x1_ref[...] + y, g_ref[...], b_ref[...])
    _prefetch_rest(pos_next_ref, ys_hbm, yg, sem_g)


def _combine(pos, x1, rw, ys, lw):
    n = x1.shape[0]
    tm = ROW_BLOCK
    nb = n // tm
    rows = TOP_K_INNER * tm
    pos_blk = (pos * SUBLANES).reshape(nb, tm, TOP_K_INNER).transpose(0, 2, 1).reshape(nb, 1, rows)
    full = lambda a: pl.BlockSpec(a.shape, lambda i: (0,) * a.ndim)
    return pl.pallas_call(
        _combine_kernel,
        grid=(nb,),
        in_specs=_index_specs(rows, nb) + [
            pl.BlockSpec((tm, D_MODEL), lambda i: (i, 0)),
            pl.BlockSpec((tm, LANES), lambda i: (i, 0)),
            pl.BlockSpec(memory_space=pl.ANY),
            full(lw['ln2_g']), full(lw['ln2_b'])],
        out_specs=pl.BlockSpec((tm, D_MODEL), lambda i: (i, 0)),
        out_shape=jax.ShapeDtypeStruct((n, D_MODEL), F32),
        scratch_shapes=_gather_scratch(rows),
        compiler_params=_params("arbitrary"),
        name="combine",
    )(pos_blk, pos_blk, x1, rw, ys, lw['ln2_g'], lw['ln2_b'])


def _dispatch_plan(eid):
    n = eid.shape[0]
    a = n * TOP_K_INNER
    tb = MOE_ROWS
    e_flat = eid.reshape(a)
    order = jnp.argsort(e_flat).astype(jnp.int32)
    experts = jnp.arange(N_EXPERTS, dtype=jnp.int32)
    counts = jnp.sum((e_flat[:, None] == experts[None, :]).astype(jnp.int32), axis=0)
    padded = (counts + tb - 1) // tb * tb
    pad_end = jnp.cumsum(padded)
    pad_start = pad_end - padded
    start = jnp.cumsum(counts) - counts
    shift = pad_start - start
    n_blocks = -(-(a + N_EXPERTS * (tb - 1)) // tb)
    blk_start = jnp.arange(n_blocks, dtype=jnp.int32) * tb
    blk_exp = jnp.minimum(jnp.sum((blk_start[:, None] >= pad_end[None, :]).astype(jnp.int32), axis=1),
                          N_EXPERTS - 1)
    e_slot = jnp.repeat(blk_exp, tb)
    sorted_pos = jnp.arange(n_blocks * tb, dtype=jnp.int32) - shift[e_slot]
    valid = sorted_pos < (start + counts)[e_slot]
    slot_tok = jnp.where(valid, order[jnp.clip(sorted_pos, 0, a - 1)] // TOP_K_INNER, 0)
    dest = shift[e_flat[order]] + jnp.arange(a, dtype=jnp.int32)
    pos = dest[jnp.argsort(order)].reshape(n, TOP_K_INNER)
    return blk_exp, slot_tok, pos


def _rope_tables(seq, dim):
    inv_freq = 1.0 / (ROPE_THETA ** (jnp.arange(0, dim, 2, dtype=F32) / dim))
    ang = jnp.arange(seq, dtype=F32)[:, None] * inv_freq[None, :]
    return jnp.cos(ang), jnp.sin(ang)


def _rope_table_block(seq):
    c16, s16 = _rope_tables(seq, A_HALF)
    c32, s32 = _rope_tables(seq, HEAD_DIM)
    ones = lambda w: jnp.ones((seq, w), F32)
    zeros = lambda w: jnp.zeros((seq, w), F32)
    cos_a = jnp.tile(jnp.concatenate([c16, c16], 1), (1, 2 * A_HEADS))
    sin_a = jnp.tile(jnp.concatenate([-s16, s16], 1), (1, 2 * A_HEADS))
    cos_b = jnp.concatenate([ones(B_NOPE), c16, c16, ones(B_SLOT - B_NOPE - B_ROPE)], 1)
    sin_b = jnp.concatenate([zeros(B_NOPE), -s16, s16, zeros(B_SLOT - B_NOPE - B_ROPE)], 1)
    cos_c = jnp.tile(jnp.concatenate([c32, c32], 1), (1, C_HPG))
    sin_c = jnp.tile(jnp.concatenate([-s32, s32], 1), (1, C_HPG))
    return jnp.concatenate([cos_a, sin_a, cos_b, sin_b, cos_c, sin_c], 1)


def _layer_weights(l, w_in, diff_lambda, diff_subln, mla_q_norm, mla_w_uq, mla_kv_norm, mla_w_ukv, w_out,
                   ln1_g, ln1_b, moe_w_coarse, moe_w_fine, w1, w3, w2, ln2_g, ln2_b):
    wi = w_in[l]
    zc = lambda rows, w: jnp.zeros((rows, w), F32)
    b0 = COL_A
    w_b = jnp.concatenate([wi[:, b0:b0 + B_Q_RANK + B_KV_RANK], zc(D_MODEL, B_NOPE),
                           wi[:, b0 + B_Q_RANK + B_KV_RANK:b0 + COL_B], zc(D_MODEL, B_SLOT - B_NOPE - B_ROPE)], 1)
    c0 = COL_A + COL_B
    cw = C_HPG * HEAD_DIM * len(C_GROUPS)
    w_c = jnp.concatenate([wi[:, c0 + part * cw + g * C_GW:c0 + part * cw + (g + 1) * C_GW]
                           for g in range(len(C_GROUPS)) for part in range(3)], 1)
    qd = B_NOPE + B_ROPE
    w_uq = jnp.concatenate([jnp.concatenate([mla_w_uq[l][:, h * qd:(h + 1) * qd], zc(B_Q_RANK, B_SLOT - qd)], 1)
                            for h in range(B_HEADS)], 1)
    kvd = B_NOPE + B_V
    w_uk = jnp.concatenate([jnp.concatenate([mla_w_ukv[l][:, h * kvd:h * kvd + B_NOPE],
                                             zc(B_KV_RANK, B_SLOT - B_NOPE)], 1) for h in range(B_HEADS)], 1)
    w_uv = jnp.concatenate([mla_w_ukv[l][:, h * kvd + B_NOPE:(h + 1) * kvd] for h in range(B_HEADS)], 1)
    w_router = jnp.concatenate(
        [moe_w_coarse[l]] + [moe_w_fine[l][g] for g in range(N_GROUPS)]
        + [zc(D_MODEL, LANES - N_GROUPS - N_EXPERTS)], 1)
    lam_init = 0.8 - 0.6 * math.exp(-0.3 * l)
    lv = diff_lambda[l].astype(F32)
    lam = jnp.exp(jnp.sum(lv[0] * lv[1])) - jnp.exp(jnp.sum(lv[2] * lv[3])) + lam_init
    return dict(
        w_a=wi[:, 0:COL_A].astype(BF16), w_b=w_b.astype(BF16), w_c=w_c.astype(BF16),
        w_uq=w_uq.astype(BF16), w_uk=w_uk.astype(BF16), w_uv=w_uv.astype(BF16),
        q_norm=mla_q_norm[l].reshape(1, B_Q_RANK), kv_norm=mla_kv_norm[l].reshape(1, B_KV_RANK),
        diff_sc=jnp.stack([lam, jnp.asarray(1.0 - lam_init, F32)]).astype(F32),
        subln=jnp.tile(diff_subln[l], A_HEADS).reshape(1, A_W),
        w_out=w_out[l].astype(BF16), ln1_g=ln1_g[l].reshape(1, D_MODEL), ln1_b=ln1_b[l].reshape(1, D_MODEL),
        w_router_hi=w_router.astype(BF16),
        w_router_lo=(w_router - w_router.astype(BF16).astype(F32)).astype(BF16),
        w1=w1[l], w3=w3[l], w2=w2[l],
        ln2_g=ln2_g[l].reshape(1, D_MODEL), ln2_b=ln2_b[l].reshape(1, D_MODEL))


def _encoder_layer(x, tab, lw, batch, seq):
    qa, ka, va, qb, kb, vb, c0, c1, c2 = _in_proj(x, tab, lw, batch, seq)
    oa = _attn_a(lw['diff_sc'], qa, ka, va, lw['subln'], batch, seq)
    ob = _attn_b(qb, kb, vb, batch, seq)
    ocs, lses = zip(*[_attn_c(c, batch, seq, dil) for c, (_, dil) in zip((c0, c1, c2), C_GROUPS)])
    x1, x1t, ri, rw = _out_proj(x, oa, ob, ocs, lses, lw, seq)
    blk_exp, slot_tok, pos = _dispatch_plan(ri[:, 0:TOP_K_INNER])
    ys = _experts(blk_exp, slot_tok, x1t, lw)
    return _combine(pos, x1, rw, ys, lw)


def kernel(x_prompt, x_sample, w_in, diff_lambda, diff_subln, mla_q_norm, mla_w_uq, mla_kv_norm, mla_w_ukv,
           w_out, ln1_g, ln1_b, moe_w_coarse, moe_w_fine, moe_w1, moe_w3, moe_w2, ln2_g, ln2_b):
    w1, w3, w2 = moe_w1.astype(BF16), moe_w3.astype(BF16), moe_w2.astype(BF16)
    layers = [_layer_weights(l, w_in, diff_lambda, diff_subln, mla_q_norm, mla_w_uq, mla_kv_norm, mla_w_ukv,
                             w_out, ln1_g, ln1_b, moe_w_coarse, moe_w_fine, w1, w3, w2, ln2_g, ln2_b)
              for l in range(DEPTH)]

    def trunk(x):
        batch, seq, _ = x.shape
        tab = _rope_table_block(seq)
        h = x.reshape(batch * seq, D_MODEL)
        for lw in layers:
            h = _encoder_layer(h, tab, lw, batch, seq)
        return h.reshape(batch, seq, D_MODEL)

    return (trunk(x_prompt), trunk(x_sample))
```

```python
import functools
import math

import jax
import jax.numpy as jnp
from jax import lax
from jax.experimental import pallas as pl
from jax.experimental.pallas import tpu as pltpu

D_MODEL = 1024
DEPTH = 4
HEAD_DIM = 64
ROPE_THETA = 10000.0
LN_EPS = 1e-5
RMS_EPS = 1e-6
NEG_INF = -1e30

A_HEADS = 4
A_HALF = HEAD_DIM // 2
A_W = A_HEADS * HEAD_DIM

B_HEADS = 6
B_Q_RANK = 256
B_KV_RANK = 128
B_NOPE = 64
B_ROPE = 32
B_V = 64
B_W = B_HEADS * B_V
B_SLOT = 128
B_VT_ROWS = B_V + 16

C_GROUPS = ((128, 1), (512, 4), (2048, 16))
C_HPG = 2
C_GW = C_HPG * HEAD_DIM
C_SIDE = 64
C_BLOCK = 128
C_QUERY_ROWS = 512

COL_A = 3 * A_W
COL_B = B_Q_RANK + B_KV_RANK + B_ROPE

N_GROUPS = 4
EXPERTS_PER_GROUP = 8
N_EXPERTS = N_GROUPS * EXPERTS_PER_GROUP
TOP_K_INNER = 2
D_EXPERT = 512

DN_ALPHA = (2 * DEPTH) ** 0.25

LOG2E = 1.4426950408889634
SCALE_A = (A_HALF ** -0.5) * LOG2E
SCALE_B = ((B_NOPE + B_ROPE) ** -0.5) * LOG2E
SCALE_C = (HEAD_DIM ** -0.5) * LOG2E

LANES = 128
SUBLANES = 8
ROW_BLOCK = 256
Q_BLOCK = 256
K_CHUNK = 256
MOE_ROWS = 256
VMEM_LIMIT = 48 * 1024 * 1024
ATTN_VMEM_LIMIT = 56 * 1024 * 1024

BF16 = jnp.bfloat16
F32 = jnp.float32

_NT = (((1,), (1,)), ((), ()))


def _dot(a, b):
    return jnp.dot(a, b, preferred_element_type=F32)


def _dot_nt(a, b):
    return lax.dot_general(a, b, _NT, preferred_element_type=F32)


def _params(*sem, vmem=VMEM_LIMIT):
    return pltpu.CompilerParams(dimension_semantics=sem, vmem_limit_bytes=vmem)


def _rope(h, cos, sin_signed, half):
    width = h.shape[1]
    lane = lax.broadcasted_iota(jnp.int32, h.shape, 1)
    first = (lane % (2 * half)) < half
    partner = jnp.where(first, pltpu.roll(h, width - half, 1), pltpu.roll(h, half, 1))
    return h * cos + partner * sin_signed


def _rms(x, g):
    return x * lax.rsqrt(jnp.mean(x * x, axis=1, keepdims=True) + RMS_EPS) * g


def _in_proj_kernel(x_ref, tab_ref, wa_ref, wb_ref, wc_ref, wuq_ref, wuk_ref, wuv_ref, qn_ref, kvn_ref,
                    qa_ref, ka_ref, va_ref, qb_ref, kb_ref, vb_ref, c0_ref, c1_ref, c2_ref, cs_scr):
    xb = x_ref[...].astype(BF16)
    cos_a, sin_a = tab_ref[:, 0:256], tab_ref[:, 256:512]
    cos_b, sin_b = tab_ref[:, 512:640], tab_ref[:, 640:768]
    cos_c, sin_c = tab_ref[:, 768:896], tab_ref[:, 896:1024]

    ha = _dot(xb, wa_ref[...])
    qa_ref[...] = (_rope(ha[:, 0:A_W], cos_a, sin_a, A_HALF // 2) * SCALE_A).astype(BF16)
    ka_ref[...] = _rope(ha[:, A_W:2 * A_W], cos_a, sin_a, A_HALF // 2).astype(BF16)
    va_ref[...] = ha[:, 2 * A_W:3 * A_W].astype(BF16)

    hb = _dot(xb, wb_ref[...])
    cq = _rms(hb[:, 0:B_Q_RANK], qn_ref[...]).astype(BF16)
    ckv = _rms(hb[:, B_Q_RANK:B_Q_RANK + B_KV_RANK], kvn_ref[...]).astype(BF16)
    qb = _dot(cq, wuq_ref[...])
    kb = _dot(ckv, wuk_ref[...])
    vb_ref[...] = _dot(ckv, wuv_ref[...]).astype(BF16)
    k_rope = _rope(hb[:, 384:512], cos_b, sin_b, B_ROPE // 2)
    for h in range(B_HEADS):
        sl = slice(B_SLOT * h, B_SLOT * (h + 1))
        qb_ref[:, sl] = (_rope(qb[:, sl], cos_b, sin_b, B_ROPE // 2) * SCALE_B).astype(BF16)
        kb_ref[:, sl] = (kb[:, sl] + k_rope).astype(BF16)

    hc = _dot(xb, wc_ref[...])
    tm = hc.shape[0]
    for g, c_ref in enumerate((c0_ref, c1_ref, c2_ref)):
        base = 3 * C_GW * g
        dil = C_GROUPS[g][1]
        qkv = [_rope(hc[:, base:base + C_GW], cos_c, sin_c, HEAD_DIM // 2) * SCALE_C,
               _rope(hc[:, base + C_GW:base + 2 * C_GW], cos_c, sin_c, HEAD_DIM // 2),
               hc[:, base + 2 * C_GW:base + 3 * C_GW]]
        for part in range(3):
            cols = slice(C_GW * part, C_GW * (part + 1))
            if dil == 1:
                c_ref[0, :, cols] = qkv[part].astype(BF16)
            else:
                cs_scr[part] = qkv[part]
                for j in range(dil):
                    c_ref[j, :, cols] = cs_scr[part, pl.ds(j, tm // dil, stride=dil), :].astype(BF16)


def _dilated_spec(width, dil, nrep):
    return pl.BlockSpec((None, dil, ROW_BLOCK // dil, width), lambda i: (i // nrep, 0, i % nrep, 0))


def _in_proj(x, tab, lw, batch, seq):
    n = x.shape[0]
    tm = ROW_BLOCK
    nrep = seq // tm
    row = lambda w: pl.BlockSpec((tm, w), lambda i: (i, 0))
    full = lambda a: pl.BlockSpec(a.shape, lambda i: (0,) * a.ndim)
    weights = (lw['w_a'], lw['w_b'], lw['w_c'], lw['w_uq'], lw['w_uk'], lw['w_uv'], lw['q_norm'], lw['kv_norm'])
    out_w = (A_W, A_W, A_W, B_HEADS * B_SLOT, B_HEADS * B_SLOT, B_W)
    return pl.pallas_call(
        _in_proj_kernel,
        grid=(n // tm,),
        in_specs=[row(D_MODEL), pl.BlockSpec((tm, 1024), lambda i: (i % nrep, 0))] + [full(w) for w in weights],
        out_specs=[row(w) for w in out_w] + [_dilated_spec(3 * C_GW, dil, nrep) for _, dil in C_GROUPS],
        out_shape=[jax.ShapeDtypeStruct((n, w), BF16) for w in out_w]
        + [jax.ShapeDtypeStruct((batch, dil, seq // dil, 3 * C_GW), BF16) for _, dil in C_GROUPS],
        scratch_shapes=[pltpu.VMEM((3, tm, C_GW), F32)],
        compiler_params=_params("parallel"),
        name="in_proj",
    )(x, tab, *weights)


def _fold_rows(x, op):
    r, c = x.shape
    return op(x.reshape(r // SUBLANES, SUBLANES, c), axis=0)


def _chunk_start(kc):
    return pl.multiple_of(kc * K_CHUNK, K_CHUNK)


def _stage_v_transposed(v_ref, vt_scr):
    @pl.when(pl.program_id(1) == 0)
    def _():
        def body(kc, carry):
            vt_scr[kc] = v_ref[pl.ds(_chunk_start(kc), K_CHUNK), :].astype(F32).T.astype(BF16)
            return carry
        lax.fori_loop(0, vt_scr.shape[0], body, 0)


def _attn_a_kernel(sc_ref, q_ref, k_ref, v_ref, g_ref, o_ref, vt_scr, s_scr):
    lam = sc_ref[0]
    post = sc_ref[1]
    tq = q_ref.shape[0]
    nkc = k_ref.shape[0] // K_CHUNK
    _stage_v_transposed(v_ref, vt_scr)
    qt = q_ref[...].astype(F32).T
    row = lax.broadcasted_iota(jnp.int32, (A_W, tq), 0)
    neg = jnp.full((SUBLANES, tq), -jnp.inf, F32)
    zero = jnp.zeros((SUBLANES, tq), F32)
    maxes, gammas, inv_l1, heads = {}, {}, {}, []
    for st in range(A_HEADS + 2):
        hs, hp, hw = st, st - 1, st - 2
        do_s, do_p, do_w = hs < A_HEADS, 0 <= hp < A_HEADS, 0 <= hw < A_HEADS
        qts = []
        if do_s:
            for c in range(2):
                lo = A_HALF * (2 * hs + c)
                qts.append(jnp.where((row >= lo) & (row < lo + A_HALF), qt, 0.0).astype(BF16))

        def body(kc, carry, qts=qts, hs=hs, hp=hp, hw=hw, do_s=do_s, do_p=do_p, do_w=do_w):
            m1, m2, l1, l2, acc = carry
            if do_s:
                kr = k_ref[pl.ds(_chunk_start(kc), K_CHUNK), :]
                s1 = _dot(kr, qts[0])
                s2 = _dot(kr, qts[1])
                s_scr[hs % 3, 0, kc] = s1
                s_scr[hs % 3, 1, kc] = s2
                m1 = jnp.maximum(m1, _fold_rows(s1, jnp.max))
                m2 = jnp.maximum(m2, _fold_rows(s2, jnp.max))
            if do_p:
                p1 = jnp.exp2(s_scr[hp % 3, 0, kc] - maxes[hp][0])
                p2 = jnp.exp2(s_scr[hp % 3, 1, kc] - maxes[hp][1])
                s_scr[hp % 3, 0, kc] = p1
                s_scr[hp % 3, 1, kc] = p2
                l1 = l1 + _fold_rows(p1, jnp.sum)
                l2 = l2 + _fold_rows(p2, jnp.sum)
            if do_w:
                w = (s_scr[hw % 3, 0, kc] - gammas[hw] * s_scr[hw % 3, 1, kc]).astype(BF16)
                acc = acc + _dot(vt_scr[kc, HEAD_DIM * hw:HEAD_DIM * (hw + 1), :], w)
            return m1, m2, l1, l2, acc

        m1, m2, l1, l2, acc = lax.fori_loop(
            0, nkc, body, (neg, neg, zero, zero, jnp.zeros((HEAD_DIM, tq), F32)), unroll=True)
        if do_s:
            maxes[hs] = (jnp.max(m1, axis=0, keepdims=True), jnp.max(m2, axis=0, keepdims=True))
        if do_p:
            l1 = jnp.sum(l1, axis=0, keepdims=True)
            l2 = jnp.sum(l2, axis=0, keepdims=True)
            gammas[hp] = lam * l1 / l2
            inv_l1[hp] = 1.0 / l1
        if do_w:
            oh = acc * inv_l1[hw]
            ms = jnp.mean(oh * oh, axis=0, keepdims=True)
            heads.append(oh * lax.rsqrt(ms + RMS_EPS))
    out_t = jnp.concatenate(heads, axis=0)
    o_ref[...] = (out_t.T * g_ref[...] * post).astype(BF16)


def _attn_a(sc, q, k, v, g, batch, seq):
    n = q.shape[0]
    tq = Q_BLOCK
    nq = seq // tq
    nkc = seq // K_CHUNK
    return pl.pallas_call(
        _attn_a_kernel,
        grid=(batch, nq),
        in_specs=[pl.BlockSpec(memory_space=pltpu.SMEM),
                  pl.BlockSpec((tq, A_W), lambda b, i: (b * nq + i, 0)),
                  pl.BlockSpec((seq, A_W), lambda b, i: (b, 0)),
                  pl.BlockSpec((seq, A_W), lambda b, i: (b, 0)),
                  pl.BlockSpec((1, A_W), lambda b, i: (0, 0))],
        out_specs=pl.BlockSpec((tq, A_W), lambda b, i: (b * nq + i, 0)),
        out_shape=jax.ShapeDtypeStruct((n, A_W), BF16),
        scratch_shapes=[pltpu.VMEM((nkc, A_W, K_CHUNK), BF16),
                        pltpu.VMEM((3, 2, nkc, K_CHUNK, tq), F32)],
        compiler_params=_params("arbitrary", "arbitrary", vmem=ATTN_VMEM_LIMIT),
        name="attn_a",
    )(sc, q, k, v, g)


def _attn_b_kernel(q_ref, k_ref, v_ref, o_ref, vt_scr, s_scr):
    tq = q_ref.shape[0]
    nkc = k_ref.shape[0] // K_CHUNK

    @pl.when(pl.program_id(1) == 0)
    def _():
        ones = jnp.ones((B_VT_ROWS - B_V, K_CHUNK), BF16)

        def stage(kc, carry):
            vt = v_ref[pl.ds(_chunk_start(kc), K_CHUNK), :].astype(F32).T.astype(BF16)
            for h in range(B_HEADS):
                vt_scr[kc, B_VT_ROWS * h:B_VT_ROWS * h + B_V, :] = vt[B_V * h:B_V * (h + 1), :]
                vt_scr[kc, B_VT_ROWS * h + B_V:B_VT_ROWS * (h + 1), :] = ones
            return carry
        lax.fori_loop(0, nkc, stage, 0)

    neg = jnp.full((SUBLANES, tq), -jnp.inf, F32)
    acc0 = jnp.zeros((B_VT_ROWS, tq), F32)
    n_pairs = B_HEADS // 2
    maxes, heads = {}, []
    for st in range(n_pairs + 1):
        ps, pw = st, st - 1
        do_s, do_w = ps < n_pairs, pw >= 0
        qts = []
        if do_s:
            for j in range(2):
                sl = slice(B_SLOT * (2 * ps + j), B_SLOT * (2 * ps + j + 1))
                qts.append(q_ref[:, sl].astype(F32).T.astype(BF16))

        def body(kc, carry, qts=qts, ps=ps, pw=pw, do_s=do_s, do_w=do_w):
            ms, accs = list(carry[0]), list(carry[1])
            for j in range(2):
                if do_s:
                    h = 2 * ps + j
                    s = _dot(k_ref[pl.ds(_chunk_start(kc), K_CHUNK), B_SLOT * h:B_SLOT * (h + 1)], qts[j])
                    s_scr[ps % 2, j, kc] = s
                    ms[j] = jnp.maximum(ms[j], _fold_rows(s, jnp.max))
                if do_w:
                    h = 2 * pw + j
                    p = jnp.exp2(s_scr[pw % 2, j, kc] - maxes[h]).astype(BF16)
                    accs[j] = accs[j] + _dot(vt_scr[kc, B_VT_ROWS * h:B_VT_ROWS * (h + 1), :], p)
            return tuple(ms), tuple(accs)

        ms, accs = lax.fori_loop(0, nkc, body, ((neg, neg), (acc0, acc0)), unroll=8)
        for j in range(2):
            if do_s:
                maxes[2 * ps + j] = jnp.max(ms[j], axis=0, keepdims=True)
            if do_w:
                heads.append(accs[j][0:B_V] * (1.0 / accs[j][B_V:B_V + 1]))
    o_ref[...] = jnp.concatenate(heads, axis=0).T.astype(BF16)


def _attn_b(q, k, v, batch, seq):
    n = q.shape[0]
    tq = Q_BLOCK
    nq = seq // tq
    nkc = seq // K_CHUNK
    wq = B_HEADS * B_SLOT
    return pl.pallas_call(
        _attn_b_kernel,
        grid=(batch, nq),
        in_specs=[pl.BlockSpec((tq, wq), lambda b, i: (b * nq + i, 0)),
                  pl.BlockSpec((seq, wq), lambda b, i: (b, 0)),
                  pl.BlockSpec((seq, B_W), lambda b, i: (b, 0))],
        out_specs=pl.BlockSpec((tq, B_W), lambda b, i: (b * nq + i, 0)),
        out_shape=jax.ShapeDtypeStruct((n, B_W), BF16),
        scratch_shapes=[pltpu.VMEM((nkc, B_HEADS * B_VT_ROWS, K_CHUNK), BF16),
                        pltpu.VMEM((2, 2, nkc, K_CHUNK, tq), F32)],
        compiler_params=_params("arbitrary", "arbitrary", vmem=ATTN_VMEM_LIMIT),
        name="attn_b",
    )(q, k, v)


def _attn_c_kernel(own_ref, prev_ref, next_ref, o_ref, lse_ref, *, sub_len):
    i = pl.program_id(2)
    n_classes, tq, _ = own_ref.shape
    tb = C_BLOCK
    win = 2 * tb
    lane = lax.broadcasted_iota(jnp.int32, (tb, C_GW), 1)
    key = lax.broadcasted_iota(jnp.int32, (tb, win), 1)
    band = jnp.abs(key - C_SIDE - lax.broadcasted_iota(jnp.int32, (tb, win), 0)) <= C_SIDE
    for c in range(n_classes):
        kv = jnp.concatenate([prev_ref[c, :, C_GW:], own_ref[c, :, C_GW:], next_ref[c, :, C_GW:]], axis=0)
        for u in range(tq // tb):
            q = own_ref[c, tb * u:tb * (u + 1), 0:C_GW]
            kvw = kv[tb * u + C_SIDE:tb * u + C_SIDE + win]
            kpos = i * tq + tb * u - C_SIDE + key
            valid = band & (kpos >= 0) & (kpos < sub_len)
            o = jnp.zeros((tb, C_GW), F32)
            lse = jnp.zeros((tb, C_GW), F32)
            for hh in range(C_HPG):
                head = (lane >= HEAD_DIM * hh) & (lane < HEAD_DIM * (hh + 1))
                qm = jnp.where(head, q, jnp.zeros_like(q))
                s = jnp.where(valid, _dot_nt(qm, kvw[:, 0:C_GW]), NEG_INF)
                m = jnp.max(s, axis=1, keepdims=True)
                p = jnp.exp2(s - m)
                l = jnp.sum(p, axis=1, keepdims=True)
                oh = _dot(p.astype(BF16), kvw[:, C_GW:2 * C_GW]) * (1.0 / l)
                o = jnp.where(head, oh, o)
                lse = jnp.where(head, m + jnp.log(l) * LOG2E, lse)
            o_ref[c, tb * u:tb * (u + 1), :] = o
            lse_ref[c, tb * u:tb * (u + 1), :] = lse


def _attn_c(c, batch, seq, dil):
    sub_len = seq // dil
    tq = min(C_QUERY_ROWS, sub_len)
    nc = min(dil, C_QUERY_ROWS // tq)
    nsub = tq // C_BLOCK
    last = sub_len // C_BLOCK - 1
    o, lse = pl.pallas_call(
        functools.partial(_attn_c_kernel, sub_len=sub_len),
        grid=(batch, dil // nc, sub_len // tq),
        in_specs=[pl.BlockSpec((None, nc, tq, 3 * C_GW), lambda b, j, i: (b, j, i, 0)),
                  pl.BlockSpec((None, nc, C_BLOCK, 3 * C_GW),
                               lambda b, j, i: (b, j, jnp.maximum(i * nsub - 1, 0), 0)),
                  pl.BlockSpec((None, nc, C_BLOCK, 3 * C_GW),
                               lambda b, j, i: (b, j, jnp.minimum((i + 1) * nsub, last), 0))],
        out_specs=[pl.BlockSpec((None, nc, tq, C_GW), lambda b, j, i: (b, j, i, 0))] * 2,
        out_shape=[jax.ShapeDtypeStruct((batch, dil, sub_len, C_GW), F32)] * 2,
        compiler_params=_params("parallel", "parallel", "parallel"),
        name=f"attn_c_d{dil}",
    )(c, c, c)
    return o, lse


def _layer_norm(z, g, b):
    mu = jnp.mean(z, axis=1, keepdims=True)
    zc = z - mu
    var = jnp.mean(zc * zc, axis=1, keepdims=True)
    return zc * lax.rsqrt(var + LN_EPS) * g + b


def _first_index(hit, lane):
    return jnp.min(jnp.where(hit, lane, LANES), axis=1, keepdims=True)


def _token_order(ref, scr):
    dil, sub, _ = ref.shape
    if dil == 1:
        return ref[0]
    for j in range(dil):
        scr[pl.ds(j, sub, stride=dil), :] = ref[j]
    return scr[...]


def _out_proj_kernel(x_ref, oa_ref, ob_ref, oc0_ref, oc1_ref, oc2_ref, l0_ref, l1_ref, l2_ref,
                     wout_ref, g_ref, b_ref, wrh_ref, wrl_ref, x1_ref, x1t_ref, ri_ref, rw_ref, ord_scr):
    oc = [_token_order(r, ord_scr.at[k]) for k, r in enumerate((oc0_ref, oc1_ref, oc2_ref))]
    la, lb, lc = [_token_order(r, ord_scr.at[3 + k]) for k, r in enumerate((l0_ref, l1_ref, l2_ref))]
    mx = jnp.maximum(jnp.maximum(la, lb), lc)
    ea, eb, ec = jnp.exp2(la - mx), jnp.exp2(lb - mx), jnp.exp2(lc - mx)
    inv = 1.0 / (ea + eb + ec)
    mix = jnp.concatenate(
        [oa_ref[...], ob_ref[...],
         (oc[0] * (ea * inv)).astype(BF16),
         (oc[1] * (eb * inv)).astype(BF16),
         (oc[2] * (ec * inv)).astype(BF16)], axis=1)
    x1 = _layer_norm(DN_ALPHA * x_ref[...] + _dot(mix, wout_ref[...]), g_ref[...], b_ref[...])
    x1_ref[...] = x1
    _store_token_tiles(x1t_ref, x1)

    x_hi = x1.astype(BF16)
    x_lo = (x1 - x_hi.astype(F32)).astype(BF16)
    logits = _dot(x_hi, wrh_ref[...]) + (_dot(x_lo, wrh_ref[...]) + _dot(x_hi, wrl_ref[...]))
    lane = lax.broadcasted_iota(jnp.int32, logits.shape, 1)
    ninf = -jnp.inf
    cl = jnp.where(lane < N_GROUPS, logits, ninf)
    cmax = jnp.max(cl, axis=1, keepdims=True)
    grp = _first_index(cl == cmax, lane)
    pg = 1.0 / jnp.sum(jnp.exp(cl - cmax), axis=1, keepdims=True)
    lo = N_GROUPS + EXPERTS_PER_GROUP * grp
    fl = jnp.where((lane >= lo) & (lane < lo + EXPERTS_PER_GROUP), logits, ninf)
    v1 = jnp.max(fl, axis=1, keepdims=True)
    i1 = _first_index(fl == v1, lane)
    fl2 = jnp.where(lane == i1, ninf, fl)
    v2 = jnp.max(fl2, axis=1, keepdims=True)
    i2 = _first_index(fl2 == v2, lane)
    e21 = jnp.exp(v2 - v1)
    t1 = pg / (1.0 + e21)
    t2 = t1 * e21
    ri_ref[...] = jnp.where(lane == 0, i1 - N_GROUPS, jnp.where(lane == 1, i2 - N_GROUPS, 0))
    rw_ref[...] = jnp.where(lane == 0, t1, jnp.where(lane == 1, t2, 0.0))


def _out_proj(x, oa, ob, ocs, lses, lw, seq):
    n = x.shape[0]
    tm = ROW_BLOCK
    nrep = seq // tm
    row = lambda w: pl.BlockSpec((tm, w), lambda i: (i, 0))
    full = lambda a: pl.BlockSpec(a.shape, lambda i: (0,) * a.ndim)
    dilated = [_dilated_spec(C_GW, dil, nrep) for _, dil in C_GROUPS]
    weights = (lw['w_out'], lw['ln1_g'], lw['ln1_b'], lw['w_router_hi'], lw['w_router_lo'])
    return pl.pallas_call(
        _out_proj_kernel,
        grid=(n // tm,),
        in_specs=[row(D_MODEL), row(A_W), row(B_W)] + dilated + dilated + [full(w) for w in weights],
        out_specs=[row(D_MODEL), pl.BlockSpec((tm * SUBLANES, LANES), lambda i: (i, 0)), row(LANES), row(LANES)],
        out_shape=[jax.ShapeDtypeStruct((n, D_MODEL), F32),
                   jax.ShapeDtypeStruct((n * SUBLANES, LANES), F32),
                   jax.ShapeDtypeStruct((n, LANES), jnp.int32),
                   jax.ShapeDtypeStruct((n, LANES), F32)],
        scratch_shapes=[pltpu.VMEM((2 * len(C_GROUPS), tm, C_GW), F32)],
        compiler_params=_params("parallel"),
        name="out_proj",
    )(x, oa, ob, *ocs, *lses, *weights)


def _store_token_tiles(ref, x):
    rows = x.shape[0]
    for j in range(SUBLANES):
        ref[pl.ds(j, rows, stride=SUBLANES), :] = x[:, LANES * j:LANES * (j + 1)]


def _load_token_tiles(ref, first, rows):
    return [ref[pl.ds(first * SUBLANES + j, rows, stride=SUBLANES), :] for j in range(SUBLANES)]


GATHER_DEPTH = 3


def _start_row_gathers(idx_ref, src_hbm, dst, sem):
    for r in range(dst.shape[0] // SUBLANES):
        first = pl.multiple_of(idx_ref[0, 0, r], SUBLANES)
        pltpu.make_async_copy(src_hbm.at[pl.ds(first, SUBLANES)],
                              dst.at[pl.ds(r * SUBLANES, SUBLANES)], sem).start()


def _prefetched_rows(idx_refs, src_hbm, buf, sem_g):
    i = pl.program_id(0)
    n = pl.num_programs(0)
    slot = lax.rem(i, GATHER_DEPTH)
    tokens = buf.shape[1] // SUBLANES
    ahead = GATHER_DEPTH - 1

    @pl.when(i == 0)
    def _():
        for d in range(ahead):
            @pl.when(d < n)
            def _():
                _start_row_gathers(idx_refs[d], src_hbm, buf.at[d], sem_g.at[d])

    @pl.when(i + ahead < n)
    def _():
        nxt = lax.rem(i + ahead, GATHER_DEPTH)
        _start_row_gathers(idx_refs[ahead], src_hbm, buf.at[nxt], sem_g.at[nxt])

    def drain(r, carry):
        pltpu.make_async_copy(src_hbm.at[pl.ds(0, SUBLANES)], buf.at[slot, pl.ds(0, SUBLANES)],
                              sem_g.at[slot]).wait()
        return carry

    lax.fori_loop(0, tokens, drain, 0, unroll=8)
    return slot


def _index_specs(rows, steps, prefetch_args=0):
    def spec(d):
        if prefetch_args:
            return pl.BlockSpec((1, 1, rows), lambda i, be: (jnp.minimum(i + d, steps - 1), 0, 0))
        return pl.BlockSpec((1, 1, rows), lambda i: (jnp.minimum(i + d, steps - 1), 0, 0))
    return [spec(d) for d in range(GATHER_DEPTH)]


def _gather_scratch(rows):
    return [pltpu.VMEM((GATHER_DEPTH, rows * SUBLANES, LANES), F32), pltpu.SemaphoreType.DMA((GATHER_DEPTH,))]


def _expert_kernel(be_ref, *refs):
    del be_ref
    tok_refs, (x_hbm, w1_ref, w3_ref, w2_ref, y_ref, xg, sem_g) = refs[:GATHER_DEPTH], refs[GATHER_DEPTH:]
    slot = _prefetched_rows(tok_refs, x_hbm, xg, sem_g)
    tb = y_ref.shape[0] // SUBLANES
    xb = jnp.concatenate([c.astype(BF16) for c in _load_token_tiles(xg.at[slot], 0, tb)], axis=1)
    h1 = _dot(xb, w1_ref[...])
    h3 = _dot(xb, w3_ref[...])
    hid = (h1 * (1.0 / (1.0 + jnp.exp(-h1))) * h3).astype(BF16)
    _store_token_tiles(y_ref, _dot(hid, w2_ref[...]))


def _experts(blk_exp, slot_tok, x1t, lw):
    n_blocks = blk_exp.shape[0]
    tb = MOE_ROWS
    tok = (slot_tok * SUBLANES).reshape(n_blocks, 1, tb)
    grid_spec = pltpu.PrefetchScalarGridSpec(
        num_scalar_prefetch=1,
        grid=(n_blocks,),
        in_specs=_index_specs(tb, n_blocks, prefetch_args=1) + [
            pl.BlockSpec(memory_space=pl.ANY),
            pl.BlockSpec((None, D_MODEL, D_EXPERT), lambda i, be: (be[i], 0, 0)),
            pl.BlockSpec((None, D_MODEL, D_EXPERT), lambda i, be: (be[i], 0, 0)),
            pl.BlockSpec((None, D_EXPERT, D_MODEL), lambda i, be: (be[i], 0, 0))],
        out_specs=pl.BlockSpec((tb * SUBLANES, LANES), lambda i, be: (i, 0)),
        scratch_shapes=_gather_scratch(tb))
    return pl.pallas_call(
        _expert_kernel,
        grid_spec=grid_spec,
        out_shape=jax.ShapeDtypeStruct((n_blocks * tb * SUBLANES, LANES), F32),
        compiler_params=_params("arbitrary"),
        name="experts",
    )(blk_exp, *([tok] * GATHER_DEPTH), x1t, lw['w1'], lw['w3'], lw['w2'])


def _combine_kernel(*refs):
    pos_refs, (x1_ref, rw_ref, ys_hbm, g_ref, b_ref, out_ref, yg, sem_g) = refs[:GATHER_DEPTH], refs[GATHER_DEPTH:]
    tm = x1_ref.shape[0]
    slot = _prefetched_rows(pos_refs, ys_hbm, yg, sem_g)
    rw = rw_ref[...]
    w0, w1 = rw[:, 0:1], rw[:, 1:2]
    y0 = _load_token_tiles(yg.at[slot], 0, tm)
    y1 = _load_token_tiles(yg.at[slot], tm, tm)
    y = jnp.concatenate([w0 * a + w1 * b for a, b in zip(y0, y1)], axis=1)
    out_ref[...] = _layer_norm(DN_ALPHA * x1_ref[...] + y, g_ref[...], b_ref[...])


def _combine(pos, x1, rw, ys, lw):
    n = x1.shape[0]
    tm = ROW_BLOCK
    nb = n // tm
    rows = TOP_K_INNER * tm
    pos_blk = (pos * SUBLANES).reshape(nb, tm, TOP_K_INNER).transpose(0, 2, 1).reshape(nb, 1, rows)
    full = lambda a: pl.BlockSpec(a.shape, lambda i: (0,) * a.ndim)
    return pl.pallas_call(
        _combine_kernel,
        grid=(nb,),
        in_specs=_index_specs(rows, nb) + [
            pl.BlockSpec((tm, D_MODEL), lambda i: (i, 0)),
            pl.BlockSpec((tm, LANES), lambda i: (i, 0)),
            pl.BlockSpec(memory_space=pl.ANY),
            full(lw['ln2_g']), full(lw['ln2_b'])],
        out_specs=pl.BlockSpec((tm, D_MODEL), lambda i: (i, 0)),
        out_shape=jax.ShapeDtypeStruct((n, D_MODEL), F32),
        scratch_shapes=_gather_scratch(rows),
        compiler_params=_params("arbitrary"),
        name="combine",
    )(*([pos_blk] * GATHER_DEPTH), x1, rw, ys, lw['ln2_g'], lw['ln2_b'])


def _dispatch_plan(eid):
    n = eid.shape[0]
    a = n * TOP_K_INNER
    tb = MOE_ROWS
    e_flat = eid.reshape(a)
    order = jnp.argsort(e_flat).astype(jnp.int32)
    experts = jnp.arange(N_EXPERTS, dtype=jnp.int32)
    counts = jnp.sum((e_flat[:, None] == experts[None, :]).astype(jnp.int32), axis=0)
    padded = (counts + tb - 1) // tb * tb
    pad_end = jnp.cumsum(padded)
    pad_start = pad_end - padded
    start = jnp.cumsum(counts) - counts
    shift = pad_start - start
    n_blocks = -(-(a + N_EXPERTS * (tb - 1)) // tb)
    blk_start = jnp.arange(n_blocks, dtype=jnp.int32) * tb
    blk_exp = jnp.minimum(jnp.sum((blk_start[:, None] >= pad_end[None, :]).astype(jnp.int32), axis=1),
                          N_EXPERTS - 1)
    e_slot = jnp.repeat(blk_exp, tb)
    sorted_pos = jnp.arange(n_blocks * tb, dtype=jnp.int32) - shift[e_slot]
    valid = sorted_pos < (start + counts)[e_slot]
    slot_tok = jnp.where(valid, order[jnp.clip(sorted_pos, 0, a - 1)] // TOP_K_INNER, 0)
    dest = shift[e_flat[order]] + jnp.arange(a, dtype=jnp.int32)
    pos = dest[jnp.argsort(order)].reshape(n, TOP_K_INNER)
    return blk_exp, slot_tok, pos


def _rope_tables(seq, dim):
    inv_freq = 1.0 / (ROPE_THETA ** (jnp.arange(0, dim, 2, dtype=F32) / dim))
    ang = jnp.arange(seq, dtype=F32)[:, None] * inv_freq[None, :]
    return jnp.cos(ang), jnp.sin(ang)


def _rope_table_block(seq):
    c16, s16 = _rope_tables(seq, A_HALF)
    c32, s32 = _rope_tables(seq, HEAD_DIM)
    ones = lambda w: jnp.ones((seq, w), F32)
    zeros = lambda w: jnp.zeros((seq, w), F32)
    cos_a = jnp.tile(jnp.concatenate([c16, c16], 1), (1, 2 * A_HEADS))
    sin_a = jnp.tile(jnp.concatenate([-s16, s16], 1), (1, 2 * A_HEADS))
    cos_b = jnp.concatenate([ones(B_NOPE), c16, c16, ones(B_SLOT - B_NOPE - B_ROPE)], 1)
    sin_b = jnp.concatenate([zeros(B_NOPE), -s16, s16, zeros(B_SLOT - B_NOPE - B_ROPE)], 1)
    cos_c = jnp.tile(jnp.concatenate([c32, c32], 1), (1, C_HPG))
    sin_c = jnp.tile(jnp.concatenate([-s32, s32], 1), (1, C_HPG))
    return jnp.concatenate([cos_a, sin_a, cos_b, sin_b, cos_c, sin_c], 1)


def _layer_weights(l, w_in, diff_lambda, diff_subln, mla_q_norm, mla_w_uq, mla_kv_norm, mla_w_ukv, w_out,
                   ln1_g, ln1_b, moe_w_coarse, moe_w_fine, w1, w3, w2, ln2_g, ln2_b):
    wi = w_in[l]
    zc = lambda rows, w: jnp.zeros((rows, w), F32)
    b0 = COL_A
    w_b = jnp.concatenate([wi[:, b0:b0 + B_Q_RANK + B_KV_RANK], zc(D_MODEL, B_NOPE),
                           wi[:, b0 + B_Q_RANK + B_KV_RANK:b0 + COL_B], zc(D_MODEL, B_SLOT - B_NOPE - B_ROPE)], 1)
    c0 = COL_A + COL_B
    cw = C_HPG * HEAD_DIM * len(C_GROUPS)
    w_c = jnp.concatenate([wi[:, c0 + part * cw + g * C_GW:c0 + part * cw + (g + 1) * C_GW]
                           for g in range(len(C_GROUPS)) for part in range(3)], 1)
    qd = B_NOPE + B_ROPE
    w_uq = jnp.concatenate([jnp.concatenate([mla_w_uq[l][:, h * qd:(h + 1) * qd], zc(B_Q_RANK, B_SLOT - qd)], 1)
                            for h in range(B_HEADS)], 1)
    kvd = B_NOPE + B_V
    w_uk = jnp.concatenate([jnp.concatenate([mla_w_ukv[l][:, h * kvd:h * kvd + B_NOPE],
                                             zc(B_KV_RANK, B_SLOT - B_NOPE)], 1) for h in range(B_HEADS)], 1)
    w_uv = jnp.concatenate([mla_w_ukv[l][:, h * kvd + B_NOPE:(h + 1) * kvd] for h in range(B_HEADS)], 1)
    w_router = jnp.concatenate(
        [moe_w_coarse[l]] + [moe_w_fine[l][g] for g in range(N_GROUPS)]
        + [zc(D_MODEL, LANES - N_GROUPS - N_EXPERTS)], 1)
    lam_init = 0.8 - 0.6 * math.exp(-0.3 * l)
    lv = diff_lambda[l].astype(F32)
    lam = jnp.exp(jnp.sum(lv[0] * lv[1])) - jnp.exp(jnp.sum(lv[2] * lv[3])) + lam_init
    return dict(
        w_a=wi[:, 0:COL_A].astype(BF16), w_b=w_b.astype(BF16), w_c=w_c.astype(BF16),
        w_uq=w_uq.astype(BF16), w_uk=w_uk.astype(BF16), w_uv=w_uv.astype(BF16),
        q_norm=mla_q_norm[l].reshape(1, B_Q_RANK), kv_norm=mla_kv_norm[l].reshape(1, B_KV_RANK),
        diff_sc=jnp.stack([lam, jnp.asarray(1.0 - lam_init, F32)]).astype(F32),
        subln=jnp.tile(diff_subln[l], A_HEADS).reshape(1, A_W),
        w_out=w_out[l].astype(BF16), ln1_g=ln1_g[l].reshape(1, D_MODEL), ln1_b=ln1_b[l].reshape(1, D_MODEL),
        w_router_hi=w_router.astype(BF16),
        w_router_lo=(w_router - w_router.astype(BF16).astype(F32)).astype(BF16),
        w1=w1[l], w3=w3[l], w2=w2[l],
        ln2_g=ln2_g[l].reshape(1, D_MODEL), ln2_b=ln2_b[l].reshape(1, D_MODEL))


def _encoder_layer(x, tab, lw, batch, seq):
    qa, ka, va, qb, kb, vb, c0, c1, c2 = _in_proj(x, tab, lw, batch, seq)
    oa = _attn_a(lw['diff_sc'], qa, ka, va, lw['subln'], batch, seq)
    ob = _attn_b(qb, kb, vb, batch, seq)
    ocs, lses = zip(*[_attn_c(c, batch, seq, dil) for c, (_, dil) in zip((c0, c1, c2), C_GROUPS)])
    x1, x1t, ri, rw = _out_proj(x, oa, ob, ocs, lses, lw, seq)
    blk_exp, slot_tok, pos = _dispatch_plan(ri[:, 0:TOP_K_INNER])
    ys = _experts(blk_exp, slot_tok, x1t, lw)
    return _combine(pos, x1, rw, ys, lw)


def kernel(x_prompt, x_sample, w_in, diff_lambda, diff_subln, mla_q_norm, mla_w_uq, mla_kv_norm, mla_w_ukv,
           w_out, ln1_g, ln1_b, moe_w_coarse, moe_w_fine, moe_w1, moe_w3, moe_w2, ln2_g, ln2_b):
    w1, w3, w2 = moe_w1.astype(BF16), moe_w3.astype(BF16), moe_w2.astype(BF16)
    layers = [_layer_weights(l, w_in, diff_lambda, diff_subln, mla_q_norm, mla_w_uq, mla_kv_norm, mla_w_ukv,
                             w_out, ln1_g, ln1_b, moe_w_coarse, moe_w_fine, w1, w3, w2, ln2_g, ln2_b)
              for l in range(DEPTH)]

    def trunk(x):
        batch, seq, _ = x.shape
        tab = _rope_table_block(seq)
        h = x.reshape(batch * seq, D_MODEL)
        for lw in layers:
            h = _encoder_layer(h, tab, lw, batch, seq)
        return h.reshape(batch, seq, D_MODEL)

    return (trunk(x_prompt), trunk(x_sample))
```

```python
import functools
import math

import jax
import jax.numpy as jnp
from jax import lax
from jax.experimental import pallas as pl
from jax.experimental.pallas import tpu as pltpu

D_MODEL = 1024
DEPTH = 4
HEAD_DIM = 64
ROPE_THETA = 10000.0
LN_EPS = 1e-5
RMS_EPS = 1e-6
NEG_INF = -1e30

A_HEADS = 4
A_HALF = HEAD_DIM // 2
A_W = A_HEADS * HEAD_DIM

B_HEADS = 6
B_Q_RANK = 256
B_KV_RANK = 128
B_NOPE = 64
B_ROPE = 32
B_V = 64
B_W = B_HEADS * B_V
B_SLOT = 128
B_VT_ROWS = B_V + 16

C_GROUPS = ((128, 1), (512, 4), (2048, 16))
C_HPG = 2
C_GW = C_HPG * HEAD_DIM
C_SIDE = 64
C_BLOCK = 128
C_QUERY_ROWS = 512

COL_A = 3 * A_W
COL_B = B_Q_RANK + B_KV_RANK + B_ROPE

N_GROUPS = 4
EXPERTS_PER_GROUP = 8
N_EXPERTS = N_GROUPS * EXPERTS_PER_GROUP
TOP_K_INNER = 2
D_EXPERT = 512

DN_ALPHA = (2 * DEPTH) ** 0.25

LOG2E = 1.4426950408889634
SCALE_A = (A_HALF ** -0.5) * LOG2E
SCALE_B = ((B_NOPE + B_ROPE) ** -0.5) * LOG2E
SCALE_C = (HEAD_DIM ** -0.5) * LOG2E

LANES = 128
SUBLANES = 8
ROW_BLOCK = 256
Q_BLOCK = 256
K_CHUNK = 256
MOE_ROWS = 256
VMEM_LIMIT = 48 * 1024 * 1024
ATTN_VMEM_LIMIT = 56 * 1024 * 1024

BF16 = jnp.bfloat16
F32 = jnp.float32

_NT = (((1,), (1,)), ((), ()))


def _dot(a, b):
    return jnp.dot(a, b, preferred_element_type=F32)


def _dot_nt(a, b):
    return lax.dot_general(a, b, _NT, preferred_element_type=F32)


def _params(*sem, vmem=VMEM_LIMIT):
    return pltpu.CompilerParams(dimension_semantics=sem, vmem_limit_bytes=vmem)


def _rope(h, cos, sin_signed, half):
    width = h.shape[1]
    lane = lax.broadcasted_iota(jnp.int32, h.shape, 1)
    first = (lane % (2 * half)) < half
    partner = jnp.where(first, pltpu.roll(h, width - half, 1), pltpu.roll(h, half, 1))
    return h * cos + partner * sin_signed


def _rms(x, g):
    return x * lax.rsqrt(jnp.mean(x * x, axis=1, keepdims=True) + RMS_EPS) * g


def _in_proj_kernel(x_ref, tab_ref, wa_ref, wb_ref, wc_ref, wuq_ref, wuk_ref, wuv_ref, qn_ref, kvn_ref,
                    qa_ref, ka_ref, va_ref, qb_ref, kb_ref, vb_ref, c0_ref, c1_ref, c2_ref, cs_scr):
    xb = x_ref[...].astype(BF16)
    cos_a, sin_a = tab_ref[:, 0:256], tab_ref[:, 256:512]
    cos_b, sin_b = tab_ref[:, 512:640], tab_ref[:, 640:768]
    cos_c, sin_c = tab_ref[:, 768:896], tab_ref[:, 896:1024]

    ha = _dot(xb, wa_ref[...])
    qa_ref[...] = (_rope(ha[:, 0:A_W], cos_a, sin_a, A_HALF // 2) * SCALE_A).astype(BF16)
    ka_ref[...] = _rope(ha[:, A_W:2 * A_W], cos_a, sin_a, A_HALF // 2).astype(BF16)
    va_ref[...] = ha[:, 2 * A_W:3 * A_W].astype(BF16)

    hb = _dot(xb, wb_ref[...])
    cq = _rms(hb[:, 0:B_Q_RANK], qn_ref[...]).astype(BF16)
    ckv = _rms(hb[:, B_Q_RANK:B_Q_RANK + B_KV_RANK], kvn_ref[...]).astype(BF16)
    qb = _dot(cq, wuq_ref[...])
    kb = _dot(ckv, wuk_ref[...])
    vb_ref[...] = _dot(ckv, wuv_ref[...]).astype(BF16)
    k_rope = _rope(hb[:, 384:512], cos_b, sin_b, B_ROPE // 2)
    for h in range(B_HEADS):
        sl = slice(B_SLOT * h, B_SLOT * (h + 1))
        qb_ref[:, sl] = (_rope(qb[:, sl], cos_b, sin_b, B_ROPE // 2) * SCALE_B).astype(BF16)
        kb_ref[:, sl] = (kb[:, sl] + k_rope).astype(BF16)

    hc = _dot(xb, wc_ref[...])
    tm = hc.shape[0]
    for g, c_ref in enumerate((c0_ref, c1_ref, c2_ref)):
        base = 3 * C_GW * g
        dil = C_GROUPS[g][1]
        qkv = [_rope(hc[:, base:base + C_GW], cos_c, sin_c, HEAD_DIM // 2) * SCALE_C,
               _rope(hc[:, base + C_GW:base + 2 * C_GW], cos_c, sin_c, HEAD_DIM // 2),
               hc[:, base + 2 * C_GW:base + 3 * C_GW]]
        for part in range(3):
            cols = slice(C_GW * part, C_GW * (part + 1))
            if dil == 1:
                c_ref[0, :, cols] = qkv[part].astype(BF16)
            else:
                cs_scr[part] = qkv[part]
                for j in range(dil):
                    c_ref[j, :, cols] = cs_scr[part, pl.ds(j, tm // dil, stride=dil), :].astype(BF16)


def _dilated_spec(width, dil, nrep):
    return pl.BlockSpec((None, dil, ROW_BLOCK // dil, width), lambda i: (i // nrep, 0, i % nrep, 0))


def _in_proj(x, tab, lw, batch, seq):
    n = x.shape[0]
    tm = ROW_BLOCK
    nrep = seq // tm
    row = lambda w: pl.BlockSpec((tm, w), lambda i: (i, 0))
    full = lambda a: pl.BlockSpec(a.shape, lambda i: (0,) * a.ndim)
    weights = (lw['w_a'], lw['w_b'], lw['w_c'], lw['w_uq'], lw['w_uk'], lw['w_uv'], lw['q_norm'], lw['kv_norm'])
    out_w = (A_W, A_W, A_W, B_HEADS * B_SLOT, B_HEADS * B_SLOT, B_W)
    return pl.pallas_call(
        _in_proj_kernel,
        grid=(n // tm,),
        in_specs=[row(D_MODEL), pl.BlockSpec((tm, 1024), lambda i: (i % nrep, 0))] + [full(w) for w in weights],
        out_specs=[row(w) for w in out_w] + [_dilated_spec(3 * C_GW, dil, nrep) for _, dil in C_GROUPS],
        out_shape=[jax.ShapeDtypeStruct((n, w), BF16) for w in out_w]
        + [jax.ShapeDtypeStruct((batch, dil, seq // dil, 3 * C_GW), BF16) for _, dil in C_GROUPS],
        scratch_shapes=[pltpu.VMEM((3, tm, C_GW), F32)],
        compiler_params=_params("parallel"),
        name="in_proj",
    )(x, tab, *weights)


def _fold_rows(x, op):
    r, c = x.shape
    return op(x.reshape(r // SUBLANES, SUBLANES, c), axis=0)


def _chunk_start(kc):
    return pl.multiple_of(kc * K_CHUNK, K_CHUNK)


def _stage_v_transposed(v_ref, vt_scr):
    @pl.when(pl.program_id(1) == 0)
    def _():
        def body(kc, carry):
            vt_scr[kc] = v_ref[pl.ds(_chunk_start(kc), K_CHUNK), :].astype(F32).T.astype(BF16)
            return carry
        lax.fori_loop(0, vt_scr.shape[0], body, 0)


def _attn_a_kernel(sc_ref, q_ref, k_ref, v_ref, g_ref, o_ref, vt_scr, s_scr):
    lam = sc_ref[0]
    post = sc_ref[1]
    tq = q_ref.shape[0]
    nkc = k_ref.shape[0] // K_CHUNK
    _stage_v_transposed(v_ref, vt_scr)
    qt = q_ref[...].astype(F32).T
    row = lax.broadcasted_iota(jnp.int32, (A_W, tq), 0)
    neg = jnp.full((SUBLANES, tq), -jnp.inf, F32)
    zero = jnp.zeros((SUBLANES, tq), F32)
    maxes, gammas, inv_l1, heads = {}, {}, {}, []
    for st in range(A_HEADS + 2):
        hs, hp, hw = st, st - 1, st - 2
        do_s, do_p, do_w = hs < A_HEADS, 0 <= hp < A_HEADS, 0 <= hw < A_HEADS
        qts = []
        if do_s:
            for c in range(2):
                lo = A_HALF * (2 * hs + c)
                qts.append(jnp.where((row >= lo) & (row < lo + A_HALF), qt, 0.0).astype(BF16))

        def body(kc, carry, qts=qts, hs=hs, hp=hp, hw=hw, do_s=do_s, do_p=do_p, do_w=do_w):
            m1, m2, l1, l2, acc = carry
            if do_s:
                kr = k_ref[pl.ds(_chunk_start(kc), K_CHUNK), :]
                s1 = _dot(kr, qts[0])
                s2 = _dot(kr, qts[1])
                s_scr[hs % 3, 0, kc] = s1
                s_scr[hs % 3, 1, kc] = s2
                m1 = jnp.maximum(m1, _fold_rows(s1, jnp.max))
                m2 = jnp.maximum(m2, _fold_rows(s2, jnp.max))
            if do_p:
                p1 = jnp.exp2(s_scr[hp % 3, 0, kc] - maxes[hp][0])
                p2 = jnp.exp2(s_scr[hp % 3, 1, kc] - maxes[hp][1])
                s_scr[hp % 3, 0, kc] = p1
                s_scr[hp % 3, 1, kc] = p2
                l1 = l1 + _fold_rows(p1, jnp.sum)
                l2 = l2 + _fold_rows(p2, jnp.sum)
            if do_w:
                w = (s_scr[hw % 3, 0, kc] - gammas[hw] * s_scr[hw % 3, 1, kc]).astype(BF16)
                acc = acc + _dot(vt_scr[kc, HEAD_DIM * hw:HEAD_DIM * (hw + 1), :], w)
            return m1, m2, l1, l2, acc

        m1, m2, l1, l2, acc = lax.fori_loop(
            0, nkc, body, (neg, neg, zero, zero, jnp.zeros((HEAD_DIM, tq), F32)), unroll=True)
        if do_s:
            maxes[hs] = (jnp.max(m1, axis=0, keepdims=True), jnp.max(m2, axis=0, keepdims=True))
        if do_p:
            l1 = jnp.sum(l1, axis=0, keepdims=True)
            l2 = jnp.sum(l2, axis=0, keepdims=True)
            gammas[hp] = lam * l1 / l2
            inv_l1[hp] = 1.0 / l1
        if do_w:
            oh = acc * inv_l1[hw]
            ms = jnp.mean(oh * oh, axis=0, keepdims=True)
            heads.append(oh * lax.rsqrt(ms + RMS_EPS))
    out_t = jnp.concatenate(heads, axis=0)
    o_ref[...] = (out_t.T * g_ref[...] * post).astype(BF16)


def _attn_a(sc, q, k, v, g, batch, seq):
    n = q.shape[0]
    tq = Q_BLOCK
    nq = seq // tq
    nkc = seq // K_CHUNK
    return pl.pallas_call(
        _attn_a_kernel,
        grid=(batch, nq),
        in_specs=[pl.BlockSpec(memory_space=pltpu.SMEM),
                  pl.BlockSpec((tq, A_W), lambda b, i: (b * nq + i, 0)),
                  pl.BlockSpec((seq, A_W), lambda b, i: (b, 0)),
                  pl.BlockSpec((seq, A_W), lambda b, i: (b, 0)),
                  pl.BlockSpec((1, A_W), lambda b, i: (0, 0))],
        out_specs=pl.BlockSpec((tq, A_W), lambda b, i: (b * nq + i, 0)),
        out_shape=jax.ShapeDtypeStruct((n, A_W), BF16),
        scratch_shapes=[pltpu.VMEM((nkc, A_W, K_CHUNK), BF16),
                        pltpu.VMEM((3, 2, nkc, K_CHUNK, tq), F32)],
        compiler_params=_params("arbitrary", "arbitrary", vmem=ATTN_VMEM_LIMIT),
        name="attn_a",
    )(sc, q, k, v, g)


def _attn_b_kernel(q_ref, k_ref, v_ref, o_ref, vt_scr, s_scr):
    tq = q_ref.shape[0]
    nkc = k_ref.shape[0] // K_CHUNK

    @pl.when(pl.program_id(1) == 0)
    def _():
        ones = jnp.ones((B_VT_ROWS - B_V, K_CHUNK), BF16)

        def stage(kc, carry):
            vt = v_ref[pl.ds(_chunk_start(kc), K_CHUNK), :].astype(F32).T.astype(BF16)
            for h in range(B_HEADS):
                vt_scr[kc, B_VT_ROWS * h:B_VT_ROWS * h + B_V, :] = vt[B_V * h:B_V * (h + 1), :]
                vt_scr[kc, B_VT_ROWS * h + B_V:B_VT_ROWS * (h + 1), :] = ones
            return carry
        lax.fori_loop(0, nkc, stage, 0)

    neg = jnp.full((SUBLANES, tq), -jnp.inf, F32)
    acc0 = jnp.zeros((B_VT_ROWS, tq), F32)
    n_pairs = B_HEADS // 2
    maxes, heads = {}, []
    for st in range(n_pairs + 1):
        ps, pw = st, st - 1
        do_s, do_w = ps < n_pairs, pw >= 0
        qts = []
        if do_s:
            for j in range(2):
                sl = slice(B_SLOT * (2 * ps + j), B_SLOT * (2 * ps + j + 1))
                qts.append(q_ref[:, sl].astype(F32).T.astype(BF16))

        def body(kc, carry, qts=qts, ps=ps, pw=pw, do_s=do_s, do_w=do_w):
            ms, accs = list(carry[0]), list(carry[1])
            for j in range(2):
                if do_s:
                    h = 2 * ps + j
                    s = _dot(k_ref[pl.ds(_chunk_start(kc), K_CHUNK), B_SLOT * h:B_SLOT * (h + 1)], qts[j])
                    s_scr[ps % 2, j, kc] = s
                    ms[j] = jnp.maximum(ms[j], _fold_rows(s, jnp.max))
                if do_w:
                    h = 2 * pw + j
                    p = jnp.exp2(s_scr[pw % 2, j, kc] - maxes[h]).astype(BF16)
                    accs[j] = accs[j] + _dot(vt_scr[kc, B_VT_ROWS * h:B_VT_ROWS * (h + 1), :], p)
            return tuple(ms), tuple(accs)

        ms, accs = lax.fori_loop(0, nkc, body, ((neg, neg), (acc0, acc0)), unroll=8)
        for j in range(2):
            if do_s:
                maxes[2 * ps + j] = jnp.max(ms[j], axis=0, keepdims=True)
            if do_w:
                heads.append(accs[j][0:B_V] * (1.0 / accs[j][B_V:B_V + 1]))
    o_ref[...] = jnp.concatenate(heads, axis=0).T.astype(BF16)


def _attn_b(q, k, v, batch, seq):
    n = q.shape[0]
    tq = Q_BLOCK
    nq = seq // tq
    nkc = seq // K_CHUNK
    wq = B_HEADS * B_SLOT
    return pl.pallas_call(
        _attn_b_kernel,
        grid=(batch, nq),
        in_specs=[pl.BlockSpec((tq, wq), lambda b, i: (b * nq + i, 0)),
                  pl.BlockSpec((seq, wq), lambda b, i: (b, 0)),
                  pl.BlockSpec((seq, B_W), lambda b, i: (b, 0))],
        out_specs=pl.BlockSpec((tq, B_W), lambda b, i: (b * nq + i, 0)),
        out_shape=jax.ShapeDtypeStruct((n, B_W), BF16),
        scratch_shapes=[pltpu.VMEM((nkc, B_HEADS * B_VT_ROWS, K_CHUNK), BF16),
                        pltpu.VMEM((2, 2, nkc, K_CHUNK, tq), F32)],
        compiler_params=_params("arbitrary", "arbitrary", vmem=ATTN_VMEM_LIMIT),
        name="attn_b",
    )(q, k, v)


def _attn_c_kernel(own_ref, prev_ref, next_ref, o_ref, lse_ref, *, sub_len):
    i = pl.program_id(2)
    n_classes, tq, _ = own_ref.shape
    tb = C_BLOCK
    win = 2 * tb
    lane = lax.broadcasted_iota(jnp.int32, (tb, C_GW), 1)
    key = lax.broadcasted_iota(jnp.int32, (tb, win), 1)
    band = jnp.abs(key - C_SIDE - lax.broadcasted_iota(jnp.int32, (tb, win), 0)) <= C_SIDE
    for c in range(n_classes):
        kv = jnp.concatenate([prev_ref[c, :, C_GW:], own_ref[c, :, C_GW:], next_ref[c, :, C_GW:]], axis=0)
        for u in range(tq // tb):
            q = own_ref[c, tb * u:tb * (u + 1), 0:C_GW]
            kvw = kv[tb * u + C_SIDE:tb * u + C_SIDE + win]
            kpos = i * tq + tb * u - C_SIDE + key
            valid = band & (kpos >= 0) & (kpos < sub_len)
            o = jnp.zeros((tb, C_GW), F32)
            lse = jnp.zeros((tb, C_GW), F32)
            for hh in range(C_HPG):
                head = (lane >= HEAD_DIM * hh) & (lane < HEAD_DIM * (hh + 1))
                qm = jnp.where(head, q, jnp.zeros_like(q))
                s = jnp.where(valid, _dot_nt(qm, kvw[:, 0:C_GW]), NEG_INF)
                m = jnp.max(s, axis=1, keepdims=True)
                p = jnp.exp2(s - m)
                l = jnp.sum(p, axis=1, keepdims=True)
                oh = _dot(p.astype(BF16), kvw[:, C_GW:2 * C_GW]) * (1.0 / l)
                o = jnp.where(head, oh, o)
                lse = jnp.where(head, m + jnp.log(l) * LOG2E, lse)
            o_ref[c, tb * u:tb * (u + 1), :] = o
            lse_ref[c, tb * u:tb * (u + 1), :] = lse


def _attn_c(c, batch, seq, dil):
    sub_len = seq // dil
    tq = min(C_QUERY_ROWS, sub_len)
    nc = min(dil, C_QUERY_ROWS // tq)
    nsub = tq // C_BLOCK
    last = sub_len // C_BLOCK - 1
    o, lse = pl.pallas_call(
        functools.partial(_attn_c_kernel, sub_len=sub_len),
        grid=(batch, dil // nc, sub_len // tq),
        in_specs=[pl.BlockSpec((None, nc, tq, 3 * C_GW), lambda b, j, i: (b, j, i, 0)),
                  pl.BlockSpec((None, nc, C_BLOCK, 3 * C_GW),
                               lambda b, j, i: (b, j, jnp.maximum(i * nsub - 1, 0), 0)),
                  pl.BlockSpec((None, nc, C_BLOCK, 3 * C_GW),
                               lambda b, j, i: (b, j, jnp.minimum((i + 1) * nsub, last), 0))],
        out_specs=[pl.BlockSpec((None, nc, tq, C_GW), lambda b, j, i: (b, j, i, 0))] * 2,
        out_shape=[jax.ShapeDtypeStruct((batch, dil, sub_len, C_GW), F32)] * 2,
        compiler_params=_params("parallel", "parallel", "parallel"),
        name=f"attn_c_d{dil}",
    )(c, c, c)
    return o, lse


def _layer_norm(z, g, b):
    mu = jnp.mean(z, axis=1, keepdims=True)
    zc = z - mu
    var = jnp.mean(zc * zc, axis=1, keepdims=True)
    return zc * lax.rsqrt(var + LN_EPS) * g + b


def _first_index(hit, lane):
    return jnp.min(jnp.where(hit, lane, LANES), axis=1, keepdims=True)


def _token_order(ref, scr):
    dil, sub, _ = ref.shape
    if dil == 1:
        return ref[0]
    for j in range(dil):
        scr[pl.ds(j, sub, stride=dil), :] = ref[j]
    return scr[...]


def _out_proj_kernel(x_ref, oa_ref, ob_ref, oc0_ref, oc1_ref, oc2_ref, l0_ref, l1_ref, l2_ref,
                     wout_ref, g_ref, b_ref, wrh_ref, wrl_ref, x1_ref, x1t_ref, ri_ref, rw_ref, ord_scr):
    oc = [_token_order(r, ord_scr.at[k]) for k, r in enumerate((oc0_ref, oc1_ref, oc2_ref))]
    la, lb, lc = [_token_order(r, ord_scr.at[3 + k]) for k, r in enumerate((l0_ref, l1_ref, l2_ref))]
    mx = jnp.maximum(jnp.maximum(la, lb), lc)
    ea, eb, ec = jnp.exp2(la - mx), jnp.exp2(lb - mx), jnp.exp2(lc - mx)
    inv = 1.0 / (ea + eb + ec)
    mix = jnp.concatenate(
        [oa_ref[...], ob_ref[...],
         (oc[0] * (ea * inv)).astype(BF16),
         (oc[1] * (eb * inv)).astype(BF16),
         (oc[2] * (ec * inv)).astype(BF16)], axis=1)
    x1 = _layer_norm(DN_ALPHA * x_ref[...] + _dot(mix, wout_ref[...]), g_ref[...], b_ref[...])
    x1_ref[...] = x1
    _store_token_tiles(x1t_ref, x1)

    x_hi = x1.astype(BF16)
    x_lo = (x1 - x_hi.astype(F32)).astype(BF16)
    logits = _dot(x_hi, wrh_ref[...]) + (_dot(x_lo, wrh_ref[...]) + _dot(x_hi, wrl_ref[...]))
    lane = lax.broadcasted_iota(jnp.int32, logits.shape, 1)
    ninf = -jnp.inf
    cl = jnp.where(lane < N_GROUPS, logits, ninf)
    cmax = jnp.max(cl, axis=1, keepdims=True)
    grp = _first_index(cl == cmax, lane)
    pg = 1.0 / jnp.sum(jnp.exp(cl - cmax), axis=1, keepdims=True)
    lo = N_GROUPS + EXPERTS_PER_GROUP * grp
    fl = jnp.where((lane >= lo) & (lane < lo + EXPERTS_PER_GROUP), logits, ninf)
    v1 = jnp.max(fl, axis=1, keepdims=True)
    i1 = _first_index(fl == v1, lane)
    fl2 = jnp.where(lane == i1, ninf, fl)
    v2 = jnp.max(fl2, axis=1, keepdims=True)
    i2 = _first_index(fl2 == v2, lane)
    e21 = jnp.exp(v2 - v1)
    t1 = pg / (1.0 + e21)
    t2 = t1 * e21
    ri_ref[...] = jnp.where(lane == 0, i1 - N_GROUPS, jnp.where(lane == 1, i2 - N_GROUPS, 0))
    rw_ref[...] = jnp.where(lane == 0, t1, jnp.where(lane == 1, t2, 0.0))


def _out_proj(x, oa, ob, ocs, lses, lw, seq):
    n = x.shape[0]
    tm = ROW_BLOCK
    nrep = seq // tm
    row = lambda w: pl.BlockSpec((tm, w), lambda i: (i, 0))
    full = lambda a: pl.BlockSpec(a.shape, lambda i: (0,) * a.ndim)
    dilated = [_dilated_spec(C_GW, dil, nrep) for _, dil in C_GROUPS]
    weights = (lw['w_out'], lw['ln1_g'], lw['ln1_b'], lw['w_router_hi'], lw['w_router_lo'])
    return pl.pallas_call(
        _out_proj_kernel,
        grid=(n // tm,),
        in_specs=[row(D_MODEL), row(A_W), row(B_W)] + dilated + dilated + [full(w) for w in weights],
        out_specs=[row(D_MODEL), pl.BlockSpec((tm * SUBLANES, LANES), lambda i: (i, 0)), row(LANES), row(LANES)],
        out_shape=[jax.ShapeDtypeStruct((n, D_MODEL), F32),
                   jax.ShapeDtypeStruct((n * SUBLANES, LANES), F32),
                   jax.ShapeDtypeStruct((n, LANES), jnp.int32),
                   jax.ShapeDtypeStruct((n, LANES), F32)],
        scratch_shapes=[pltpu.VMEM((2 * len(C_GROUPS), tm, C_GW), F32)],
        compiler_params=_params("parallel"),
        name="out_proj",
    )(x, oa, ob, *ocs, *lses, *weights)


def _store_token_tiles(ref, x):
    rows = x.shape[0]
    for j in range(SUBLANES):
        ref[pl.ds(j, rows, stride=SUBLANES), :] = x[:, LANES * j:LANES * (j + 1)]


def _load_token_tiles(ref, first, rows):
    return [ref[pl.ds(first * SUBLANES + j, rows, stride=SUBLANES), :] for j in range(SUBLANES)]


GATHER_DEPTH = 3


def _start_row_gathers(idx_ref, src_hbm, dst, sem):
    for r in range(dst.shape[0] // SUBLANES):
        first = pl.multiple_of(idx_ref[0, 0, r], SUBLANES)
        pltpu.make_async_copy(src_hbm.at[pl.ds(first, SUBLANES)],
                              dst.at[pl.ds(r * SUBLANES, SUBLANES)], sem).start()


def _prefetched_rows(idx_refs, src_hbm, buf, sem_g):
    i = pl.program_id(0)
    n = pl.num_programs(0)
    slot = lax.rem(i, GATHER_DEPTH)
    tokens = buf.shape[1] // SUBLANES
    ahead = GATHER_DEPTH - 1

    @pl.when(i == 0)
    def _():
        for d in range(ahead):
            @pl.when(d < n)
            def _():
                _start_row_gathers(idx_refs[d], src_hbm, buf.at[d], sem_g.at[d])

    @pl.when(i + ahead < n)
    def _():
        nxt = lax.rem(i + ahead, GATHER_DEPTH)
        _start_row_gathers(idx_refs[ahead], src_hbm, buf.at[nxt], sem_g.at[nxt])

    def drain(r, carry):
        pltpu.make_async_copy(src_hbm.at[pl.ds(0, SUBLANES)], buf.at[slot, pl.ds(0, SUBLANES)],
                              sem_g.at[slot]).wait()
        return carry

    lax.fori_loop(0, tokens, drain, 0, unroll=8)
    return slot


def _index_specs(rows, steps, prefetch_args=0):
    def spec(d):
        if prefetch_args:
            return pl.BlockSpec((1, 1, rows), lambda i, be: (jnp.minimum(i + d, steps - 1), 0, 0))
        return pl.BlockSpec((1, 1, rows), lambda i: (jnp.minimum(i + d, steps - 1), 0, 0))
    return [spec(d) for d in range(GATHER_DEPTH)]


def _gather_scratch(rows):
    return [pltpu.VMEM((GATHER_DEPTH, rows * SUBLANES, LANES), F32), pltpu.SemaphoreType.DMA((GATHER_DEPTH,))]


def _expert_kernel(be_ref, *refs):
    del be_ref
    tok_refs, (x_hbm, w1_ref, w3_ref, w2_ref, y_ref, xg, sem_g) = refs[:GATHER_DEPTH], refs[GATHER_DEPTH:]
    slot = _prefetched_rows(tok_refs, x_hbm, xg, sem_g)
    tb = y_ref.shape[0] // SUBLANES
    xb = jnp.concatenate([c.astype(BF16) for c in _load_token_tiles(xg.at[slot], 0, tb)], axis=1)
    h1 = _dot(xb, w1_ref[...])
    h3 = _dot(xb, w3_ref[...])
    hid = (h1 * (1.0 / (1.0 + jnp.exp(-h1))) * h3).astype(BF16)
    _store_token_tiles(y_ref, _dot(hid, w2_ref[...]))


def _experts(blk_exp, slot_tok, x1t, lw):
    n_blocks = blk_exp.shape[0]
    tb = MOE_ROWS
    tok = (slot_tok * SUBLANES).reshape(n_blocks, 1, tb)
    grid_spec = pltpu.PrefetchScalarGridSpec(
        num_scalar_prefetch=1,
        grid=(n_blocks,),
        in_specs=_index_specs(tb, n_blocks, prefetch_args=1) + [
            pl.BlockSpec(memory_space=pl.ANY),
            pl.BlockSpec((None, D_MODEL, D_EXPERT), lambda i, be: (be[i], 0, 0)),
            pl.BlockSpec((None, D_MODEL, D_EXPERT), lambda i, be: (be[i], 0, 0)),
            pl.BlockSpec((None, D_EXPERT, D_MODEL), lambda i, be: (be[i], 0, 0))],
        out_specs=pl.BlockSpec((tb * SUBLANES, LANES), lambda i, be: (i, 0)),
        scratch_shapes=_gather_scratch(tb))
    return pl.pallas_call(
        _expert_kernel,
        grid_spec=grid_spec,
        out_shape=jax.ShapeDtypeStruct((n_blocks * tb * SUBLANES, LANES), F32),
        compiler_params=_params("arbitrary"),
        name="experts",
    )(blk_exp, *([tok] * GATHER_DEPTH), x1t, lw['w1'], lw['w3'], lw['w2'])


def _combine_kernel(*refs):
    pos_refs, (x1_ref, rw_ref, ys_hbm, g_ref, b_ref, out_ref, yg, sem_g) = refs[:GATHER_DEPTH], refs[GATHER_DEPTH:]
    tm = x1_ref.shape[0]
    slot = _prefetched_rows(pos_refs, ys_hbm, yg, sem_g)
    rw = rw_ref[...]
    w0, w1 = rw[:, 0:1], rw[:, 1:2]
    y0 = _load_token_tiles(yg.at[slot], 0, tm)
    y1 = _load_token_tiles(yg.at[slot], tm, tm)
    y = jnp.concatenate([w0 * a + w1 * b for a, b in zip(y0, y1)], axis=1)
    out_ref[...] = _layer_norm(DN_ALPHA * x1_ref[...] + y, g_ref[...], b_ref[...])


def _combine(pos, x1, rw, ys, lw):
    n = x1.shape[0]
    tm = ROW_BLOCK
    nb = n // tm
    rows = TOP_K_INNER * tm
    pos_blk = (pos * SUBLANES).reshape(nb, tm, TOP_K_INNER).transpose(0, 2, 1).reshape(nb, 1, rows)
    full = lambda a: pl.BlockSpec(a.shape, lambda i: (0,) * a.ndim)
    return pl.pallas_call(
        _combine_kernel,
        grid=(nb,),
        in_specs=_index_specs(rows, nb) + [
            pl.BlockSpec((tm, D_MODEL), lambda i: (i, 0)),
            pl.BlockSpec((tm, LANES), lambda i: (i, 0)),
            pl.BlockSpec(memory_space=pl.ANY),
            full(lw['ln2_g']), full(lw['ln2_b'])],
        out_specs=pl.BlockSpec((tm, D_MODEL), lambda i: (i, 0)),
        out_shape=jax.ShapeDtypeStruct((n, D_MODEL), F32),
        scratch_shapes=_gather_scratch(rows),
        compiler_params=_params("arbitrary"),
        name="combine",
    )(*([pos_blk] * GATHER_DEPTH), x1, rw, ys, lw['ln2_g'], lw['ln2_b'])


def _dispatch_plan(eid):
    n = eid.shape[0]
    a = n * TOP_K_INNER
    tb = MOE_ROWS
    e_flat = eid.reshape(a)
    order = jnp.argsort(e_flat).astype(jnp.int32)
    experts = jnp.arange(N_EXPERTS, dtype=jnp.int32)
    counts = jnp.sum((e_flat[:, None] == experts[None, :]).astype(jnp.int32), axis=0)
    padded = (counts + tb - 1) // tb * tb
    pad_end = jnp.cumsum(padded)
    pad_start = pad_end - padded
    start = jnp.cumsum(counts) - counts
    shift = pad_start - start
    n_blocks = -(-(a + N_EXPERTS * (tb - 1)) // tb)
    blk_start = jnp.arange(n_blocks, dtype=jnp.int32) * tb
    blk_exp = jnp.minimum(jnp.sum((blk_start[:, None] >= pad_end[None, :]).astype(jnp.int32), axis=1),
                          N_EXPERTS - 1)
    e_slot = jnp.repeat(blk_exp, tb)
    sorted_pos = jnp.arange(n_blocks * tb, dtype=jnp.int32) - shift[e_slot]
    valid = sorted_pos < (start + counts)[e_slot]
    slot_tok = jnp.where(valid, order[jnp.clip(sorted_pos, 0, a - 1)] // TOP_K_INNER, 0)
    dest = shift[e_flat[order]] + jnp.arange(a, dtype=jnp.int32)
    pos = dest[jnp.argsort(order)].reshape(n, TOP_K_INNER)
    return blk_exp, slot_tok, pos


def _rope_tables(seq, dim):
    inv_freq = 1.0 / (ROPE_THETA ** (jnp.arange(0, dim, 2, dtype=F32) / dim))
    ang = jnp.arange(seq, dtype=F32)[:, None] * inv_freq[None, :]
    return jnp.cos(ang), jnp.sin(ang)


def _rope_table_block(seq):
    c16, s16 = _rope_tables(seq, A_HALF)
    c32, s32 = _rope_tables(seq, HEAD_DIM)
    ones = lambda w: jnp.ones((seq, w), F32)
    zeros = lambda w: jnp.zeros((seq, w), F32)
    cos_a = jnp.tile(jnp.concatenate([c16, c16], 1), (1, 2 * A_HEADS))
    sin_a = jnp.tile(jnp.concatenate([-s16, s16], 1), (1, 2 * A_HEADS))
    cos_b = jnp.concatenate([ones(B_NOPE), c16, c16, ones(B_SLOT - B_NOPE - B_ROPE)], 1)
    sin_b = jnp.concatenate([zeros(B_NOPE), -s16, s16, zeros(B_SLOT - B_NOPE - B_ROPE)], 1)
    cos_c = jnp.tile(jnp.concatenate([c32, c32], 1), (1, C_HPG))
    sin_c = jnp.tile(jnp.concatenate([-s32, s32], 1), (1, C_HPG))
    return jnp.concatenate([cos_a, sin_a, cos_b, sin_b, cos_c, sin_c], 1)


def _layer_weights(l, w_in, diff_lambda, diff_subln, mla_q_norm, mla_w_uq, mla_kv_norm, mla_w_ukv, w_out,
                   ln1_g, ln1_b, moe_w_coarse, moe_w_fine, w1, w3, w2, ln2_g, ln2_b):
    wi = w_in[l]
    zc = lambda rows, w: jnp.zeros((rows, w), F32)
    b0 = COL_A
    w_b = jnp.concatenate([wi[:, b0:b0 + B_Q_RANK + B_KV_RANK], zc(D_MODEL, B_NOPE),
                           wi[:, b0 + B_Q_RANK + B_KV_RANK:b0 + COL_B], zc(D_MODEL, B_SLOT - B_NOPE - B_ROPE)], 1)
    c0 = COL_A + COL_B
    cw = C_HPG * HEAD_DIM * len(C_GROUPS)
    w_c = jnp.concatenate([wi[:, c0 + part * cw + g * C_GW:c0 + part * cw + (g + 1) * C_GW]
                           for g in range(len(C_GROUPS)) for part in range(3)], 1)
    qd = B_NOPE + B_ROPE
    w_uq = jnp.concatenate([jnp.concatenate([mla_w_uq[l][:, h * qd:(h + 1) * qd], zc(B_Q_RANK, B_SLOT - qd)], 1)
                            for h in range(B_HEADS)], 1)
    kvd = B_NOPE + B_V
    w_uk = jnp.concatenate([jnp.concatenate([mla_w_ukv[l][:, h * kvd:h * kvd + B_NOPE],
                                             zc(B_KV_RANK, B_SLOT - B_NOPE)], 1) for h in range(B_HEADS)], 1)
    w_uv = jnp.concatenate([mla_w_ukv[l][:, h * kvd + B_NOPE:(h + 1) * kvd] for h in range(B_HEADS)], 1)
    w_router = jnp.concatenate(
        [moe_w_coarse[l]] + [moe_w_fine[l][g] for g in range(N_GROUPS)]
        + [zc(D_MODEL, LANES - N_GROUPS - N_EXPERTS)], 1)
    lam_init = 0.8 - 0.6 * math.exp(-0.3 * l)
    lv = diff_lambda[l].astype(F32)
    lam = jnp.exp(jnp.sum(lv[0] * lv[1])) - jnp.exp(jnp.sum(lv[2] * lv[3])) + lam_init
    return dict(
        w_a=wi[:, 0:COL_A].astype(BF16), w_b=w_b.astype(BF16), w_c=w_c.astype(BF16),
        w_uq=w_uq.astype(BF16), w_uk=w_uk.astype(BF16), w_uv=w_uv.astype(BF16),
        q_norm=mla_q_norm[l].reshape(1, B_Q_RANK), kv_norm=mla_kv_norm[l].reshape(1, B_KV_RANK),
        diff_sc=jnp.stack([lam, jnp.asarray(1.0 - lam_init, F32)]).astype(F32),
        subln=jnp.tile(diff_subln[l], A_HEADS).reshape(1, A_W),
        w_out=w_out[l].astype(BF16), ln1_g=ln1_g[l].reshape(1, D_MODEL), ln1_b=ln1_b[l].reshape(1, D_MODEL),
        w_router_hi=w_router.astype(BF16),
        w_router_lo=(w_router - w_router.astype(BF16).astype(F32)).astype(BF16),
        w1=w1[l], w3=w3[l], w2=w2[l],
        ln2_g=ln2_g[l].reshape(1, D_MODEL), ln2_b=ln2_b[l].reshape(1, D_MODEL))


def _mixer_half(x, tab, lw, batch, seq):
    qa, ka, va, qb, kb, vb, c0, c1, c2 = _in_proj(x, tab, lw, batch, seq)
    oa = _attn_a(lw['diff_sc'], qa, ka, va, lw['subln'], batch, seq)
    ob = _attn_b(qb, kb, vb, batch, seq)
    ocs, lses = zip(*[_attn_c(c, batch, seq, dil) for c, (_, dil) in zip((c0, c1, c2), C_GROUPS)])
    return _out_proj(x, oa, ob, ocs, lses, lw, seq)


def _moe_half(routed, lw):
    x1, x1t, ri, rw = routed
    blk_exp, slot_tok, pos = _dispatch_plan(ri[:, 0:TOP_K_INNER])
    ys = _experts(blk_exp, slot_tok, x1t, lw)
    return _combine(pos, x1, rw, ys, lw)


def kernel(x_prompt, x_sample, w_in, diff_lambda, diff_subln, mla_q_norm, mla_w_uq, mla_kv_norm, mla_w_ukv,
           w_out, ln1_g, ln1_b, moe_w_coarse, moe_w_fine, moe_w1, moe_w3, moe_w2, ln2_g, ln2_b):
    w1, w3, w2 = moe_w1.astype(BF16), moe_w3.astype(BF16), moe_w2.astype(BF16)
    layers = [_layer_weights(l, w_in, diff_lambda, diff_subln, mla_q_norm, mla_w_uq, mla_kv_norm, mla_w_ukv,
                             w_out, ln1_g, ln1_b, moe_w_coarse, moe_w_fine, w1, w3, w2, ln2_g, ln2_b)
              for l in range(DEPTH)]

    xs = (x_prompt, x_sample)
    dims = [x.shape[:2] for x in xs]
    tabs = [_rope_table_block(seq) for _, seq in dims]
    hs = [x.reshape(b * s, D_MODEL) for x, (b, s) in zip(xs, dims)]
    for lw in layers:
        routed = [_mixer_half(h, tab, lw, b, s) for h, tab, (b, s) in zip(hs, tabs, dims)]
        hs = [_moe_half(r, lw) for r in routed]
    return tuple(h.reshape(b, s, D_MODEL) for h, (b, s) in zip(hs, dims))
```
